```python
import math
import jax
import jax.numpy as jnp
from jax import lax
import numpy as np


D_MODEL = 1024
BATCH = 16
SEQ = 2048
DEPTH = 2

MEM_LEN = 256
CHUNK = 128
SGU_GROUPS = 4
SGU_GROUP_DIM = 128
SGU_WIDTH = SGU_GROUPS * SGU_GROUP_DIM
DIFF_HEADS = 4
DIFF_HEAD_DIM = 64
DIFF_V_DIM = 2 * DIFF_HEAD_DIM
DIFF_QK_WIDTH = DIFF_HEADS * 2 * DIFF_HEAD_DIM
DIFF_V_WIDTH = DIFF_HEADS * DIFF_V_DIM
Q_BLOCK = 128
IN_EVEN = 2 * SGU_WIDTH + 2 * DIFF_QK_WIDTH + DIFF_V_WIDTH
MIX_EVEN = SGU_WIDTH + DIFF_V_WIDTH
S5_WIDTH = D_MODEL
S5_GROUP = 16
S5_GROUPS = S5_WIDTH // S5_GROUP
S5_STATE = 64
DT_MIN = 0.001
DT_MAX = 0.1
X_HEADS = 4
X_HEAD_DIM = D_MODEL // X_HEADS
N_EXPERTS = 32
TOP_K = 4
D_FF = D_MODEL
SWIGLU_LIMIT = 7.0
SWIGLU_ALPHA = 1.702
MOE_BLOCK = 256
ALPHA = (2 * DEPTH) ** 0.25
BETA = (8 * DEPTH) ** -0.25
LN_EPS = 1e-5
NEG_INF = -1e30

kernel_name = 'hybrid_sgu_diffattn_s5_moe_block'


def layer_norm(x, g, b):
    xf = x.astype(jnp.float32)
    mu = jnp.mean(xf, axis=-1, keepdims=True)
    var = jnp.mean(jnp.square(xf - mu), axis=-1, keepdims=True)
    return ((xf - mu) * lax.rsqrt(var + LN_EPS)).astype(x.dtype) * g + b


def rms_norm(x, g):
    xf = x.astype(jnp.float32)
    return (xf * lax.rsqrt(jnp.mean(jnp.square(xf), axis=-1, keepdims=True) + LN_EPS)).astype(x.dtype) * g


def diff_lambda_init(layer_idx):
    return 0.8 - 0.6 * math.exp(-0.3 * layer_idx)


def spatial_gating(u, v, ln_g, ln_b, w_spatial, b_spatial):
    b, s, _ = u.shape
    u = jax.nn.gelu(u)
    v = jax.nn.gelu(v).reshape(b, s, SGU_GROUPS, SGU_GROUP_DIM)
    v = layer_norm(v, ln_g.reshape(SGU_GROUPS, SGU_GROUP_DIM), ln_b.reshape(SGU_GROUPS, SGU_GROUP_DIM))
    v = v.reshape(b, s // CHUNK, CHUNK, SGU_GROUPS, SGU_GROUP_DIM)
    w = jnp.tril(w_spatial)
    gate = jnp.einsum('gts,bcsgd->bctgd', w, v) + b_spatial.T[None, None, :, :, None]
    return u * gate.reshape(b, s, SGU_WIDTH)


def diff_attention(q, k, v, lam_q1, lam_k1, lam_q2, lam_k2, subln_g, lam_init):
    b, s, _ = q.shape
    q = q.reshape(b, s, DIFF_HEADS, 2, DIFF_HEAD_DIM)
    k = k.reshape(b, s, DIFF_HEADS, 2, DIFF_HEAD_DIM)
    v = v.reshape(b, s, DIFF_HEADS, DIFF_V_DIM)
    lam = (jnp.exp(jnp.sum(lam_q1 * lam_k1).astype(jnp.float32))
           - jnp.exp(jnp.sum(lam_q2 * lam_k2).astype(jnp.float32)) + lam_init)
    n_blk = s // Q_BLOCK
    q_blocks = jnp.moveaxis(q.reshape(b, n_blk, Q_BLOCK, DIFF_HEADS, 2, DIFF_HEAD_DIM), 1, 0)
    k_pos = jnp.arange(s)
    scale = DIFF_HEAD_DIM ** -0.5

    def attend_block(args):
        qb, blk = args
        q_pos = blk * Q_BLOCK + jnp.arange(Q_BLOCK)
        mask = k_pos[None, :] <= q_pos[:, None]
        sc = jnp.einsum('bqhcd,bkhcd->bhcqk', qb, k).astype(jnp.float32) * scale
        p = jax.nn.softmax(jnp.where(mask, sc, NEG_INF), axis=-1)
        attn = p[:, :, 0] - lam * p[:, :, 1]
        return jnp.einsum('bhqk,bkhe->bqhe', attn.astype(v.dtype), v)

    o = lax.map(attend_block, (q_blocks, jnp.arange(n_blk)))
    o = jnp.moveaxis(o, 0, 1).reshape(b, s, DIFF_HEADS, DIFF_V_DIM)
    o = rms_norm(o, subln_g) * (1.0 - lam_init)
    return o.reshape(b, s, DIFF_V_WIDTH)


def even_mixer(x, w_in, sgu_ln_g, sgu_ln_b, w_spatial, b_spatial,
               lam_q1, lam_k1, lam_q2, lam_k2, subln_g, w_out, lam_init):
    h = x @ w_in
    u, v_s, q, k, v_d = jnp.split(
        h, [SGU_WIDTH, 2 * SGU_WIDTH, 2 * SGU_WIDTH + DIFF_QK_WIDTH,
            2 * SGU_WIDTH + 2 * DIFF_QK_WIDTH], axis=-1)
    a = spatial_gating(u, v_s, sgu_ln_g, sgu_ln_b, w_spatial, b_spatial)
    d = diff_attention(q, k, v_d, lam_q1, lam_k1, lam_q2, lam_k2, subln_g, lam_init)
    return jnp.concatenate([a, d], axis=-1) @ w_out


def s5_mixer(x, w_in, log_dt, lambda_re, lambda_im, b_re, b_im, c_re, c_im, d_skip, w_val, w_gate):
    b, s, _ = x.shape
    u = (x @ w_in).reshape(b, s, S5_GROUPS, S5_GROUP).astype(jnp.float32)
    dt = jnp.exp(log_dt.astype(jnp.float32))[:, None]
    lr = lambda_re.astype(jnp.float32)
    li = lambda_im.astype(jnp.float32)
    mag = jnp.exp(lr * dt)
    ar = mag * jnp.cos(li * dt)
    ai = mag * jnp.sin(li * dt)
    den = lr * lr + li * li
    zr = ((ar - 1.0) * lr + ai * li) / den
    zi = (ai * lr - (ar - 1.0) * li) / den
    br = b_re.astype(jnp.float32)
    bi = b_im.astype(jnp.float32)
    bbar_re = zr[..., None] * br - zi[..., None] * bi
    bbar_im = zr[..., None] * bi + zi[..., None] * br
    bu_re = jnp.einsum('bsgc,gpc->bsgp', u, bbar_re)
    bu_im = jnp.einsum('bsgc,gpc->bsgp', u, bbar_im)
    a_re = jnp.broadcast_to(ar, (1, s, S5_GROUPS, S5_STATE))
    a_im = jnp.broadcast_to(ai, (1, s, S5_GROUPS, S5_STATE))

    def combine(e1, e2):
        a1r, a1i, b1r, b1i = e1
        a2r, a2i, b2r, b2i = e2
        return (a2r * a1r - a2i * a1i,
                a2r * a1i + a2i * a1r,
                a2r * b1r - a2i * b1i + b2r,
                a2r * b1i + a2i * b1r + b2i)

    _, _, xr, xi = lax.associative_scan(combine, (a_re, a_im, bu_re, bu_im), axis=1)
    y = (jnp.einsum('bsgp,gcp->bsgc', xr, c_re.astype(jnp.float32))
         - jnp.einsum('bsgp,gcp->bsgc', xi, c_im.astype(jnp.float32))
         + d_skip.astype(jnp.float32) * u)
    y = jax.nn.gelu(y.reshape(b, s, S5_WIDTH)).astype(x.dtype)
    return (y @ w_val) * jax.nn.sigmoid(y @ w_gate)


def memory_cross_attention(x, k_mem, v_mem, w_q, w_o):
    b, s, _ = x.shape
    q = (x @ w_q).reshape(b, s, X_HEADS, X_HEAD_DIM)
    sc = jnp.einsum('bshd,bmhd->bhsm', q, k_mem).astype(jnp.float32) * (X_HEAD_DIM ** -0.5)
    p = jax.nn.softmax(sc, axis=-1)
    o = jnp.einsum('bhsm,bmhd->bshd', p.astype(v_mem.dtype), v_mem).reshape(b, s, D_MODEL)
    return o @ w_o


def moe_ffn(x, router_w, router_b, w_up, b_up, w_down, b_down):
    b, s, d = x.shape
    n_tok = b * s
    x2d = x.reshape(n_tok, d)
    logits = (x2d @ router_w + router_b).astype(jnp.float32)
    top_logit, top_idx = lax.top_k(logits, TOP_K)
    gates = jax.nn.softmax(top_logit, axis=-1).astype(x.dtype)
    n_assign = n_tok * TOP_K
    flat_e = top_idx.reshape(n_assign)
    flat_t = jnp.repeat(jnp.arange(n_tok, dtype=jnp.int32), TOP_K)
    flat_g = gates.reshape(n_assign)
    order = jnp.argsort(flat_e)
    se, st, sg = flat_e[order], flat_t[order], flat_g[order]
    counts = jnp.bincount(flat_e, length=N_EXPERTS)
    start = jnp.cumsum(counts) - counts
    padded = (counts + MOE_BLOCK - 1) // MOE_BLOCK * MOE_BLOCK
    pend = jnp.cumsum(padded)
    pstart = pend - padded
    dest = pstart[se] + (jnp.arange(n_assign) - start[se])
    n_blocks = -(-n_assign // MOE_BLOCK) + N_EXPERTS
    n_pad = n_blocks * MOE_BLOCK
    buf_tok = jnp.zeros((n_pad,), jnp.int32).at[dest].set(st)
    buf_gate = jnp.zeros((n_pad,), x.dtype).at[dest].set(sg)
    blk_expert = jnp.minimum(
        jnp.searchsorted(pend, jnp.arange(n_blocks) * MOE_BLOCK, side='right'), N_EXPERTS - 1)

    def expert_block(args):
        tok, gate, e = args
        h = x2d[tok] @ w_up[e] + b_up[e]
        h_glu, h_lin = jnp.split(h, 2, axis=-1)
        h_glu = jnp.minimum(h_glu, SWIGLU_LIMIT)
        h_lin = jnp.clip(h_lin, -SWIGLU_LIMIT, SWIGLU_LIMIT)
        act = h_glu * jax.nn.sigmoid(SWIGLU_ALPHA * h_glu) * (h_lin + 1.0)
        return (act @ w_down[e] + b_down[e]) * gate[:, None]

    ys = lax.map(expert_block, (buf_tok.reshape(n_blocks, MOE_BLOCK),
                                buf_gate.reshape(n_blocks, MOE_BLOCK), blk_expert))
    y = jnp.zeros_like(x2d).at[buf_tok].add(ys.reshape(n_pad, d))
    return y.reshape(b, s, d)


def setup_inputs(seed: int = 0) -> dict:
    keys = iter(jax.random.split(jax.random.key(seed), 128))

    def nrm(shape, scale):
        return scale * jax.random.normal(next(keys), shape, jnp.float32)

    def gain(shape):
        return 1.0 + nrm(shape, 0.02)

    def small(shape):
        return nrm(shape, 0.01)

    fan = D_MODEL ** -0.5
    p = {}
    p['x'] = nrm((BATCH, SEQ, D_MODEL), 1.0)
    p['mem'] = nrm((BATCH, MEM_LEN, D_MODEL), 1.0)
    p['w_mem_kv'] = nrm((D_MODEL, 2 * D_MODEL), fan)

    def common(pre):
        p[pre + 'ln1_g'] = gain((D_MODEL,))
        p[pre + 'ln1_b'] = small((D_MODEL,))
        p[pre + 'xq'] = nrm((D_MODEL, D_MODEL), fan)
        p[pre + 'xo'] = nrm((D_MODEL, D_MODEL), BETA * fan)
        p[pre + 'ln2_g'] = gain((D_MODEL,))
        p[pre + 'ln2_b'] = small((D_MODEL,))
        p[pre + 'router_w'] = nrm((D_MODEL, N_EXPERTS), fan)
        p[pre + 'router_b'] = small((N_EXPERTS,))
        p[pre + 'exp_w_up'] = nrm((N_EXPERTS, D_MODEL, 2 * D_FF), fan)
        p[pre + 'exp_b_up'] = small((N_EXPERTS, 2 * D_FF))
        p[pre + 'exp_w_down'] = nrm((N_EXPERTS, D_FF, D_MODEL), BETA * D_FF ** -0.5)
        p[pre + 'exp_b_down'] = small((N_EXPERTS, D_MODEL))
        p[pre + 'ln3_g'] = gain((D_MODEL,))
        p[pre + 'ln3_b'] = small((D_MODEL,))

    p['l0_w_in'] = nrm((D_MODEL, IN_EVEN), fan)
    p['l0_sgu_ln_g'] = gain((SGU_WIDTH,))
    p['l0_sgu_ln_b'] = small((SGU_WIDTH,))
    p['l0_w_spatial'] = nrm((SGU_GROUPS, CHUNK, CHUNK), 0.5 * CHUNK ** -0.5)
    p['l0_b_spatial'] = gain((SGU_GROUPS, CHUNK))
    p['l0_lam_q1'] = nrm((DIFF_HEAD_DIM,), 0.1)
    p['l0_lam_k1'] = nrm((DIFF_HEAD_DIM,), 0.1)
    p['l0_lam_q2'] = nrm((DIFF_HEAD_DIM,), 0.1)
    p['l0_lam_k2'] = nrm((DIFF_HEAD_DIM,), 0.1)
    p['l0_subln_g'] = gain((DIFF_V_DIM,))
    p['l0_w_out'] = nrm((MIX_EVEN, D_MODEL), BETA * MIX_EVEN ** -0.5)
    common('l0_')
    p['l1_w_in'] = nrm((D_MODEL, S5_WIDTH), fan)
    p['l1_log_dt'] = jax.random.uniform(next(keys), (S5_GROUPS,), jnp.float32,
                                        minval=math.log(DT_MIN), maxval=math.log(DT_MAX))
    p['l1_lambda_re'] = -0.5 + small((S5_GROUPS, S5_STATE))
    p['l1_lambda_im'] = (math.pi * jnp.arange(S5_STATE, dtype=jnp.float32))[None, :] + small((S5_GROUPS, S5_STATE))
    p['l1_b_re'] = nrm((S5_GROUPS, S5_STATE, S5_GROUP), (2 * S5_GROUP) ** -0.5)
    p['l1_b_im'] = nrm((S5_GROUPS, S5_STATE, S5_GROUP), (2 * S5_GROUP) ** -0.5)
    p['l1_c_re'] = nrm((S5_GROUPS, S5_GROUP, S5_STATE), 0.5)
    p['l1_c_im'] = nrm((S5_GROUPS, S5_GROUP, S5_STATE), 0.5)
    p['l1_d_skip'] = nrm((S5_GROUPS, S5_GROUP), 1.0)
    p['l1_w_val'] = nrm((S5_WIDTH, D_MODEL), BETA * S5_WIDTH ** -0.5)
    p['l1_w_gate'] = nrm((S5_WIDTH, D_MODEL), S5_WIDTH ** -0.5)
    common('l1_')
    return p


def reference(x, mem, w_mem_kv,
              l0_w_in, l0_sgu_ln_g, l0_sgu_ln_b, l0_w_spatial, l0_b_spatial,
              l0_lam_q1, l0_lam_k1, l0_lam_q2, l0_lam_k2, l0_subln_g, l0_w_out,
              l0_ln1_g, l0_ln1_b, l0_xq, l0_xo, l0_ln2_g, l0_ln2_b,
              l0_router_w, l0_router_b, l0_exp_w_up, l0_exp_b_up, l0_exp_w_down, l0_exp_b_down,
              l0_ln3_g, l0_ln3_b,
              l1_w_in, l1_log_dt, l1_lambda_re, l1_lambda_im, l1_b_re, l1_b_im, l1_c_re, l1_c_im,
              l1_d_skip, l1_w_val, l1_w_gate,
              l1_ln1_g, l1_ln1_b, l1_xq, l1_xo, l1_ln2_g, l1_ln2_b,
              l1_router_w, l1_router_b, l1_exp_w_up, l1_exp_b_up, l1_exp_w_down, l1_exp_b_down,
              l1_ln3_g, l1_ln3_b):
    b = mem.shape[0]
    kv = mem @ w_mem_kv
    k_mem = kv[..., :D_MODEL].reshape(b, MEM_LEN, X_HEADS, X_HEAD_DIM)
    v_mem = kv[..., D_MODEL:].reshape(b, MEM_LEN, X_HEADS, X_HEAD_DIM)

    layers = [
        dict(mixer=(l0_w_in, l0_sgu_ln_g, l0_sgu_ln_b, l0_w_spatial, l0_b_spatial,
                    l0_lam_q1, l0_lam_k1, l0_lam_q2, l0_lam_k2, l0_subln_g, l0_w_out),
             ln1=(l0_ln1_g, l0_ln1_b), cross=(l0_xq, l0_xo), ln2=(l0_ln2_g, l0_ln2_b),
             moe=(l0_router_w, l0_router_b, l0_exp_w_up, l0_exp_b_up, l0_exp_w_down, l0_exp_b_down),
             ln3=(l0_ln3_g, l0_ln3_b)),
        dict(mixer=(l1_w_in, l1_log_dt, l1_lambda_re, l1_lambda_im, l1_b_re, l1_b_im,
                    l1_c_re, l1_c_im, l1_d_skip, l1_w_val, l1_w_gate),
             ln1=(l1_ln1_g, l1_ln1_b), cross=(l1_xq, l1_xo), ln2=(l1_ln2_g, l1_ln2_b),
             moe=(l1_router_w, l1_router_b, l1_exp_w_up, l1_exp_b_up, l1_exp_w_down, l1_exp_b_down),
             ln3=(l1_ln3_g, l1_ln3_b)),
    ]
    for i in range(DEPTH):
        prm = layers[i]
        if i % 2 == 0:
            h = even_mixer(x, *prm['mixer'], lam_init=diff_lambda_init(i))
        else:
            h = s5_mixer(x, *prm['mixer'])
        x = layer_norm(ALPHA * x + h, *prm['ln1'])
        x = layer_norm(ALPHA * x + memory_cross_attention(x, k_mem, v_mem, *prm['cross']), *prm['ln2'])
        x = layer_norm(ALPHA * x + moe_ffn(x, *prm['moe']), *prm['ln3'])
    return x
```

```python
import functools
import math

import jax
import jax.numpy as jnp
from jax import lax
from jax.experimental import pallas as pl
from jax.experimental.pallas import tpu as pltpu

F32 = jnp.float32
BF16 = jnp.bfloat16

D_MODEL = 1024
CHUNK = 128
SGU_GROUPS = 4
SGU_WIDTH = 512
DIFF_HEADS = 4
DIFF_HEAD_DIM = 64
DIFF_V_DIM = 128
S5_GROUP = 16
S5_GROUPS = 64
S5_STATE = 64
X_HEADS = 4
X_HEAD_DIM = 256
N_EXPERTS = 32
TOP_K = 4
D_FF = 1024
SWIGLU_LIMIT = 7.0
SWIGLU_ALPHA = 1.702
MOE_BLOCK = 256
DEPTH = 2
ALPHA = (2 * DEPTH) ** 0.25
LN_EPS = 1e-5
NEG_INF = -1e30

LANES = 128
VMEM_LIMIT_BYTES = 56 * 1024 * 1024

S5_CHUNK_GROUPS = 8
S5_CHUNK_IN = S5_CHUNK_GROUPS * S5_GROUP
S5_CHUNK_STATE = S5_CHUNK_GROUPS * S5_STATE
S5_N_CHUNKS = S5_GROUPS // S5_CHUNK_GROUPS
S5_STATES = S5_GROUPS * S5_STATE


def _params(*sem):
    return pltpu.CompilerParams(dimension_semantics=sem, vmem_limit_bytes=VMEM_LIMIT_BYTES)


def _gelu(x):
    return 0.5 * x * (1.0 + jnp.tanh(math.sqrt(2.0 / math.pi) * (x + 0.044715 * (x * x * x))))


def _sigmoid(x):
    return 1.0 / (1.0 + jnp.exp(-x))


def _layer_norm(z, g, b):
    mu = jnp.mean(z, axis=-1, keepdims=True)
    zc = z - mu
    var = jnp.mean(zc * zc, axis=-1, keepdims=True)
    return zc * lax.rsqrt(var + LN_EPS) * g + b


def _dot(a, b):
    return jnp.dot(a, b, preferred_element_type=F32)


def _dot_nt(a, b):
    return lax.dot_general(a, b, (((1,), (1,)), ((), ())), preferred_element_type=F32)


def _matmul_kernel(x_ref, w_ref, o_ref):
    o_ref[...] = _dot(x_ref[...].astype(BF16), w_ref[...]).astype(o_ref.dtype)


def _matmul(x, w, out_dtype, tm, tn):
    m, k = x.shape
    n = w.shape[1]
    return pl.pallas_call(
        _matmul_kernel,
        out_shape=jax.ShapeDtypeStruct((m, n), out_dtype),
        grid=(n // tn, m // tm),
        in_specs=[pl.BlockSpec((tm, k), lambda j, i: (i, 0)),
                  pl.BlockSpec((k, tn), lambda j, i: (0, j))],
        out_specs=pl.BlockSpec((tm, tn), lambda j, i: (i, j)),
        compiler_params=_params("parallel", "parallel"),
        name="matmul",
    )(x, w)


def _inproj_sgu_kernel(x_ref, w_ref, lng_ref, lnb_ref, wsp_ref, bsp_ref, a_ref, qkv_ref, *, tm):
    xb = x_ref[...].astype(BF16)
    qkv_ref[...] = _dot(xb, w_ref[:, 2 * SGU_WIDTH:]).astype(BF16)
    h = _dot(xb, w_ref[:, :2 * SGU_WIDTH])
    for g in range(SGU_GROUPS):
        lo = g * CHUNK
        u = _gelu(h[:, lo:lo + CHUNK])
        v = _gelu(h[:, SGU_WIDTH + lo:SGU_WIDTH + lo + CHUNK])
        vn = _layer_norm(v, lng_ref[g:g + 1, :], lnb_ref[g:g + 1, :]).astype(BF16)
        w_g = wsp_ref[g]
        b_g = bsp_ref[:, g:g + 1]
        for c in range(tm // CHUNK):
            r = c * CHUNK
            gate = _dot(w_g, vn[r:r + CHUNK, :]) + b_g
            a_ref[r:r + CHUNK, lo:lo + CHUNK] = (u[r:r + CHUNK, :] * gate).astype(BF16)


def _inproj_sgu(x2d, w_in, ln_g, ln_b, w_spatial, b_spatial, tm=256):
    n, d = x2d.shape
    w = w_in.astype(BF16)
    wsp = jnp.tril(w_spatial).astype(BF16)
    bsp_t = b_spatial.T
    n_qkv = w.shape[1] - 2 * SGU_WIDTH
    const = lambda i: (0, 0)
    return pl.pallas_call(
        functools.partial(_inproj_sgu_kernel, tm=tm),
        out_shape=(jax.ShapeDtypeStruct((n, SGU_WIDTH), BF16),
                   jax.ShapeDtypeStruct((n, n_qkv), BF16)),
        grid=(n // tm,),
        in_specs=[pl.BlockSpec((tm, d), lambda i: (i, 0)),
                  pl.BlockSpec(w.shape, const),
                  pl.BlockSpec((SGU_GROUPS, CHUNK), const),
                  pl.BlockSpec((SGU_GROUPS, CHUNK), const),
                  pl.BlockSpec(wsp.shape, lambda i: (0, 0, 0)),
                  pl.BlockSpec(bsp_t.shape, const)],
        out_specs=(pl.BlockSpec((tm, SGU_WIDTH), lambda i: (i, 0)),
                   pl.BlockSpec((tm, n_qkv), lambda i: (i, 0))),
        compiler_params=_params("parallel"),
        name="inproj_sgu",
    )(x2d, w, ln_g.reshape(SGU_GROUPS, CHUNK), ln_b.reshape(SGU_GROUPS, CHUNK), wsp, bsp_t)


def _diff_attn_kernel(q_ref, k_ref, v_ref, lam_ref, g_ref, o_ref, *, tq, lam_init):
    i = pl.program_id(2)
    lane = lax.broadcasted_iota(jnp.int32, (1, 2 * DIFF_HEAD_DIM), 1)
    q = q_ref[...] * jnp.asarray(DIFF_HEAD_DIM ** -0.5, BF16)
    zero = jnp.zeros_like(q)
    q1 = jnp.where(lane < DIFF_HEAD_DIM, q, zero)
    q2 = jnp.where(lane >= DIFF_HEAD_DIM, q, zero)

    def update(state, s, vj):
        m, l, acc = state
        m_new = jnp.maximum(m, jnp.max(s, axis=-1, keepdims=True))
        alpha = jnp.exp(m - m_new)
        p = jnp.exp(s - m_new)
        l_new = alpha * l + jnp.sum(p, axis=-1, keepdims=True)
        acc_new = alpha * acc + _dot(p.astype(BF16), vj)
        return m_new, l_new, acc_new

    def block(j, carry, masked):
        st1, st2 = carry
        off = pl.multiple_of(j * tq, tq)
        kj = k_ref[pl.ds(off, tq), :]
        vj = v_ref[pl.ds(off, tq), :]
        s1 = _dot_nt(q1, kj)
        s2 = _dot_nt(q2, kj)
        if masked:
            row = lax.broadcasted_iota(jnp.int32, (tq, tq), 0)
            col = lax.broadcasted_iota(jnp.int32, (tq, tq), 1)
            keep = col <= row
            s1 = jnp.where(keep, s1, NEG_INF)
            s2 = jnp.where(keep, s2, NEG_INF)
        return update(st1, s1, vj), update(st2, s2, vj)

    init = (jnp.full((tq, 1), NEG_INF, F32), jnp.zeros((tq, 1), F32), jnp.zeros((tq, DIFF_V_DIM), F32))
    carry = lax.fori_loop(0, i, lambda j, c: block(j, c, False), (init, init))
    (_, l1, a1), (_, l2, a2) = block(i, carry, True)

    lam_v = lam_ref[...]
    s_a = jnp.sum(lam_v[0:1, :] * lam_v[1:2, :], axis=-1, keepdims=True)
    s_b = jnp.sum(lam_v[2:3, :] * lam_v[3:4, :], axis=-1, keepdims=True)
    lam = jnp.exp(s_a) - jnp.exp(s_b) + lam_init
    o = a1 / l1 - lam * (a2 / l2)
    o = o * lax.rsqrt(jnp.mean(o * o, axis=-1, keepdims=True) + LN_EPS) * g_ref[...] * (1.0 - lam_init)
    o_ref[...] = o.astype(BF16)


def _diff_attention(qkv, lam_q1, lam_k1, lam_q2, lam_k2, subln_g, lam_init, batch, seq, tq=256):
    n = qkv.shape[0]
    nq = seq // tq
    lam_v = jnp.stack([lam_q1, lam_k1, lam_q2, lam_k2])
    hw = 2 * DIFF_HEAD_DIM
    return pl.pallas_call(
        functools.partial(_diff_attn_kernel, tq=tq, lam_init=lam_init),
        out_shape=jax.ShapeDtypeStruct((n, DIFF_HEADS * DIFF_V_DIM), BF16),
        grid=(batch, DIFF_HEADS, nq),
        in_specs=[pl.BlockSpec((tq, hw), lambda b, h, i: (b * nq + i, h)),
                  pl.BlockSpec((seq, hw), lambda b, h, i: (b, DIFF_HEADS + h)),
                  pl.BlockSpec((seq, DIFF_V_DIM), lambda b, h, i: (b, 2 * DIFF_HEADS + h)),
                  pl.BlockSpec(lam_v.shape, lambda b, h, i: (0, 0)),
                  pl.BlockSpec((1, DIFF_V_DIM), lambda b, h, i: (0, 0))],
        out_specs=pl.BlockSpec((tq, DIFF_V_DIM), lambda b, h, i: (b * nq + i, h)),
        compiler_params=_params("parallel", "parallel", "parallel"),
        name="diff_attention",
    )(qkv, qkv, qkv, lam_v, subln_g.reshape(1, DIFF_V_DIM))


def _outproj_ln_kernel(a_ref, d_ref, w_ref, x_ref, g_ref, b_ref, o_ref):
    y = _dot(a_ref[...], w_ref[:SGU_WIDTH, :]) + _dot(d_ref[...], w_ref[SGU_WIDTH:, :])
    o_ref[...] = _layer_norm(ALPHA * x_ref[...] + y, g_ref[...], b_ref[...])


def _outproj_ln(a, dattn, w_out, x2d, ln_g, ln_b, tm=512):
    n, d = x2d.shape
    w = w_out.astype(BF16)
    const = lambda i: (0, 0)
    row = lambda i: (i, 0)
    return pl.pallas_call(
        _outproj_ln_kernel,
        out_shape=jax.ShapeDtypeStruct((n, d), F32),
        grid=(n // tm,),
        in_specs=[pl.BlockSpec((tm, a.shape[1]), row),
                  pl.BlockSpec((tm, dattn.shape[1]), row),
                  pl.BlockSpec(w.shape, const),
                  pl.BlockSpec((tm, d), row),
                  pl.BlockSpec((1, d), const),
                  pl.BlockSpec((1, d), const)],
        out_specs=pl.BlockSpec((tm, d), row),
        compiler_params=_params("parallel"),
        name="outproj_ln1",
    )(a, dattn, w, x2d, ln_g.reshape(1, d), ln_b.reshape(1, d))


def _cross_router_kernel(x_ref, wq_ref, wo_ref, k_ref, v_ref, g_ref, b_ref, rwh_ref, rwl_ref, rb_ref,
                         x2_ref, idx_ref, gate_ref):
    x = x_ref[...]
    q = (_dot(x.astype(BF16), wq_ref[...]) * (X_HEAD_DIM ** -0.5)).astype(BF16)
    heads = []
    for h in range(X_HEADS):
        lo = h * X_HEAD_DIM
        s = _dot_nt(q[:, lo:lo + X_HEAD_DIM], k_ref[:, lo:lo + X_HEAD_DIM])
        p = jnp.exp(s - jnp.max(s, axis=-1, keepdims=True))
        p = p / jnp.sum(p, axis=-1, keepdims=True)
        heads.append(_dot(p.astype(BF16), v_ref[:, lo:lo + X_HEAD_DIM]).astype(BF16))
    o = jnp.concatenate(heads, axis=-1)
    x2 = _layer_norm(ALPHA * x + _dot(o, wo_ref[...]), g_ref[...], b_ref[...])
    x2_ref[...] = x2

    x_hi = x2.astype(BF16)
    x_lo = (x2 - x_hi.astype(F32)).astype(BF16)
    logits = _dot(x_hi, rwh_ref[...]) + _dot(x_lo, rwh_ref[...]) + _dot(x_hi, rwl_ref[...]) + rb_ref[...]
    lane = lax.broadcasted_iota(jnp.int32, logits.shape, 1).astype(F32)
    work = logits
    top_v, top_i = [], []
    for _ in range(TOP_K):
        m = jnp.max(work, axis=-1, keepdims=True)
        sel = jnp.min(jnp.where(work == m, lane, float(LANES)), axis=-1, keepdims=True)
        top_v.append(m)
        top_i.append(sel)
        work = jnp.where(lane == sel, -jnp.inf, work)
    e = [jnp.exp(v - top_v[0]) for v in top_v]
    denom = e[0] + e[1] + e[2] + e[3]
    idx_out = jnp.zeros(logits.shape, F32)
    gate_out = jnp.zeros(logits.shape, F32)
    for k in range(TOP_K):
        idx_out = jnp.where(lane == float(k), top_i[k], idx_out)
        gate_out = jnp.where(lane == float(k), e[k] / denom, gate_out)
    idx_ref[...] = idx_out.astype(jnp.int32)
    gate_ref[...] = gate_out


def _cross_router(x1, kv, w_q, w_o, ln_g, ln_b, router_w, router_b, batch, seq, mem_len, tm=256):
    n, d = x1.shape
    nt = seq // tm
    pad = LANES - N_EXPERTS
    rw = jnp.pad(router_w, ((0, 0), (0, pad)))
    rw_hi = rw.astype(BF16)
    rw_lo = (rw - rw_hi.astype(F32)).astype(BF16)
    rb = jnp.pad(router_b, (0, pad), constant_values=-jnp.inf).reshape(1, LANES)
    const = lambda b, i: (0, 0)
    row = lambda b, i: (b * nt + i, 0)
    return pl.pallas_call(
        _cross_router_kernel,
        out_shape=(jax.ShapeDtypeStruct((n, d), F32),
                   jax.ShapeDtypeStruct((n, LANES), jnp.int32),
                   jax.ShapeDtypeStruct((n, LANES), F32)),
        grid=(batch, nt),
        in_specs=[pl.BlockSpec((tm, d), row),
                  pl.BlockSpec((d, d), const),
                  pl.BlockSpec((d, d), const),
                  pl.BlockSpec((mem_len, d), lambda b, i: (b, 0)),
                  pl.BlockSpec((mem_len, d), lambda b, i: (b, 1)),
                  pl.BlockSpec((1, d), const),
                  pl.BlockSpec((1, d), const),
                  pl.BlockSpec((d, LANES), const),
                  pl.BlockSpec((d, LANES), const),
                  pl.BlockSpec((1, LANES), const)],
        out_specs=(pl.BlockSpec((tm, d), row),
                   pl.BlockSpec((tm, LANES), row),
                   pl.BlockSpec((tm, LANES), row)),
        compiler_params=_params("parallel", "parallel"),
        name="cross_attn_router",
    )(x1, w_q.astype(BF16), w_o.astype(BF16), kv, kv, ln_g.reshape(1, d), ln_b.reshape(1, d), rw_hi, rw_lo, rb)


def _expert_kernel(be_ref, nused_ref, tok_ref, x_hbm, wup_ref, bup_ref, wdn_ref, bdn_ref, y_ref,
                   xbuf, wup_bf, wdn_bf, sem):
    i = pl.program_id(0)

    @pl.when(i >= nused_ref[0])
    def _():
        y_ref[...] = jnp.zeros_like(y_ref)

    @pl.when(i < nused_ref[0])
    def _():
        def row_copy(r):
            return pltpu.make_async_copy(x_hbm.at[pl.ds(tok_ref[0, 0, r], 1), :], xbuf.at[pl.ds(r, 1), :], sem)

        def start(r, c):
            row_copy(r).start()
            return c

        lax.fori_loop(0, MOE_BLOCK, start, 0)

        prev = be_ref[jnp.maximum(i - 1, 0)]

        @pl.when(jnp.logical_or(i == 0, be_ref[i] != prev))
        def _():
            wup_bf[...] = wup_ref[0].astype(BF16)
            wdn_bf[...] = wdn_ref[0].astype(BF16)

        def wait(r, c):
            row_copy(r).wait()
            return c

        lax.fori_loop(0, MOE_BLOCK, wait, 0)

        xb = xbuf[...].astype(BF16)
        h = _dot(xb, wup_bf[...]) + bup_ref[0]
        glu = jnp.minimum(h[:, :D_FF], SWIGLU_LIMIT)
        lin = jnp.clip(h[:, D_FF:], -SWIGLU_LIMIT, SWIGLU_LIMIT)
        act = glu * _sigmoid(SWIGLU_ALPHA * glu) * (lin + 1.0)
        y_ref[...] = _dot(act.astype(BF16), wdn_bf[...]) + bdn_ref[0]


def _experts(x2, buf_tok, blk_expert, n_used, w_up, b_up, w_down, b_down):
    n_blocks = blk_expert.shape[0]
    d = x2.shape[1]
    f2 = w_up.shape[2]
    grid_spec = pltpu.PrefetchScalarGridSpec(
        num_scalar_prefetch=2,
        grid=(n_blocks,),
        in_specs=[pl.BlockSpec((1, 1, MOE_BLOCK), lambda i, be, nu: (i, 0, 0), memory_space=pltpu.SMEM),
                  pl.BlockSpec(memory_space=pl.ANY),
                  pl.BlockSpec((1, d, f2), lambda i, be, nu: (be[i], 0, 0)),
                  pl.BlockSpec((1, 1, f2), lambda i, be, nu: (be[i], 0, 0)),
                  pl.BlockSpec((1, D_FF, d), lambda i, be, nu: (be[i], 0, 0)),
                  pl.BlockSpec((1, 1, d), lambda i, be, nu: (be[i], 0, 0))],
        out_specs=pl.BlockSpec((MOE_BLOCK, d), lambda i, be, nu: (i, 0)),
        scratch_shapes=[pltpu.VMEM((MOE_BLOCK, d), F32),
                        pltpu.VMEM((d, f2), BF16),
                        pltpu.VMEM((D_FF, d), BF16),
                        pltpu.SemaphoreType.DMA],
    )
    return pl.pallas_call(
        _expert_kernel,
        out_shape=jax.ShapeDtypeStruct((n_blocks * MOE_BLOCK, d), F32),
        grid_spec=grid_spec,
        compiler_params=_params("arbitrary"),
        name="moe_experts",
    )(blk_expert, n_used, buf_tok.reshape(n_blocks, 1, MOE_BLOCK), x2, w_up,
      b_up.reshape(N_EXPERTS, 1, f2), w_down, b_down.reshape(N_EXPERTS, 1, d))


def _combine_ln_kernel(dest_ref, ys_hbm, gate_ref, x_ref, g_ref, b_ref, o_ref, gbuf, sem, *, tm):
    def row_copy(j):
        k = j // tm
        r = j - k * tm
        return pltpu.make_async_copy(ys_hbm.at[pl.ds(dest_ref[0, 0, j], 1), :], gbuf.at[k, pl.ds(r, 1), :], sem)

    def start(j, c):
        row_copy(j).start()
        return c

    def wait(j, c):
        row_copy(j).wait()
        return c

    lax.fori_loop(0, TOP_K * tm, start, 0)
    lax.fori_loop(0, TOP_K * tm, wait, 0)
    gates = gate_ref[...]
    y = gates[:, 0:1] * gbuf[0]
    for k in range(1, TOP_K):
        y = y + gates[:, k:k + 1] * gbuf[k]
    o_ref[...] = _layer_norm(ALPHA * x_ref[...] + y, g_ref[...], b_ref[...])


def _combine_ln(ys, dest, gates, x2, ln_g, ln_b, tm=256):
    n, d = x2.shape
    nt = n // tm
    dest_t = dest.reshape(nt, tm, TOP_K).transpose(0, 2, 1).reshape(nt, 1, TOP_K * tm)
    const = lambda i: (0, 0)
    row = lambda i: (i, 0)
    return pl.pallas_call(
        functools.partial(_combine_ln_kernel, tm=tm),
        out_shape=jax.ShapeDtypeStruct((n, d), F32),
        grid=(nt,),
        in_specs=[pl.BlockSpec((1, 1, TOP_K * tm), lambda i: (i, 0, 0), memory_space=pltpu.SMEM),
                  pl.BlockSpec(memory_space=pl.ANY),
                  pl.BlockSpec((tm, LANES), row),
                  pl.BlockSpec((tm, d), row),
                  pl.BlockSpec((1, d), const),
                  pl.BlockSpec((1, d), const)],
        out_specs=pl.BlockSpec((tm, d), row),
        scratch_shapes=[pltpu.VMEM((TOP_K, tm, d), F32), pltpu.SemaphoreType.DMA],
        compiler_params=_params("arbitrary"),
        name="moe_combine_ln3",
    )(dest_t, ys, gates, x2, ln_g.reshape(1, d), ln_b.reshape(1, d))


def _moe_layer(x2, top_idx, gates_pad, w_up, b_up, w_down, b_down, ln_g, ln_b):
    n = x2.shape[0]
    n_assign = n * TOP_K
    n_blocks = n_assign // MOE_BLOCK + N_EXPERTS
    n_pad = n_blocks * MOE_BLOCK
    flat_e = top_idx.reshape(n_assign)
    flat_t = jnp.repeat(jnp.arange(n, dtype=jnp.int32), TOP_K)
    order = jnp.argsort(flat_e)
    se, st = flat_e[order], flat_t[order]
    counts = jnp.bincount(flat_e, length=N_EXPERTS)
    start = jnp.cumsum(counts) - counts
    padded = (counts + MOE_BLOCK - 1) // MOE_BLOCK * MOE_BLOCK
    pend = jnp.cumsum(padded)
    pstart = pend - padded
    dest_sorted = (pstart[se] + (jnp.arange(n_assign) - start[se])).astype(jnp.int32)
    buf_tok = jnp.zeros((n_pad,), jnp.int32).at[dest_sorted].set(st)
    dest = jnp.zeros((n_assign,), jnp.int32).at[order].set(dest_sorted).reshape(n, TOP_K)
    blk_expert = jnp.minimum(
        jnp.searchsorted(pend, jnp.arange(n_blocks) * MOE_BLOCK, side='right'), N_EXPERTS - 1).astype(jnp.int32)
    n_used = (pend[-1] // MOE_BLOCK).astype(jnp.int32).reshape(1)

    ys = _experts(x2, buf_tok, blk_expert, n_used, w_up, b_up, w_down, b_down)
    return _combine_ln(ys, dest, gates_pad, x2, ln_g, ln_b)


def _s5_kernel(x_ref, win_ref, bblk_ref, ar_ref, ai_ref, cblk_ref, dsk_ref, wval_ref, wgate_ref, g_ref, b_ref,
               o_ref, bur, bui, sr, si, *, tt, batch):
    rows = tt * batch

    @pl.when(pl.program_id(0) == 0)
    def _():
        sr[...] = jnp.zeros_like(sr)
        si[...] = jnp.zeros_like(si)

    x = x_ref[...]
    u = _dot(x.astype(BF16), win_ref[...])
    ub = u.astype(BF16)
    for c in range(S5_N_CHUNKS):
        bu = _dot(ub[:, c * S5_CHUNK_IN:(c + 1) * S5_CHUNK_IN], bblk_ref[c])
        bur[:, c * S5_CHUNK_STATE:(c + 1) * S5_CHUNK_STATE] = bu[:, :S5_CHUNK_STATE]
        bui[:, c * S5_CHUNK_STATE:(c + 1) * S5_CHUNK_STATE] = bu[:, S5_CHUNK_STATE:]

    for c in range(S5_N_CHUNKS):
        cols = pl.ds(c * S5_CHUNK_STATE, S5_CHUNK_STATE)
        a_r = ar_ref[:, cols]
        a_i = ai_ref[:, cols]

        def step(t, carry):
            s_r, s_i = carry
            rsl = pl.ds(pl.multiple_of(t * batch, batch), batch)
            n_r = a_r * s_r - a_i * s_i + bur[rsl, cols]
            n_i = a_r * s_i + a_i * s_r + bui[rsl, cols]
            bur[rsl, cols] = n_r
            bui[rsl, cols] = n_i
            return n_r, n_i

        f_r, f_i = lax.fori_loop(0, tt, step, (sr[:, cols], si[:, cols]), unroll=True)
        sr[:, cols] = f_r
        si[:, cols] = f_i

    ys = []
    for c in range(S5_N_CHUNKS):
        cols = pl.ds(c * S5_CHUNK_STATE, S5_CHUNK_STATE)
        xri = jnp.concatenate([bur[:, cols].astype(BF16), bui[:, cols].astype(BF16)], axis=-1)
        ys.append(_dot(xri, cblk_ref[c]))
    y = jnp.concatenate(ys, axis=-1) + dsk_ref[...] * u
    yb = _gelu(y).astype(BF16)
    hmix = _dot(yb, wval_ref[...]) * _sigmoid(_dot(yb, wgate_ref[...]))
    o_ref[...] = _layer_norm(ALPHA * x + hmix, g_ref[...], b_ref[...])


def _s5_discretize(log_dt, lambda_re, lambda_im, b_re, b_im, c_re, c_im):
    dt = jnp.exp(log_dt)[:, None]
    mag = jnp.exp(lambda_re * dt)
    ar = mag * jnp.cos(lambda_im * dt)
    ai = mag * jnp.sin(lambda_im * dt)
    den = lambda_re * lambda_re + lambda_im * lambda_im
    zr = ((ar - 1.0) * lambda_re + ai * lambda_im) / den
    zi = (ai * lambda_re - (ar - 1.0) * lambda_im) / den
    bbar_re = zr[..., None] * b_re - zi[..., None] * b_im
    bbar_im = zr[..., None] * b_im + zi[..., None] * b_re
    ng, gc, p, gw = S5_N_CHUNKS, S5_CHUNK_GROUPS, S5_STATE, S5_GROUP
    eye = jnp.eye(gc, dtype=F32)
    bre = bbar_re.reshape(ng, gc, p, gw).transpose(0, 1, 3, 2)
    bim = bbar_im.reshape(ng, gc, p, gw).transpose(0, 1, 3, 2)
    blk_re = jnp.einsum('cgip,gh->cgihp', bre, eye).reshape(ng, gc * gw, gc * p)
    blk_im = jnp.einsum('cgip,gh->cgihp', bim, eye).reshape(ng, gc * gw, gc * p)
    bblk = jnp.concatenate([blk_re, blk_im], axis=-1).astype(BF16)
    cre = c_re.reshape(ng, gc, gw, p).transpose(0, 1, 3, 2)
    cim = c_im.reshape(ng, gc, gw, p).transpose(0, 1, 3, 2)
    cblk_re = jnp.einsum('cgpi,gh->cgphi', cre, eye).reshape(ng, gc * p, gc * gw)
    cblk_im = jnp.einsum('cgpi,gh->cgphi', cim, eye).reshape(ng, gc * p, gc * gw)
    cblk = jnp.concatenate([cblk_re, -cblk_im], axis=1).astype(BF16)
    return bblk, ar.reshape(1, S5_STATES), ai.reshape(1, S5_STATES), cblk


def _s5_mixer_ln(x_tm, w_in, log_dt, lambda_re, lambda_im, b_re, b_im, c_re, c_im, d_skip, w_val, w_gate,
                 ln_g, ln_b, batch, tt=16):
    n, d = x_tm.shape
    rows = tt * batch
    bblk, ar, ai, cblk = _s5_discretize(log_dt, lambda_re, lambda_im, b_re, b_im, c_re, c_im)
    const2 = lambda i: (0, 0)
    const3 = lambda i: (0, 0, 0)
    row = lambda i: (i, 0)
    return pl.pallas_call(
        functools.partial(_s5_kernel, tt=tt, batch=batch),
        out_shape=jax.ShapeDtypeStruct((n, d), F32),
        grid=(n // rows,),
        in_specs=[pl.BlockSpec((rows, d), row),
                  pl.BlockSpec((d, d), const2),
                  pl.BlockSpec(bblk.shape, const3),
                  pl.BlockSpec(ar.shape, const2),
                  pl.BlockSpec(ai.shape, const2),
                  pl.BlockSpec(cblk.shape, const3),
                  pl.BlockSpec((1, d), const2),
                  pl.BlockSpec((d, d), const2),
                  pl.BlockSpec((d, d), const2),
                  pl.BlockSpec((1, d), const2),
                  pl.BlockSpec((1, d), const2)],
        out_specs=pl.BlockSpec((rows, d), row),
        scratch_shapes=[pltpu.VMEM((rows, S5_STATES), F32),
                        pltpu.VMEM((rows, S5_STATES), F32),
                        pltpu.VMEM((batch, S5_STATES), F32),
                        pltpu.VMEM((batch, S5_STATES), F32)],
        compiler_params=_params("arbitrary"),
        name="s5_mixer_ln1",
    )(x_tm, w_in.astype(BF16), bblk, ar, ai, cblk, d_skip.reshape(1, d), w_val.astype(BF16), w_gate.astype(BF16),
      ln_g.reshape(1, d), ln_b.reshape(1, d))


def _diff_lambda_init(layer_idx):
    return 0.8 - 0.6 * math.exp(-0.3 * layer_idx)


def kernel(x, mem, w_mem_kv, l0_w_in, l0_sgu_ln_g, l0_sgu_ln_b, l0_w_spatial, l0_b_spatial, l0_lam_q1, l0_lam_k1, l0_lam_q2, l0_lam_k2, l0_subln_g, l0_w_out, l0_ln1_g, l0_ln1_b, l0_xq, l0_xo, l0_ln2_g, l0_ln2_b, l0_router_w, l0_router_b, l0_exp_w_up, l0_exp_b_up, l0_exp_w_down, l0_exp_b_down, l0_ln3_g, l0_ln3_b, l1_w_in, l1_log_dt, l1_lambda_re, l1_lambda_im, l1_b_re, l1_b_im, l1_c_re, l1_c_im, l1_d_skip, l1_w_val, l1_w_gate, l1_ln1_g, l1_ln1_b, l1_xq, l1_xo, l1_ln2_g, l1_ln2_b, l1_router_w, l1_router_b, l1_exp_w_up, l1_exp_b_up, l1_exp_w_down, l1_exp_b_down, l1_ln3_g, l1_ln3_b):
    batch, seq, d = x.shape
    mem_len = mem.shape[1]
    n = batch * seq
    x0 = x.reshape(n, d)

    kv = _matmul(mem.reshape(batch * mem_len, d), w_mem_kv.astype(BF16), BF16,
                 tm=min(512, batch * mem_len), tn=d)

    a, qkv = _inproj_sgu(x0, l0_w_in, l0_sgu_ln_g, l0_sgu_ln_b, l0_w_spatial, l0_b_spatial)
    dattn = _diff_attention(qkv, l0_lam_q1, l0_lam_k1, l0_lam_q2, l0_lam_k2, l0_subln_g,
                            _diff_lambda_init(0), batch, seq)
    x1 = _outproj_ln(a, dattn, l0_w_out, x0, l0_ln1_g, l0_ln1_b)
    x2, idx, gates = _cross_router(x1, kv, l0_xq, l0_xo, l0_ln2_g, l0_ln2_b, l0_router_w, l0_router_b,
                                   batch, seq, mem_len)
    x3 = _moe_layer(x2, idx[:, :TOP_K], gates, l0_exp_w_up, l0_exp_b_up, l0_exp_w_down, l0_exp_b_down,
                    l0_ln3_g, l0_ln3_b)

    x3_tm = x3.reshape(batch, seq, d).transpose(1, 0, 2).reshape(n, d)
    x4_tm = _s5_mixer_ln(x3_tm, l1_w_in, l1_log_dt, l1_lambda_re, l1_lambda_im, l1_b_re, l1_b_im,
                         l1_c_re, l1_c_im, l1_d_skip, l1_w_val, l1_w_gate, l1_ln1_g, l1_ln1_b, batch)
    x4 = x4_tm.reshape(seq, batch, d).transpose(1, 0, 2).reshape(n, d)
    x5, idx, gates = _cross_router(x4, kv, l1_xq, l1_xo, l1_ln2_g, l1_ln2_b, l1_router_w, l1_router_b,
                                   batch, seq, mem_len)
    x6 = _moe_layer(x5, idx[:, :TOP_K], gates, l1_exp_w_up, l1_exp_b_up, l1_exp_w_down, l1_exp_b_down,
                    l1_ln3_g, l1_ln3_b)
    return x6.reshape(batch, seq, d)
```

```python
import functools
import math

import jax
import jax.numpy as jnp
from jax import lax
from jax.experimental import pallas as pl
from jax.experimental.pallas import tpu as pltpu

F32 = jnp.float32
BF16 = jnp.bfloat16

D_MODEL = 1024
CHUNK = 128
SGU_GROUPS = 4
SGU_WIDTH = 512
DIFF_HEADS = 4
DIFF_HEAD_DIM = 64
DIFF_V_DIM = 128
S5_GROUP = 16
S5_GROUPS = 64
S5_STATE = 64
X_HEADS = 4
X_HEAD_DIM = 256
N_EXPERTS = 32
TOP_K = 4
D_FF = 1024
SWIGLU_LIMIT = 7.0
SWIGLU_ALPHA = 1.702
MOE_BLOCK = 256
DEPTH = 2
ALPHA = (2 * DEPTH) ** 0.25
LN_EPS = 1e-5
NEG_INF = -1e30

LANES = 128
SUBLANES = 8
VMEM_LIMIT_BYTES = 56 * 1024 * 1024

S5_CHUNK_GROUPS = 8
S5_CHUNK_IN = S5_CHUNK_GROUPS * S5_GROUP
S5_CHUNK_STATE = S5_CHUNK_GROUPS * S5_STATE
S5_N_CHUNKS = S5_GROUPS // S5_CHUNK_GROUPS
S5_STATES = S5_GROUPS * S5_STATE


def _params(*sem):
    return pltpu.CompilerParams(dimension_semantics=sem, vmem_limit_bytes=VMEM_LIMIT_BYTES)


def _gelu(x):
    return 0.5 * x * (1.0 + jnp.tanh(math.sqrt(2.0 / math.pi) * (x + 0.044715 * (x * x * x))))


def _sigmoid(x):
    return 1.0 / (1.0 + jnp.exp(-x))


def _layer_norm(z, g, b):
    mu = jnp.mean(z, axis=-1, keepdims=True)
    zc = z - mu
    var = jnp.mean(zc * zc, axis=-1, keepdims=True)
    return zc * lax.rsqrt(var + LN_EPS) * g + b


def _dot(a, b):
    return jnp.dot(a, b, preferred_element_type=F32)


def _dot_nt(a, b):
    return lax.dot_general(a, b, (((1,), (1,)), ((), ())), preferred_element_type=F32)


def _matmul_kernel(x_ref, w_ref, o_ref):
    o_ref[...] = _dot(x_ref[...].astype(BF16), w_ref[...]).astype(o_ref.dtype)


def _matmul(x, w, out_dtype, tm, tn):
    m, k = x.shape
    n = w.shape[1]
    return pl.pallas_call(
        _matmul_kernel,
        out_shape=jax.ShapeDtypeStruct((m, n), out_dtype),
        grid=(n // tn, m // tm),
        in_specs=[pl.BlockSpec((tm, k), lambda j, i: (i, 0)),
                  pl.BlockSpec((k, tn), lambda j, i: (0, j))],
        out_specs=pl.BlockSpec((tm, tn), lambda j, i: (i, j)),
        compiler_params=_params("parallel", "parallel"),
        name="matmul",
    )(x, w)


def _inproj_sgu_kernel(x_ref, w_ref, lng_ref, lnb_ref, wsp_ref, bsp_ref, a_ref, qkv_ref, *, tm):
    xb = x_ref[...].astype(BF16)
    qkv_ref[...] = _dot(xb, w_ref[:, 2 * SGU_WIDTH:]).astype(BF16)
    h = _dot(xb, w_ref[:, :2 * SGU_WIDTH])
    for g in range(SGU_GROUPS):
        lo = g * CHUNK
        u = _gelu(h[:, lo:lo + CHUNK])
        v = _gelu(h[:, SGU_WIDTH + lo:SGU_WIDTH + lo + CHUNK])
        vn = _layer_norm(v, lng_ref[g:g + 1, :], lnb_ref[g:g + 1, :]).astype(BF16)
        w_g = wsp_ref[g]
        b_g = bsp_ref[:, g:g + 1]
        for c in range(tm // CHUNK):
            r = c * CHUNK
            gate = _dot(w_g, vn[r:r + CHUNK, :]) + b_g
            a_ref[r:r + CHUNK, lo:lo + CHUNK] = (u[r:r + CHUNK, :] * gate).astype(BF16)


def _inproj_sgu(x2d, w_in, ln_g, ln_b, w_spatial, b_spatial, tm=256):
    n, d = x2d.shape
    w = w_in.astype(BF16)
    wsp = jnp.tril(w_spatial).astype(BF16)
    bsp_t = b_spatial.T
    n_qkv = w.shape[1] - 2 * SGU_WIDTH
    const = lambda i: (0, 0)
    return pl.pallas_call(
        functools.partial(_inproj_sgu_kernel, tm=tm),
        out_shape=(jax.ShapeDtypeStruct((n, SGU_WIDTH), BF16),
                   jax.ShapeDtypeStruct((n, n_qkv), BF16)),
        grid=(n // tm,),
        in_specs=[pl.BlockSpec((tm, d), lambda i: (i, 0)),
                  pl.BlockSpec(w.shape, const),
                  pl.BlockSpec((SGU_GROUPS, CHUNK), const),
                  pl.BlockSpec((SGU_GROUPS, CHUNK), const),
                  pl.BlockSpec(wsp.shape, lambda i: (0, 0, 0)),
                  pl.BlockSpec(bsp_t.shape, const)],
        out_specs=(pl.BlockSpec((tm, SGU_WIDTH), lambda i: (i, 0)),
                   pl.BlockSpec((tm, n_qkv), lambda i: (i, 0))),
        compiler_params=_params("parallel"),
        name="inproj_sgu",
    )(x2d, w, ln_g.reshape(SGU_GROUPS, CHUNK), ln_b.reshape(SGU_GROUPS, CHUNK), wsp, bsp_t)


def _diff_attn_kernel(q_ref, k_ref, v_ref, lam_ref, g_ref, o_ref, *, tq, lam_init):
    i = pl.program_id(2)
    lane = lax.broadcasted_iota(jnp.int32, (1, 2 * DIFF_HEAD_DIM), 1)
    q = q_ref[...] * jnp.asarray(DIFF_HEAD_DIM ** -0.5, BF16)
    zero = jnp.zeros_like(q)
    q1 = jnp.where(lane < DIFF_HEAD_DIM, q, zero)
    q2 = jnp.where(lane >= DIFF_HEAD_DIM, q, zero)

    def update(state, s, vj):
        m, l, acc = state
        m_new = jnp.maximum(m, jnp.max(s, axis=-1, keepdims=True))
        alpha = jnp.exp(m - m_new)
        p = jnp.exp(s - m_new)
        l_new = alpha * l + jnp.sum(p, axis=-1, keepdims=True)
        acc_new = alpha * acc + _dot(p.astype(BF16), vj)
        return m_new, l_new, acc_new

    def block(j, carry, masked):
        st1, st2 = carry
        off = pl.multiple_of(j * tq, tq)
        kj = k_ref[pl.ds(off, tq), :]
        vj = v_ref[pl.ds(off, tq), :]
        s1 = _dot_nt(q1, kj)
        s2 = _dot_nt(q2, kj)
        if masked:
            row = lax.broadcasted_iota(jnp.int32, (tq, tq), 0)
            col = lax.broadcasted_iota(jnp.int32, (tq, tq), 1)
            keep = col <= row
            s1 = jnp.where(keep, s1, NEG_INF)
            s2 = jnp.where(keep, s2, NEG_INF)
        return update(st1, s1, vj), update(st2, s2, vj)

    init = (jnp.full((tq, 1), NEG_INF, F32), jnp.zeros((tq, 1), F32), jnp.zeros((tq, DIFF_V_DIM), F32))
    carry = lax.fori_loop(0, i, lambda j, c: block(j, c, False), (init, init))
    (_, l1, a1), (_, l2, a2) = block(i, carry, True)

    lam_v = lam_ref[...]
    s_a = jnp.sum(lam_v[0:1, :] * lam_v[1:2, :], axis=-1, keepdims=True)
    s_b = jnp.sum(lam_v[2:3, :] * lam_v[3:4, :], axis=-1, keepdims=True)
    lam = jnp.exp(s_a) - jnp.exp(s_b) + lam_init
    o = a1 / l1 - lam * (a2 / l2)
    o = o * lax.rsqrt(jnp.mean(o * o, axis=-1, keepdims=True) + LN_EPS) * g_ref[...] * (1.0 - lam_init)
    o_ref[...] = o.astype(BF16)


def _diff_attention(qkv, lam_q1, lam_k1, lam_q2, lam_k2, subln_g, lam_init, batch, seq, tq=256):
    n = qkv.shape[0]
    nq = seq // tq
    lam_v = jnp.stack([lam_q1, lam_k1, lam_q2, lam_k2])
    hw = 2 * DIFF_HEAD_DIM
    return pl.pallas_call(
        functools.partial(_diff_attn_kernel, tq=tq, lam_init=lam_init),
        out_shape=jax.ShapeDtypeStruct((n, DIFF_HEADS * DIFF_V_DIM), BF16),
        grid=(batch, DIFF_HEADS, nq),
        in_specs=[pl.BlockSpec((tq, hw), lambda b, h, i: (b * nq + i, h)),
                  pl.BlockSpec((seq, hw), lambda b, h, i: (b, DIFF_HEADS + h)),
                  pl.BlockSpec((seq, DIFF_V_DIM), lambda b, h, i: (b, 2 * DIFF_HEADS + h)),
                  pl.BlockSpec(lam_v.shape, lambda b, h, i: (0, 0)),
                  pl.BlockSpec((1, DIFF_V_DIM), lambda b, h, i: (0, 0))],
        out_specs=pl.BlockSpec((tq, DIFF_V_DIM), lambda b, h, i: (b * nq + i, h)),
        compiler_params=_params("parallel", "parallel", "parallel"),
        name="diff_attention",
    )(qkv, qkv, qkv, lam_v, subln_g.reshape(1, DIFF_V_DIM))


def _outproj_ln_kernel(a_ref, d_ref, w_ref, x_ref, g_ref, b_ref, o_ref):
    y = _dot(a_ref[...], w_ref[:SGU_WIDTH, :]) + _dot(d_ref[...], w_ref[SGU_WIDTH:, :])
    o_ref[...] = _layer_norm(ALPHA * x_ref[...] + y, g_ref[...], b_ref[...])


def _outproj_ln(a, dattn, w_out, x2d, ln_g, ln_b, tm=512):
    n, d = x2d.shape
    w = w_out.astype(BF16)
    const = lambda i: (0, 0)
    row = lambda i: (i, 0)
    return pl.pallas_call(
        _outproj_ln_kernel,
        out_shape=jax.ShapeDtypeStruct((n, d), F32),
        grid=(n // tm,),
        in_specs=[pl.BlockSpec((tm, a.shape[1]), row),
                  pl.BlockSpec((tm, dattn.shape[1]), row),
                  pl.BlockSpec(w.shape, const),
                  pl.BlockSpec((tm, d), row),
                  pl.BlockSpec((1, d), const),
                  pl.BlockSpec((1, d), const)],
        out_specs=pl.BlockSpec((tm, d), row),
        compiler_params=_params("parallel"),
        name="outproj_ln1",
    )(a, dattn, w, x2d, ln_g.reshape(1, d), ln_b.reshape(1, d))


def _cross_router_kernel(x_ref, wq_ref, wo_ref, k_ref, v_ref, g_ref, b_ref, rwh_ref, rwl_ref, rb_ref,
                         x2_ref, idx_ref, gate_ref, rank_ref, cnt_ref, seen):
    @pl.when(jnp.logical_and(pl.program_id(0) == 0, pl.program_id(1) == 0))
    def _():
        seen[...] = jnp.zeros_like(seen)

    x = x_ref[...]
    q = (_dot(x.astype(BF16), wq_ref[...]) * (X_HEAD_DIM ** -0.5)).astype(BF16)
    heads = []
    for h in range(X_HEADS):
        lo = h * X_HEAD_DIM
        s = _dot_nt(q[:, lo:lo + X_HEAD_DIM], k_ref[:, lo:lo + X_HEAD_DIM])
        p = jnp.exp(s - jnp.max(s, axis=-1, keepdims=True))
        p = p / jnp.sum(p, axis=-1, keepdims=True)
        heads.append(_dot(p.astype(BF16), v_ref[:, lo:lo + X_HEAD_DIM]).astype(BF16))
    o = jnp.concatenate(heads, axis=-1)
    x2 = _layer_norm(ALPHA * x + _dot(o, wo_ref[...]), g_ref[...], b_ref[...])
    x2_ref[...] = x2

    x_hi = x2.astype(BF16)
    x_lo = (x2 - x_hi.astype(F32)).astype(BF16)
    logits = _dot(x_hi, rwh_ref[...]) + _dot(x_lo, rwh_ref[...]) + _dot(x_hi, rwl_ref[...]) + rb_ref[...]
    lane = lax.broadcasted_iota(jnp.int32, logits.shape, 1).astype(F32)
    work = logits
    top_v, top_i = [], []
    for _ in range(TOP_K):
        m = jnp.max(work, axis=-1, keepdims=True)
        sel = jnp.min(jnp.where(work == m, lane, float(LANES)), axis=-1, keepdims=True)
        top_v.append(m)
        top_i.append(sel)
        work = jnp.where(lane == sel, -jnp.inf, work)
    e = [jnp.exp(v - top_v[0]) for v in top_v]
    denom = e[0] + e[1] + e[2] + e[3]
    idx_out = jnp.zeros(logits.shape, F32)
    gate_out = jnp.zeros(logits.shape, F32)
    for k in range(TOP_K):
        idx_out = jnp.where(lane == float(k), top_i[k], idx_out)
        gate_out = jnp.where(lane == float(k), e[k] / denom, gate_out)
    idx_ref[...] = idx_out.astype(jnp.int32)
    gate_ref[...] = gate_out

    tm = logits.shape[0]
    onehot = jnp.zeros(logits.shape, F32)
    for k in range(TOP_K):
        onehot = jnp.where(lane == top_i[k], 1.0, onehot)
    r_i = lax.broadcasted_iota(jnp.int32, (tm, tm), 0)
    c_i = lax.broadcasted_iota(jnp.int32, (tm, tm), 1)
    earlier = jnp.where(c_i < r_i, 1.0, 0.0).astype(BF16)
    pos = _dot(earlier, onehot.astype(BF16)) + seen[...]
    rank_out = jnp.zeros(logits.shape, F32)
    for k in range(TOP_K):
        rk = jnp.sum(jnp.where(lane == top_i[k], pos, 0.0), axis=-1, keepdims=True)
        rank_out = jnp.where(lane == float(k), rk, rank_out)
    rank_ref[...] = rank_out.astype(jnp.int32)
    total = seen[...] + jnp.sum(onehot, axis=0, keepdims=True)
    seen[...] = total
    cnt_ref[...] = jnp.broadcast_to(total, cnt_ref.shape).astype(jnp.int32)


def _cross_router(x1, kv, w_q, w_o, ln_g, ln_b, router_w, router_b, batch, seq, mem_len, tm=256):
    n, d = x1.shape
    nt = seq // tm
    pad = LANES - N_EXPERTS
    rw = jnp.pad(router_w, ((0, 0), (0, pad)))
    rw_hi = rw.astype(BF16)
    rw_lo = (rw - rw_hi.astype(F32)).astype(BF16)
    rb = jnp.pad(router_b, (0, pad), constant_values=-jnp.inf).reshape(1, LANES)
    const = lambda b, i: (0, 0)
    row = lambda b, i: (b * nt + i, 0)
    return pl.pallas_call(
        _cross_router_kernel,
        out_shape=(jax.ShapeDtypeStruct((n, d), F32),
                   jax.ShapeDtypeStruct((n, LANES), jnp.int32),
                   jax.ShapeDtypeStruct((n, LANES), F32),
                   jax.ShapeDtypeStruct((n, LANES), jnp.int32),
                   jax.ShapeDtypeStruct((SUBLANES, LANES), jnp.int32)),
        grid=(batch, nt),
        in_specs=[pl.BlockSpec((tm, d), row),
                  pl.BlockSpec((d, d), const),
                  pl.BlockSpec((d, d), const),
                  pl.BlockSpec((mem_len, d), lambda b, i: (b, 0)),
                  pl.BlockSpec((mem_len, d), lambda b, i: (b, 1)),
                  pl.BlockSpec((1, d), const),
                  pl.BlockSpec((1, d), const),
                  pl.BlockSpec((d, LANES), const),
                  pl.BlockSpec((d, LANES), const),
                  pl.BlockSpec((1, LANES), const)],
        out_specs=(pl.BlockSpec((tm, d), row),
                   pl.BlockSpec((tm, LANES), row),
                   pl.BlockSpec((tm, LANES), row),
                   pl.BlockSpec((tm, LANES), row),
                   pl.BlockSpec((SUBLANES, LANES), const)),
        scratch_shapes=[pltpu.VMEM((1, LANES), F32)],
        compiler_params=_params("arbitrary", "arbitrary"),
        name="cross_attn_router",
    )(x1, w_q.astype(BF16), w_o.astype(BF16), kv, kv, ln_g.reshape(1, d), ln_b.reshape(1, d), rw_hi, rw_lo, rb)


def _dispatch_kernel(pstart_ref, cnt_ref, dest_ref, x_ref, xs_hbm, slots, zrow, sems, zsem, *, tm, nt):
    i = pl.program_id(0)
    slot = lax.rem(i, 2)
    slots[slot] = x_ref[...]

    for k in range(TOP_K):
        def issue(r, c, k=k):
            dst = dest_ref[0, 0, k * tm + r]
            pltpu.make_async_copy(slots.at[slot, pl.ds(r, 1), :], xs_hbm.at[pl.ds(dst, 1), :], sems.at[slot]).start()
            return c

        lax.fori_loop(0, tm, issue, 0, unroll=8)

    def drain(s):
        def wait(r, c):
            pltpu.make_async_copy(slots.at[s, pl.ds(0, 1), :], xs_hbm.at[pl.ds(0, 1), :], sems.at[s]).wait()
            return c

        lax.fori_loop(0, TOP_K * tm, wait, 0, unroll=8)

    @pl.when(i > 0)
    def _():
        drain(1 - slot)

    @pl.when(i == nt - 1)
    def _():
        drain(slot)
        zrow[...] = jnp.zeros_like(zrow)

        def pad_rows(e):
            c = cnt_ref[e]
            return lax.rem(MOE_BLOCK - lax.rem(c, MOE_BLOCK), MOE_BLOCK)

        def fill(e, c):
            base = pstart_ref[e] + cnt_ref[e]

            def one(r, c2):
                pltpu.make_async_copy(zrow.at[pl.ds(0, 1), :], xs_hbm.at[pl.ds(base + r, 1), :], zsem).start()
                return c2

            lax.fori_loop(0, pad_rows(e), one, 0)
            return c

        def fill_wait(e, c):
            def one(r, c2):
                pltpu.make_async_copy(zrow.at[pl.ds(0, 1), :], xs_hbm.at[pl.ds(0, 1), :], zsem).wait()
                return c2

            lax.fori_loop(0, pad_rows(e), one, 0)
            return c

        lax.fori_loop(0, N_EXPERTS, fill, 0)
        lax.fori_loop(0, N_EXPERTS, fill_wait, 0)


def _dispatch(x2, dest_t, pstart, counts, n_pad, tm):
    n, d = x2.shape
    nt = n // tm
    grid_spec = pltpu.PrefetchScalarGridSpec(
        num_scalar_prefetch=2,
        grid=(nt,),
        in_specs=[pl.BlockSpec((1, 1, TOP_K * tm), lambda i, ps, cn: (i, 0, 0), memory_space=pltpu.SMEM),
                  pl.BlockSpec((tm, d), lambda i, ps, cn: (i, 0))],
        out_specs=pl.BlockSpec(memory_space=pl.ANY),
        scratch_shapes=[pltpu.VMEM((2, tm, d), F32),
                        pltpu.VMEM((SUBLANES, d), F32),
                        pltpu.SemaphoreType.DMA((2,)),
                        pltpu.SemaphoreType.DMA],
    )
    return pl.pallas_call(
        functools.partial(_dispatch_kernel, tm=tm, nt=nt),
        out_shape=jax.ShapeDtypeStruct((n_pad, d), F32),
        grid_spec=grid_spec,
        compiler_params=_params("arbitrary"),
        name="moe_dispatch",
    )(pstart, counts, dest_t, x2)


def _expert_kernel(be_ref, nused_ref, xs_ref, wup_ref, bup_ref, wdn_ref, bdn_ref, y_ref, wup_bf, wdn_bf):
    i = pl.program_id(0)

    @pl.when(i >= nused_ref[0])
    def _():
        y_ref[...] = jnp.zeros_like(y_ref)

    @pl.when(i < nused_ref[0])
    def _():
        prev = be_ref[jnp.maximum(i - 1, 0)]

        @pl.when(jnp.logical_or(i == 0, be_ref[i] != prev))
        def _():
            wup_bf[...] = wup_ref[0].astype(BF16)
            wdn_bf[...] = wdn_ref[0].astype(BF16)

        xb = xs_ref[...].astype(BF16)
        h = _dot(xb, wup_bf[...]) + bup_ref[0]
        glu = jnp.minimum(h[:, :D_FF], SWIGLU_LIMIT)
        lin = jnp.clip(h[:, D_FF:], -SWIGLU_LIMIT, SWIGLU_LIMIT)
        act = glu * _sigmoid(SWIGLU_ALPHA * glu) * (lin + 1.0)
        y_ref[...] = _dot(act.astype(BF16), wdn_bf[...]) + bdn_ref[0]


def _experts(xs, blk_expert, n_used, w_up, b_up, w_down, b_down):
    n_blocks = blk_expert.shape[0]
    d = xs.shape[1]
    f2 = w_up.shape[2]
    grid_spec = pltpu.PrefetchScalarGridSpec(
        num_scalar_prefetch=2,
        grid=(n_blocks,),
        in_specs=[pl.BlockSpec((MOE_BLOCK, d), lambda i, be, nu: (jnp.where(i < nu[0], i, 0), 0)),
                  pl.BlockSpec((1, d, f2), lambda i, be, nu: (be[i], 0, 0)),
                  pl.BlockSpec((1, 1, f2), lambda i, be, nu: (be[i], 0, 0)),
                  pl.BlockSpec((1, D_FF, d), lambda i, be, nu: (be[i], 0, 0)),
                  pl.BlockSpec((1, 1, d), lambda i, be, nu: (be[i], 0, 0))],
        out_specs=pl.BlockSpec((MOE_BLOCK, d), lambda i, be, nu: (i, 0)),
        scratch_shapes=[pltpu.VMEM((d, f2), BF16),
                        pltpu.VMEM((D_FF, d), BF16)],
    )
    return pl.pallas_call(
        _expert_kernel,
        out_shape=jax.ShapeDtypeStruct((n_blocks * MOE_BLOCK, d), F32),
        grid_spec=grid_spec,
        compiler_params=_params("arbitrary"),
        name="moe_experts",
    )(blk_expert, n_used, xs, w_up, b_up.reshape(N_EXPERTS, 1, f2), w_down, b_down.reshape(N_EXPERTS, 1, d))


def _combine_ln_kernel(dest_ref, dest_next_ref, ys_hbm, gate_ref, x_ref, g_ref, b_ref, o_ref, gbuf, sems, *, tm, nt):
    i = pl.program_id(0)
    slot = lax.rem(i, 2)

    def gather(dref, s):
        for k in range(TOP_K):
            def issue(r, c, k=k):
                src = dref[0, 0, k * tm + r]
                pltpu.make_async_copy(ys_hbm.at[pl.ds(src, 1), :], gbuf.at[s, k, pl.ds(r, 1), :], sems.at[s]).start()
                return c

            lax.fori_loop(0, tm, issue, 0, unroll=8)

    @pl.when(i == 0)
    def _():
        gather(dest_ref, 0)

    @pl.when(i + 1 < nt)
    def _():
        gather(dest_next_ref, 1 - slot)

    def wait(r, c):
        pltpu.make_async_copy(ys_hbm.at[pl.ds(0, 1), :], gbuf.at[slot, 0, pl.ds(0, 1), :], sems.at[slot]).wait()
        return c

    lax.fori_loop(0, TOP_K * tm, wait, 0, unroll=8)
    gates = gate_ref[...]
    y = gates[:, 0:1] * gbuf[slot, 0]
    for k in range(1, TOP_K):
        y = y + gates[:, k:k + 1] * gbuf[slot, k]
    o_ref[...] = _layer_norm(ALPHA * x_ref[...] + y, g_ref[...], b_ref[...])


def _combine_ln(ys, dest_t, gates, x2, ln_g, ln_b, tm):
    n, d = x2.shape
    nt = n // tm
    const = lambda i: (0, 0)
    row = lambda i: (i, 0)
    return pl.pallas_call(
        functools.partial(_combine_ln_kernel, tm=tm, nt=nt),
        out_shape=jax.ShapeDtypeStruct((n, d), F32),
        grid=(nt,),
        in_specs=[pl.BlockSpec((1, 1, TOP_K * tm), lambda i: (i, 0, 0), memory_space=pltpu.SMEM),
                  pl.BlockSpec((1, 1, TOP_K * tm), lambda i: (jnp.minimum(i + 1, nt - 1), 0, 0),
                               memory_space=pltpu.SMEM),
                  pl.BlockSpec(memory_space=pl.ANY),
                  pl.BlockSpec((tm, LANES), row),
                  pl.BlockSpec((tm, d), row),
                  pl.BlockSpec((1, d), const),
                  pl.BlockSpec((1, d), const)],
        out_specs=pl.BlockSpec((tm, d), row),
        scratch_shapes=[pltpu.VMEM((2, TOP_K, tm, d), F32), pltpu.SemaphoreType.DMA((2,))],
        compiler_params=_params("arbitrary"),
        name="moe_combine_ln3",
    )(dest_t, dest_t, ys, gates, x2, ln_g.reshape(1, d), ln_b.reshape(1, d))


def _moe_layer(x2, idx_pad, rank_pad, cnt_pad, gates_pad, w_up, b_up, w_down, b_down, ln_g, ln_b, tm=256):
    n = x2.shape[0]
    nt = n // tm
    n_blocks = n * TOP_K // MOE_BLOCK + N_EXPERTS
    counts = cnt_pad[0, :N_EXPERTS]
    padded = (counts + MOE_BLOCK - 1) // MOE_BLOCK * MOE_BLOCK
    pend = jnp.cumsum(padded)
    pstart = (pend - padded).astype(jnp.int32)
    block_start = jnp.arange(n_blocks, dtype=jnp.int32) * MOE_BLOCK
    blk_expert = jnp.minimum(jnp.sum(pend[None, :] <= block_start[:, None], axis=1), N_EXPERTS - 1).astype(jnp.int32)
    n_used = (pend[-1] // MOE_BLOCK).astype(jnp.int32).reshape(1)
    idx = idx_pad[:, :TOP_K]
    expert_ids = jnp.arange(N_EXPERTS, dtype=jnp.int32)
    dest = jnp.sum(jnp.where(idx[..., None] == expert_ids, pstart, 0), axis=-1) + rank_pad[:, :TOP_K]
    dest_t = dest.astype(jnp.int32).reshape(nt, tm, TOP_K).transpose(0, 2, 1).reshape(nt, 1, TOP_K * tm)

    xs = _dispatch(x2, dest_t, pstart, counts.astype(jnp.int32), n_blocks * MOE_BLOCK, tm)
    ys = _experts(xs, blk_expert, n_used, w_up, b_up, w_down, b_down)
    return _combine_ln(ys, dest_t, gates_pad, x2, ln_g, ln_b, tm)


def _s5_kernel(x_ref, win_ref, bblk_ref, ar_ref, ai_ref, cblk_ref, dsk_ref, wval_ref, wgate_ref, g_ref, b_ref,
               o_ref, bur, bui, sr, si, *, tt, batch):
    rows = tt * batch

    @pl.when(pl.program_id(0) == 0)
    def _():
        sr[...] = jnp.zeros_like(sr)
        si[...] = jnp.zeros_like(si)

    x = x_ref[...]
    u = _dot(x.astype(BF16), win_ref[...])
    ub = u.astype(BF16)
    for c in range(S5_N_CHUNKS):
        bu = _dot(ub[:, c * S5_CHUNK_IN:(c + 1) * S5_CHUNK_IN], bblk_ref[c])
        bur[:, c * S5_CHUNK_STATE:(c + 1) * S5_CHUNK_STATE] = bu[:, :S5_CHUNK_STATE]
        bui[:, c * S5_CHUNK_STATE:(c + 1) * S5_CHUNK_STATE] = bu[:, S5_CHUNK_STATE:]

    for c in range(S5_N_CHUNKS):
        cols = pl.ds(c * S5_CHUNK_STATE, S5_CHUNK_STATE)
        a_r = ar_ref[:, cols]
        a_i = ai_ref[:, cols]

        def step(t, carry):
            s_r, s_i = carry
            rsl = pl.ds(pl.multiple_of(t * batch, batch), batch)
            n_r = a_r * s_r - a_i * s_i + bur[rsl, cols]
            n_i = a_r * s_i + a_i * s_r + bui[rsl, cols]
            bur[rsl, cols] = n_r
            bui[rsl, cols] = n_i
            return n_r, n_i

        f_r, f_i = lax.fori_loop(0, tt, step, (sr[:, cols], si[:, cols]), unroll=True)
        sr[:, cols] = f_r
        si[:, cols] = f_i

    ys = []
    for c in range(S5_N_CHUNKS):
        cols = pl.ds(c * S5_CHUNK_STATE, S5_CHUNK_STATE)
        xri = jnp.concatenate([bur[:, cols].astype(BF16), bui[:, cols].astype(BF16)], axis=-1)
        ys.append(_dot(xri, cblk_ref[c]))
    y = jnp.concatenate(ys, axis=-1) + dsk_ref[...] * u
    yb = _gelu(y).astype(BF16)
    hmix = _dot(yb, wval_ref[...]) * _sigmoid(_dot(yb, wgate_ref[...]))
    o_ref[...] = _layer_norm(ALPHA * x + hmix, g_ref[...], b_ref[...])


def _s5_discretize(log_dt, lambda_re, lambda_im, b_re, b_im, c_re, c_im):
    dt = jnp.exp(log_dt)[:, None]
    mag = jnp.exp(lambda_re * dt)
    ar = mag * jnp.cos(lambda_im * dt)
    ai = mag * jnp.sin(lambda_im * dt)
    den = lambda_re * lambda_re + lambda_im * lambda_im
    zr = ((ar - 1.0) * lambda_re + ai * lambda_im) / den
    zi = (ai * lambda_re - (ar - 1.0) * lambda_im) / den
    bbar_re = zr[..., None] * b_re - zi[..., None] * b_im
    bbar_im = zr[..., None] * b_im + zi[..., None] * b_re
    ng, gc, p, gw = S5_N_CHUNKS, S5_CHUNK_GROUPS, S5_STATE, S5_GROUP
    eye = jnp.eye(gc, dtype=F32)
    bre = bbar_re.reshape(ng, gc, p, gw).transpose(0, 1, 3, 2)
    bim = bbar_im.reshape(ng, gc, p, gw).transpose(0, 1, 3, 2)
    blk_re = jnp.einsum('cgip,gh->cgihp', bre, eye).reshape(ng, gc * gw, gc * p)
    blk_im = jnp.einsum('cgip,gh->cgihp', bim, eye).reshape(ng, gc * gw, gc * p)
    bblk = jnp.concatenate([blk_re, blk_im], axis=-1).astype(BF16)
    cre = c_re.reshape(ng, gc, gw, p).transpose(0, 1, 3, 2)
    cim = c_im.reshape(ng, gc, gw, p).transpose(0, 1, 3, 2)
    cblk_re = jnp.einsum('cgpi,gh->cgphi', cre, eye).reshape(ng, gc * p, gc * gw)
    cblk_im = jnp.einsum('cgpi,gh->cgphi', cim, eye).reshape(ng, gc * p, gc * gw)
    cblk = jnp.concatenate([cblk_re, -cblk_im], axis=1).astype(BF16)
    return bblk, ar.reshape(1, S5_STATES), ai.reshape(1, S5_STATES), cblk


def _s5_mixer_ln(x_tm, w_in, log_dt, lambda_re, lambda_im, b_re, b_im, c_re, c_im, d_skip, w_val, w_gate,
                 ln_g, ln_b, batch, tt=16):
    n, d = x_tm.shape
    rows = tt * batch
    bblk, ar, ai, cblk = _s5_discretize(log_dt, lambda_re, lambda_im, b_re, b_im, c_re, c_im)
    const2 = lambda i: (0, 0)
    const3 = lambda i: (0, 0, 0)
    row = lambda i: (i, 0)
    return pl.pallas_call(
        functools.partial(_s5_kernel, tt=tt, batch=batch),
        out_shape=jax.ShapeDtypeStruct((n, d), F32),
        grid=(n // rows,),
        in_specs=[pl.BlockSpec((rows, d), row),
                  pl.BlockSpec((d, d), const2),
                  pl.BlockSpec(bblk.shape, const3),
                  pl.BlockSpec(ar.shape, const2),
                  pl.BlockSpec(ai.shape, const2),
                  pl.BlockSpec(cblk.shape, const3),
                  pl.BlockSpec((1, d), const2),
                  pl.BlockSpec((d, d), const2),
                  pl.BlockSpec((d, d), const2),
                  pl.BlockSpec((1, d), const2),
                  pl.BlockSpec((1, d), const2)],
        out_specs=pl.BlockSpec((rows, d), row),
        scratch_shapes=[pltpu.VMEM((rows, S5_STATES), F32),
                        pltpu.VMEM((rows, S5_STATES), F32),
                        pltpu.VMEM((batch, S5_STATES), F32),
                        pltpu.VMEM((batch, S5_STATES), F32)],
        compiler_params=_params("arbitrary"),
        name="s5_mixer_ln1",
    )(x_tm, w_in.astype(BF16), bblk, ar, ai, cblk, d_skip.reshape(1, d), w_val.astype(BF16), w_gate.astype(BF16),
      ln_g.reshape(1, d), ln_b.reshape(1, d))


def _diff_lambda_init(layer_idx):
    return 0.8 - 0.6 * math.exp(-0.3 * layer_idx)


def kernel(x, mem, w_mem_kv, l0_w_in, l0_sgu_ln_g, l0_sgu_ln_b, l0_w_spatial, l0_b_spatial, l0_lam_q1, l0_lam_k1, l0_lam_q2, l0_lam_k2, l0_subln_g, l0_w_out, l0_ln1_g, l0_ln1_b, l0_xq, l0_xo, l0_ln2_g, l0_ln2_b, l0_router_w, l0_router_b, l0_exp_w_up, l0_exp_b_up, l0_exp_w_down, l0_exp_b_down, l0_ln3_g, l0_ln3_b, l1_w_in, l1_log_dt, l1_lambda_re, l1_lambda_im, l1_b_re, l1_b_im, l1_c_re, l1_c_im, l1_d_skip, l1_w_val, l1_w_gate, l1_ln1_g, l1_ln1_b, l1_xq, l1_xo, l1_ln2_g, l1_ln2_b, l1_router_w, l1_router_b, l1_exp_w_up, l1_exp_b_up, l1_exp_w_down, l1_exp_b_down, l1_ln3_g, l1_ln3_b):
    batch, seq, d = x.shape
    mem_len = mem.shape[1]
    n = batch * seq
    x0 = x.reshape(n, d)

    kv = _matmul(mem.reshape(batch * mem_len, d), w_mem_kv.astype(BF16), BF16,
                 tm=min(512, batch * mem_len), tn=d)

    a, qkv = _inproj_sgu(x0, l0_w_in, l0_sgu_ln_g, l0_sgu_ln_b, l0_w_spatial, l0_b_spatial)
    dattn = _diff_attention(qkv, l0_lam_q1, l0_lam_k1, l0_lam_q2, l0_lam_k2, l0_subln_g,
                            _diff_lambda_init(0), batch, seq)
    x1 = _outproj_ln(a, dattn, l0_w_out, x0, l0_ln1_g, l0_ln1_b)
    x2, idx, gates, rank, cnt = _cross_router(x1, kv, l0_xq, l0_xo, l0_ln2_g, l0_ln2_b, l0_router_w, l0_router_b,
                                              batch, seq, mem_len)
    x3 = _moe_layer(x2, idx, rank, cnt, gates, l0_exp_w_up, l0_exp_b_up, l0_exp_w_down, l0_exp_b_down,
                    l0_ln3_g, l0_ln3_b)

    x3_tm = x3.reshape(batch, seq, d).transpose(1, 0, 2).reshape(n, d)
    x4_tm = _s5_mixer_ln(x3_tm, l1_w_in, l1_log_dt, l1_lambda_re, l1_lambda_im, l1_b_re, l1_b_im,
                         l1_c_re, l1_c_im, l1_d_skip, l1_w_val, l1_w_gate, l1_ln1_g, l1_ln1_b, batch)
    x4 = x4_tm.reshape(seq, batch, d).transpose(1, 0, 2).reshape(n, d)
    x5, idx, gates, rank, cnt = _cross_router(x4, kv, l1_xq, l1_xo, l1_ln2_g, l1_ln2_b, l1_router_w, l1_router_b,
                                              batch, seq, mem_len)
    x6 = _moe_layer(x5, idx, rank, cnt, gates, l1_exp_w_up, l1_exp_b_up, l1_exp_w_down, l1_exp_b_down,
                    l1_ln3_g, l1_ln3_b)
    return x6.reshape(batch, seq, d)
```

```python
import functools
import math

import jax
import jax.numpy as jnp
from jax import lax
from jax.experimental import pallas as pl
from jax.experimental.pallas import tpu as pltpu

F32 = jnp.float32
BF16 = jnp.bfloat16

D_MODEL = 1024
CHUNK = 128
SGU_GROUPS = 4
SGU_WIDTH = 512
DIFF_HEADS = 4
DIFF_HEAD_DIM = 64
DIFF_V_DIM = 128
S5_GROUP = 16
S5_GROUPS = 64
S5_STATE = 64
X_HEADS = 4
X_HEAD_DIM = 256
N_EXPERTS = 32
TOP_K = 4
D_FF = 1024
SWIGLU_LIMIT = 7.0
SWIGLU_ALPHA = 1.702
MOE_BLOCK = 256
DEPTH = 2
ALPHA = (2 * DEPTH) ** 0.25
LN_EPS = 1e-5
NEG_INF = -1e30

LANES = 128
SUBLANES = 8
VMEM_LIMIT_BYTES = 56 * 1024 * 1024

S5_CHUNK_GROUPS = 8
S5_CHUNK_IN = S5_CHUNK_GROUPS * S5_GROUP
S5_CHUNK_STATE = S5_CHUNK_GROUPS * S5_STATE
S5_N_CHUNKS = S5_GROUPS // S5_CHUNK_GROUPS
S5_STATES = S5_GROUPS * S5_STATE


def _params(*sem):
    return pltpu.CompilerParams(dimension_semantics=sem, vmem_limit_bytes=VMEM_LIMIT_BYTES)


def _gelu(x):
    return 0.5 * x * (1.0 + jnp.tanh(math.sqrt(2.0 / math.pi) * (x + 0.044715 * (x * x * x))))


def _sigmoid(x):
    return 1.0 / (1.0 + jnp.exp(-x))


def _layer_norm(z, g, b):
    mu = jnp.mean(z, axis=-1, keepdims=True)
    zc = z - mu
    var = jnp.mean(zc * zc, axis=-1, keepdims=True)
    return zc * lax.rsqrt(var + LN_EPS) * g + b


def _dot(a, b):
    return jnp.dot(a, b, preferred_element_type=F32)


def _dot_nt(a, b):
    return lax.dot_general(a, b, (((1,), (1,)), ((), ())), preferred_element_type=F32)


def _matmul_kernel(x_ref, w_ref, o_ref):
    o_ref[...] = _dot(x_ref[...].astype(BF16), w_ref[...]).astype(o_ref.dtype)


def _matmul(x, w, out_dtype, tm, tn):
    m, k = x.shape
    n = w.shape[1]
    return pl.pallas_call(
        _matmul_kernel,
        out_shape=jax.ShapeDtypeStruct((m, n), out_dtype),
        grid=(n // tn, m // tm),
        in_specs=[pl.BlockSpec((tm, k), lambda j, i: (i, 0)),
                  pl.BlockSpec((k, tn), lambda j, i: (0, j))],
        out_specs=pl.BlockSpec((tm, tn), lambda j, i: (i, j)),
        compiler_params=_params("parallel", "parallel"),
        name="matmul",
    )(x, w)


def _inproj_sgu_kernel(x_ref, w_ref, lng_ref, lnb_ref, wsp_ref, bsp_ref, a_ref, qkv_ref, *, tm):
    xb = x_ref[...].astype(BF16)
    qkv_ref[...] = _dot(xb, w_ref[:, 2 * SGU_WIDTH:]).astype(BF16)
    h = _dot(xb, w_ref[:, :2 * SGU_WIDTH])
    for g in range(SGU_GROUPS):
        lo = g * CHUNK
        u = _gelu(h[:, lo:lo + CHUNK])
        v = _gelu(h[:, SGU_WIDTH + lo:SGU_WIDTH + lo + CHUNK])
        vn = _layer_norm(v, lng_ref[g:g + 1, :], lnb_ref[g:g + 1, :]).astype(BF16)
        w_g = wsp_ref[g]
        b_g = bsp_ref[:, g:g + 1]
        for c in range(tm // CHUNK):
            r = c * CHUNK
            gate = _dot(w_g, vn[r:r + CHUNK, :]) + b_g
            a_ref[r:r + CHUNK, lo:lo + CHUNK] = (u[r:r + CHUNK, :] * gate).astype(BF16)


def _inproj_sgu(x2d, w_in, ln_g, ln_b, w_spatial, b_spatial, tm=256):
    n, d = x2d.shape
    w = w_in.astype(BF16)
    wsp = jnp.tril(w_spatial).astype(BF16)
    bsp_t = b_spatial.T
    n_qkv = w.shape[1] - 2 * SGU_WIDTH
    const = lambda i: (0, 0)
    return pl.pallas_call(
        functools.partial(_inproj_sgu_kernel, tm=tm),
        out_shape=(jax.ShapeDtypeStruct((n, SGU_WIDTH), BF16),
                   jax.ShapeDtypeStruct((n, n_qkv), BF16)),
        grid=(n // tm,),
        in_specs=[pl.BlockSpec((tm, d), lambda i: (i, 0)),
                  pl.BlockSpec(w.shape, const),
                  pl.BlockSpec((SGU_GROUPS, CHUNK), const),
                  pl.BlockSpec((SGU_GROUPS, CHUNK), const),
                  pl.BlockSpec(wsp.shape, lambda i: (0, 0, 0)),
                  pl.BlockSpec(bsp_t.shape, const)],
        out_specs=(pl.BlockSpec((tm, SGU_WIDTH), lambda i: (i, 0)),
                   pl.BlockSpec((tm, n_qkv), lambda i: (i, 0))),
        compiler_params=_params("parallel"),
        name="inproj_sgu",
    )(x2d, w, ln_g.reshape(SGU_GROUPS, CHUNK), ln_b.reshape(SGU_GROUPS, CHUNK), wsp, bsp_t)


def _diff_attn_kernel(q_ref, k_ref, v_ref, lam_ref, g_ref, o_ref, *, tk, lam_init):
    i = pl.program_id(2)
    lane = lax.broadcasted_iota(jnp.int32, (1, 2 * DIFF_HEAD_DIM), 1)

    def split(q):
        q = q * jnp.asarray(DIFF_HEAD_DIM ** -0.5, BF16)
        zero = jnp.zeros_like(q)
        return jnp.where(lane < DIFF_HEAD_DIM, q, zero), jnp.where(lane >= DIFF_HEAD_DIM, q, zero)

    q_sub = (split(q_ref[0:tk, :]), split(q_ref[tk:2 * tk, :]))

    def kv_block(j):
        off = pl.multiple_of(j * tk, tk)
        return k_ref[pl.ds(off, tk), :], v_ref[pl.ds(off, tk), :]

    def update(state, s, vj):
        m, l, acc = state
        m_new = jnp.maximum(m, jnp.max(s, axis=-1, keepdims=True))
        alpha = jnp.exp(m - m_new)
        p = jnp.exp(s - m_new)
        l_new = alpha * l + jnp.sum(p, axis=-1, keepdims=True)
        acc_new = alpha * acc + _dot(p.astype(BF16), vj)
        return m_new, l_new, acc_new

    def attend(states, q12, kj, vj, masked):
        out = []
        for st, q in zip(states, q12):
            s = _dot_nt(q, kj)
            if masked:
                row = lax.broadcasted_iota(jnp.int32, (tk, tk), 0)
                col = lax.broadcasted_iota(jnp.int32, (tk, tk), 1)
                s = jnp.where(col <= row, s, NEG_INF)
            out.append(update(st, s, vj))
        return tuple(out)

    init = (jnp.full((tk, 1), NEG_INF, F32), jnp.zeros((tk, 1), F32), jnp.zeros((tk, DIFF_V_DIM), F32))

    def body(j, carry):
        kj, vj = kv_block(j)
        return tuple(attend(st, q12, kj, vj, False) for st, q12 in zip(carry, q_sub))

    st_a, st_b = lax.fori_loop(0, 2 * i, body, ((init, init), (init, init)))
    k0, v0 = kv_block(2 * i)
    k1, v1 = kv_block(2 * i + 1)
    st_a = attend(st_a, q_sub[0], k0, v0, True)
    st_b = attend(st_b, q_sub[1], k0, v0, False)
    st_b = attend(st_b, q_sub[1], k1, v1, True)

    lam_v = lam_ref[...]
    s_a = jnp.sum(lam_v[0:1, :] * lam_v[1:2, :], axis=-1, keepdims=True)
    s_b = jnp.sum(lam_v[2:3, :] * lam_v[3:4, :], axis=-1, keepdims=True)
    lam = jnp.exp(s_a) - jnp.exp(s_b) + lam_init
    for h, ((_, l1, a1), (_, l2, a2)) in enumerate((st_a, st_b)):
        o = a1 / l1 - lam * (a2 / l2)
        o = o * lax.rsqrt(jnp.mean(o * o, axis=-1, keepdims=True) + LN_EPS) * g_ref[...] * (1.0 - lam_init)
        o_ref[h * tk:(h + 1) * tk, :] = o.astype(BF16)


def _diff_attention(qkv, lam_q1, lam_k1, lam_q2, lam_k2, subln_g, lam_init, batch, seq, tk=256):
    n = qkv.shape[0]
    tq = 2 * tk
    nq = seq // tq
    lam_v = jnp.stack([lam_q1, lam_k1, lam_q2, lam_k2])
    hw = 2 * DIFF_HEAD_DIM
    return pl.pallas_call(
        functools.partial(_diff_attn_kernel, tk=tk, lam_init=lam_init),
        out_shape=jax.ShapeDtypeStruct((n, DIFF_HEADS * DIFF_V_DIM), BF16),
        grid=(batch, DIFF_HEADS, nq),
        in_specs=[pl.BlockSpec((tq, hw), lambda b, h, i: (b * nq + i, h)),
                  pl.BlockSpec((seq, hw), lambda b, h, i: (b, DIFF_HEADS + h)),
                  pl.BlockSpec((seq, DIFF_V_DIM), lambda b, h, i: (b, 2 * DIFF_HEADS + h)),
                  pl.BlockSpec(lam_v.shape, lambda b, h, i: (0, 0)),
                  pl.BlockSpec((1, DIFF_V_DIM), lambda b, h, i: (0, 0))],
        out_specs=pl.BlockSpec((tq, DIFF_V_DIM), lambda b, h, i: (b * nq + i, h)),
        compiler_params=_params("parallel", "parallel", "parallel"),
        name="diff_attention",
    )(qkv, qkv, qkv, lam_v, subln_g.reshape(1, DIFF_V_DIM))


def _outproj_ln_kernel(a_ref, d_ref, w_ref, x_ref, g_ref, b_ref, o_ref):
    y = _dot(a_ref[...], w_ref[:SGU_WIDTH, :]) + _dot(d_ref[...], w_ref[SGU_WIDTH:, :])
    o_ref[...] = _layer_norm(ALPHA * x_ref[...] + y, g_ref[...], b_ref[...])


def _outproj_ln(a, dattn, w_out, x2d, ln_g, ln_b, tm=512):
    n, d = x2d.shape
    w = w_out.astype(BF16)
    const = lambda i: (0, 0)
    row = lambda i: (i, 0)
    return pl.pallas_call(
        _outproj_ln_kernel,
        out_shape=jax.ShapeDtypeStruct((n, d), F32),
        grid=(n // tm,),
        in_specs=[pl.BlockSpec((tm, a.shape[1]), row),
                  pl.BlockSpec((tm, dattn.shape[1]), row),
                  pl.BlockSpec(w.shape, const),
                  pl.BlockSpec((tm, d), row),
                  pl.BlockSpec((1, d), const),
                  pl.BlockSpec((1, d), const)],
        out_specs=pl.BlockSpec((tm, d), row),
        compiler_params=_params("parallel"),
        name="outproj_ln1",
    )(a, dattn, w, x2d, ln_g.reshape(1, d), ln_b.reshape(1, d))


def _cross_router_kernel(x_ref, wq_ref, wo_ref, k_ref, v_ref, g_ref, b_ref, rwh_ref, rwl_ref, rb_ref,
                         x2_ref, idx_ref, gate_ref, rank_ref, cnt_ref):
    x = x_ref[...]
    q = (_dot(x.astype(BF16), wq_ref[...]) * (X_HEAD_DIM ** -0.5)).astype(BF16)
    heads = []
    for h in range(X_HEADS):
        lo = h * X_HEAD_DIM
        s = _dot_nt(q[:, lo:lo + X_HEAD_DIM], k_ref[:, lo:lo + X_HEAD_DIM])
        p = jnp.exp(s - jnp.max(s, axis=-1, keepdims=True))
        p = p / jnp.sum(p, axis=-1, keepdims=True)
        heads.append(_dot(p.astype(BF16), v_ref[:, lo:lo + X_HEAD_DIM]).astype(BF16))
    o = jnp.concatenate(heads, axis=-1)
    x2 = _layer_norm(ALPHA * x + _dot(o, wo_ref[...]), g_ref[...], b_ref[...])
    x2_ref[...] = x2

    x_hi = x2.astype(BF16)
    x_lo = (x2 - x_hi.astype(F32)).astype(BF16)
    logits = _dot(x_hi, rwh_ref[...]) + _dot(x_lo, rwh_ref[...]) + _dot(x_hi, rwl_ref[...]) + rb_ref[...]
    lane = lax.broadcasted_iota(jnp.int32, logits.shape, 1).astype(F32)
    work = logits
    top_v, top_i = [], []
    for _ in range(TOP_K):
        m = jnp.max(work, axis=-1, keepdims=True)
        sel = jnp.min(jnp.where(work == m, lane, float(LANES)), axis=-1, keepdims=True)
        top_v.append(m)
        top_i.append(sel)
        work = jnp.where(lane == sel, -jnp.inf, work)
    e = [jnp.exp(v - top_v[0]) for v in top_v]
    denom = e[0] + e[1] + e[2] + e[3]
    idx_out = jnp.zeros(logits.shape, F32)
    gate_out = jnp.zeros(logits.shape, F32)
    for k in range(TOP_K):
        idx_out = jnp.where(lane == float(k), top_i[k], idx_out)
        gate_out = jnp.where(lane == float(k), e[k] / denom, gate_out)
    idx_ref[...] = idx_out.astype(jnp.int32)
    gate_ref[...] = gate_out

    tm = logits.shape[0]
    onehot = jnp.zeros(logits.shape, F32)
    for k in range(TOP_K):
        onehot = jnp.where(lane == top_i[k], 1.0, onehot)
    r_i = lax.broadcasted_iota(jnp.int32, (tm, tm), 0)
    c_i = lax.broadcasted_iota(jnp.int32, (tm, tm), 1)
    earlier = jnp.where(c_i < r_i, 1.0, 0.0).astype(BF16)
    pos = _dot(earlier, onehot.astype(BF16))
    rank_out = jnp.zeros(logits.shape, F32)
    for k in range(TOP_K):
        rk = jnp.sum(jnp.where(lane == top_i[k], pos, 0.0), axis=-1, keepdims=True)
        rank_out = jnp.where(lane == float(k), rk, rank_out)
    rank_ref[...] = rank_out.astype(jnp.int32)
    total = jnp.sum(onehot, axis=0, keepdims=True)
    cnt_ref[...] = jnp.broadcast_to(total, cnt_ref.shape).astype(jnp.int32)


def _cross_router(x1, kv, w_q, w_o, ln_g, ln_b, router_w, router_b, batch, seq, mem_len, tm=256):
    n, d = x1.shape
    nt = seq // tm
    pad = LANES - N_EXPERTS
    rw = jnp.pad(router_w, ((0, 0), (0, pad)))
    rw_hi = rw.astype(BF16)
    rw_lo = (rw - rw_hi.astype(F32)).astype(BF16)
    rb = jnp.pad(router_b, (0, pad), constant_values=-jnp.inf).reshape(1, LANES)
    const = lambda b, i: (0, 0)
    row = lambda b, i: (b * nt + i, 0)
    return pl.pallas_call(
        _cross_router_kernel,
        out_shape=(jax.ShapeDtypeStruct((n, d), F32),
                   jax.ShapeDtypeStruct((n, LANES), jnp.int32),
                   jax.ShapeDtypeStruct((n, LANES), F32),
                   jax.ShapeDtypeStruct((n, LANES), jnp.int32),
                   jax.ShapeDtypeStruct((n // tm * SUBLANES, LANES), jnp.int32)),
        grid=(batch, nt),
        in_specs=[pl.BlockSpec((tm, d), row),
                  pl.BlockSpec((d, d), const),
                  pl.BlockSpec((d, d), const),
                  pl.BlockSpec((mem_len, d), lambda b, i: (b, 0)),
                  pl.BlockSpec((mem_len, d), lambda b, i: (b, 1)),
                  pl.BlockSpec((1, d), const),
                  pl.BlockSpec((1, d), const),
                  pl.BlockSpec((d, LANES), const),
                  pl.BlockSpec((d, LANES), const),
                  pl.BlockSpec((1, LANES), const)],
        out_specs=(pl.BlockSpec((tm, d), row),
                   pl.BlockSpec((tm, LANES), row),
                   pl.BlockSpec((tm, LANES), row),
                   pl.BlockSpec((tm, LANES), row),
                   pl.BlockSpec((SUBLANES, LANES), row)),
        compiler_params=_params("parallel", "parallel"),
        name="cross_attn_router",
    )(x1, w_q.astype(BF16), w_o.astype(BF16), kv, kv, ln_g.reshape(1, d), ln_b.reshape(1, d), rw_hi, rw_lo, rb)


ROW_TILE = D_MODEL // LANES
SEG_ROWS = 32


def _to_row_tiles(dst_ref, x, rows):
    for c in range(ROW_TILE):
        dst_ref[pl.ds(c, rows, stride=ROW_TILE), :] = x[:, c * LANES:(c + 1) * LANES]


def _from_row_tiles(src_ref, rows):
    return jnp.concatenate([src_ref[pl.ds(c, rows, stride=ROW_TILE), :] for c in range(ROW_TILE)], axis=-1)


def _tile_rows(row):
    return pl.multiple_of(row * ROW_TILE, ROW_TILE)


def _num_segments(count):
    return lax.shift_right_logical(count + (SEG_ROWS - 1), SEG_ROWS.bit_length() - 1)


def _tile_segments(tcnt_ref):
    return lax.fori_loop(0, N_EXPERTS, lambda e, t: t + _num_segments(tcnt_ref[0, 0, e]), 0)


def _dispatch_kernel(pstart_ref, counts_ref, padded_ref, ldest_ref, tcnt_ref, tbase_ref, lstart_ref, x_ref,
                     xs_hbm, xrt, stage, zeros, nseg, sems, zsem, rsem, *, tm, nt):
    i = pl.program_id(0)
    slot = lax.rem(i, 2)

    @pl.when(i == 0)
    def _():
        stage[...] = jnp.zeros_like(stage)
        zeros[...] = jnp.zeros_like(zeros)

    _to_row_tiles(xrt, x_ref[...], tm)
    st = stage.at[slot]
    for k in range(TOP_K):
        def move(r, c, k=k):
            dst = _tile_rows(ldest_ref[0, 0, k * tm + r])
            st[pl.ds(dst, ROW_TILE), :] = xrt[pl.ds(_tile_rows(r), ROW_TILE), :]
            return c

        lax.fori_loop(0, tm, move, 0, unroll=8)

    def seg_copy(s, src_row, dst_row):
        return pltpu.make_async_copy(stage.at[s, pl.ds(_tile_rows(src_row), SEG_ROWS * ROW_TILE), :],
                                     xs_hbm.at[pl.ds(_tile_rows(dst_row), SEG_ROWS * ROW_TILE), :], sems.at[s])

    def wait_segments(s, count):
        def wait(j, c):
            seg_copy(s, 0, 0).wait()
            return c

        lax.fori_loop(0, count, wait, 0)

    @pl.when(i > 0)
    def _():
        wait_segments(1 - slot, nseg[1 - slot])

    def issue(e, total):
        n = _num_segments(tcnt_ref[0, 0, e])
        src0 = lstart_ref[0, 0, e]
        dst0 = pstart_ref[e] + tbase_ref[0, 0, e]

        def one(j, c):
            seg_copy(slot, src0 + j * SEG_ROWS, dst0 + j * SEG_ROWS).start()
            return c

        lax.fori_loop(0, n, one, 0)
        return total + n

    total = lax.fori_loop(0, N_EXPERTS, issue, 0)
    nseg[slot] = total

    @pl.when(i == nt - 1)
    def _():
        wait_segments(slot, total)

        def gap(e):
            lo = pstart_ref[e] + counts_ref[e]
            width = padded_ref[e] - counts_ref[e]
            full = lax.shift_right_logical(width, SEG_ROWS.bit_length() - 1)
            return lo, full, width - full * SEG_ROWS

        def seg_zero(row):
            return pltpu.make_async_copy(zeros, xs_hbm.at[pl.ds(_tile_rows(row), SEG_ROWS * ROW_TILE), :], zsem)

        def row_zero(row):
            return pltpu.make_async_copy(zeros.at[pl.ds(0, ROW_TILE), :],
                                         xs_hbm.at[pl.ds(_tile_rows(row), ROW_TILE), :], rsem)

        def fill(e, c):
            lo, full, rest = gap(e)
            lax.fori_loop(0, full, lambda j, c2: (seg_zero(lo + j * SEG_ROWS).start(), c2)[1], 0)
            lax.fori_loop(0, rest, lambda j, c2: (row_zero(lo + full * SEG_ROWS + j).start(), c2)[1], 0)
            return c

        def fill_wait(e, c):
            _, full, rest = gap(e)
            lax.fori_loop(0, full, lambda j, c2: (seg_zero(0).wait(), c2)[1], 0)
            lax.fori_loop(0, rest, lambda j, c2: (row_zero(0).wait(), c2)[1], 0)
            return c

        lax.fori_loop(0, N_EXPERTS, fill, 0)
        lax.fori_loop(0, N_EXPERTS, fill_wait, 0)


def _stage_rows(tm):
    return TOP_K * tm + N_EXPERTS * SEG_ROWS


def _dispatch(x2, pstart, counts, padded, ldest_t, tcnt_t, tbase_t, lstart_t, n_pad, tm):
    n, d = x2.shape
    nt = n // tm
    smem_tile = lambda width: pl.BlockSpec((1, 1, width), lambda i, *_: (i, 0, 0), memory_space=pltpu.SMEM)
    grid_spec = pltpu.PrefetchScalarGridSpec(
        num_scalar_prefetch=3,
        grid=(nt,),
        in_specs=[smem_tile(TOP_K * tm), smem_tile(LANES), smem_tile(LANES), smem_tile(LANES),
                  pl.BlockSpec((tm, d), lambda i, *_: (i, 0))],
        out_specs=pl.BlockSpec(memory_space=pl.ANY),
        scratch_shapes=[pltpu.VMEM((tm * ROW_TILE, LANES), F32),
                        pltpu.VMEM((2, _stage_rows(tm) * ROW_TILE, LANES), F32),
                        pltpu.VMEM((SEG_ROWS * ROW_TILE, LANES), F32),
                        pltpu.SMEM((2,), jnp.int32),
                        pltpu.SemaphoreType.DMA((2,)),
                        pltpu.SemaphoreType.DMA,
                        pltpu.SemaphoreType.DMA],
    )
    return pl.pallas_call(
        functools.partial(_dispatch_kernel, tm=tm, nt=nt),
        out_shape=jax.ShapeDtypeStruct((n_pad * ROW_TILE, LANES), F32),
        grid_spec=grid_spec,
        compiler_params=_params("arbitrary"),
        name="moe_dispatch",
    )(pstart, counts, padded, ldest_t, tcnt_t, tbase_t, lstart_t, x2)


def _expert_kernel(be_ref, valid_ref, xs_ref, wup_ref, bup_ref, wdn_ref, bdn_ref, y_ref, wup_bf, wdn_bf):
    i = pl.program_id(0)

    @pl.when(valid_ref[i] == 0)
    def _():
        y_ref[...] = jnp.zeros_like(y_ref)

    @pl.when(valid_ref[i] > 0)
    def _():
        prev = be_ref[jnp.maximum(i - 1, 0)]

        @pl.when(jnp.logical_or(i == 0, be_ref[i] != prev))
        def _():
            wup_bf[...] = wup_ref[0].astype(BF16)
            wdn_bf[...] = wdn_ref[0].astype(BF16)

        xb = _from_row_tiles(xs_ref, MOE_BLOCK).astype(BF16)
        h = _dot(xb, wup_bf[...]) + bup_ref[0]
        glu = jnp.minimum(h[:, :D_FF], SWIGLU_LIMIT)
        lin = jnp.clip(h[:, D_FF:], -SWIGLU_LIMIT, SWIGLU_LIMIT)
        act = glu * _sigmoid(SWIGLU_ALPHA * glu) * (lin + 1.0)
        _to_row_tiles(y_ref, _dot(act.astype(BF16), wdn_bf[...]) + bdn_ref[0], MOE_BLOCK)


def _experts(xs, blk_expert, blk_valid, w_up, b_up, w_down, b_down):
    n_blocks = blk_expert.shape[0]
    d = w_up.shape[1]
    f2 = w_up.shape[2]
    blk_rows = MOE_BLOCK * ROW_TILE
    grid_spec = pltpu.PrefetchScalarGridSpec(
        num_scalar_prefetch=2,
        grid=(n_blocks,),
        in_specs=[pl.BlockSpec((blk_rows, LANES), lambda i, be, va: (jnp.where(va[i] > 0, i, 0), 0)),
                  pl.BlockSpec((1, d, f2), lambda i, be, va: (be[i], 0, 0)),
                  pl.BlockSpec((1, 1, f2), lambda i, be, va: (be[i], 0, 0)),
                  pl.BlockSpec((1, D_FF, d), lambda i, be, va: (be[i], 0, 0)),
                  pl.BlockSpec((1, 1, d), lambda i, be, va: (be[i], 0, 0))],
        out_specs=pl.BlockSpec((blk_rows, LANES), lambda i, be, va: (i, 0)),
        scratch_shapes=[pltpu.VMEM((d, f2), BF16),
                        pltpu.VMEM((D_FF, d), BF16)],
    )
    return pl.pallas_call(
        _expert_kernel,
        out_shape=jax.ShapeDtypeStruct((n_blocks * blk_rows, LANES), F32),
        grid_spec=grid_spec,
        compiler_params=_params("arbitrary"),
        name="moe_experts",
    )(blk_expert, blk_valid, xs, w_up, b_up.reshape(N_EXPERTS, 1, f2), w_down, b_down.reshape(N_EXPERTS, 1, d))


def _combine_ln_kernel(pstart_ref, ldest_ref, gate_ref, tcnt_ref, tbase_ref, lstart_ref,
                       tcnt_next_ref, tbase_next_ref, lstart_next_ref, ys_hbm, x_ref, g_ref, b_ref,
                       o_ref, stage, yrt, sems, *, tm, nt):
    i = pl.program_id(0)
    slot = lax.rem(i, 2)

    def seg_copy(s, src_row, dst_row):
        return pltpu.make_async_copy(ys_hbm.at[pl.ds(_tile_rows(src_row), SEG_ROWS * ROW_TILE), :],
                                     stage.at[s, pl.ds(_tile_rows(dst_row), SEG_ROWS * ROW_TILE), :], sems.at[s])

    def fetch(s, tcnt, tbase, lstart):
        def per_expert(e, c):
            src0 = pstart_ref[e] + tbase[0, 0, e]
            dst0 = lstart[0, 0, e]

            def one(j, c2):
                seg_copy(s, src0 + j * SEG_ROWS, dst0 + j * SEG_ROWS).start()
                return c2

            lax.fori_loop(0, _num_segments(tcnt[0, 0, e]), one, 0)
            return c

        lax.fori_loop(0, N_EXPERTS, per_expert, 0)

    @pl.when(i == 0)
    def _():
        fetch(0, tcnt_ref, tbase_ref, lstart_ref)

    @pl.when(i + 1 < nt)
    def _():
        fetch(1 - slot, tcnt_next_ref, tbase_next_ref, lstart_next_ref)

    def wait(j, c):
        seg_copy(slot, 0, 0).wait()
        return c

    lax.fori_loop(0, _tile_segments(tcnt_ref), wait, 0)

    st = stage.at[slot]

    def reduce(r, c):
        acc = None
        for k in range(TOP_K):
            row = st[pl.ds(_tile_rows(ldest_ref[0, 0, k * tm + r]), ROW_TILE), :]
            term = gate_ref[0, 0, k * tm + r] * row
            acc = term if acc is None else acc + term
        yrt[pl.ds(_tile_rows(r), ROW_TILE), :] = acc
        return c

    lax.fori_loop(0, tm, reduce, 0, unroll=4)
    y = _from_row_tiles(yrt, tm)
    o_ref[...] = _layer_norm(ALPHA * x_ref[...] + y, g_ref[...], b_ref[...])


def _combine_ln(ys, pstart, ldest_t, gates_t, tcnt_t, tbase_t, lstart_t, x2, ln_g, ln_b, tm):
    n, d = x2.shape
    nt = n // tm
    cur = lambda width: pl.BlockSpec((1, 1, width), lambda i, *_: (i, 0, 0), memory_space=pltpu.SMEM)
    nxt = lambda width: pl.BlockSpec((1, 1, width), lambda i, *_: (jnp.minimum(i + 1, nt - 1), 0, 0),
                                     memory_space=pltpu.SMEM)
    grid_spec = pltpu.PrefetchScalarGridSpec(
        num_scalar_prefetch=1,
        grid=(nt,),
        in_specs=[cur(TOP_K * tm), cur(TOP_K * tm), cur(LANES), cur(LANES), cur(LANES),
                  nxt(LANES), nxt(LANES), nxt(LANES),
                  pl.BlockSpec(memory_space=pl.ANY),
                  pl.BlockSpec((tm, d), lambda i, *_: (i, 0)),
                  pl.BlockSpec((1, d), lambda i, *_: (0, 0)),
                  pl.BlockSpec((1, d), lambda i, *_: (0, 0))],
        out_specs=pl.BlockSpec((tm, d), lambda i, *_: (i, 0)),
        scratch_shapes=[pltpu.VMEM((2, _stage_rows(tm) * ROW_TILE, LANES), F32),
                        pltpu.VMEM((tm * ROW_TILE, LANES), F32),
                        pltpu.SemaphoreType.DMA((2,))],
    )
    return pl.pallas_call(
        functools.partial(_combine_ln_kernel, tm=tm, nt=nt),
        out_shape=jax.ShapeDtypeStruct((n, d), F32),
        grid_spec=grid_spec,
        compiler_params=_params("arbitrary"),
        name="moe_combine_ln3",
    )(pstart, ldest_t, gates_t, tcnt_t, tbase_t, lstart_t, tcnt_t, tbase_t, lstart_t, ys, x2,
      ln_g.reshape(1, d), ln_b.reshape(1, d))


def _round_up(v, m):
    return (v + m - 1) // m * m


def _moe_layer(x2, idx_pad, lrank_pad, tcnt_pad, gates_pad, w_up, b_up, w_down, b_down, ln_g, ln_b, tm=256):
    n = x2.shape[0]
    nt = n // tm
    i32 = jnp.int32
    tcnt = tcnt_pad.reshape(nt, SUBLANES, LANES)[:, 0, :N_EXPERTS]
    tbase = jnp.cumsum(tcnt, axis=0) - tcnt
    counts = jnp.sum(tcnt, axis=0)
    padded = _round_up(counts + SEG_ROWS, MOE_BLOCK)
    pend = jnp.cumsum(padded)
    pstart = pend - padded
    n_blocks = -(-(n * TOP_K + N_EXPERTS * (SEG_ROWS + MOE_BLOCK - 1)) // MOE_BLOCK)
    block_start = jnp.arange(n_blocks, dtype=i32) * MOE_BLOCK
    blk_expert = jnp.minimum(jnp.sum(pend[None, :] <= block_start[:, None], axis=1), N_EXPERTS - 1).astype(i32)
    blk_valid = jnp.clip(counts[blk_expert] - (block_start - pstart[blk_expert]), 0, MOE_BLOCK).astype(i32)
    lseg = _round_up(tcnt, SEG_ROWS)
    lstart = jnp.cumsum(lseg, axis=1) - lseg
    idx = idx_pad[:, :TOP_K].reshape(nt, tm, TOP_K)
    chosen = idx[..., None] == jnp.arange(N_EXPERTS, dtype=i32)
    ldest = jnp.sum(jnp.where(chosen, lstart[:, None, None, :], 0), axis=-1) + lrank_pad[:, :TOP_K].reshape(nt, tm, TOP_K)
    per_tile = lambda a: a.transpose(0, 2, 1).reshape(nt, 1, TOP_K * tm)
    ldest_t = per_tile(ldest.astype(i32))
    gates_t = per_tile(gates_pad[:, :TOP_K].reshape(nt, tm, TOP_K))
    lane_pad = lambda a: jnp.pad(a.astype(i32), ((0, 0), (0, LANES - N_EXPERTS))).reshape(nt, 1, LANES)
    tcnt_t, tbase_t, lstart_t = lane_pad(tcnt), lane_pad(tbase), lane_pad(lstart)
    pstart, counts, padded = pstart.astype(i32), counts.astype(i32), padded.astype(i32)

    xs = _dispatch(x2, pstart, counts, padded, ldest_t, tcnt_t, tbase_t, lstart_t, n_blocks * MOE_BLOCK, tm)
    ys = _experts(xs, blk_expert, blk_valid, w_up, b_up, w_down, b_down)
    return _combine_ln(ys, pstart, ldest_t, gates_t, tcnt_t, tbase_t, lstart_t, x2, ln_g, ln_b, tm)


def _s5_kernel(x_ref, win_ref, bblk_ref, ar_ref, ai_ref, cblk_ref, dsk_ref, wval_ref, wgate_ref, g_ref, b_ref,
               o_ref, bur, bui, sr, si, *, tt, batch):
    rows = tt * batch

    @pl.when(pl.program_id(0) == 0)
    def _():
        sr[...] = jnp.zeros_like(sr)
        si[...] = jnp.zeros_like(si)

    x = x_ref[...]
    u = _dot(x.astype(BF16), win_ref[...])
    ub = u.astype(BF16)
    for c in range(S5_N_CHUNKS):
        bu = _dot(ub[:, c * S5_CHUNK_IN:(c + 1) * S5_CHUNK_IN], bblk_ref[c])
        bur[:, c * S5_CHUNK_STATE:(c + 1) * S5_CHUNK_STATE] = bu[:, :S5_CHUNK_STATE]
        bui[:, c * S5_CHUNK_STATE:(c + 1) * S5_CHUNK_STATE] = bu[:, S5_CHUNK_STATE:]

    for c in range(S5_N_CHUNKS):
        cols = pl.ds(c * S5_CHUNK_STATE, S5_CHUNK_STATE)
        a_r = ar_ref[:, cols]
        a_i = ai_ref[:, cols]

        def step(t, carry):
            s_r, s_i = carry
            rsl = pl.ds(pl.multiple_of(t * batch, batch), batch)
            n_r = a_r * s_r - a_i * s_i + bur[rsl, cols]
            n_i = a_r * s_i + a_i * s_r + bui[rsl, cols]
            bur[rsl, cols] = n_r
            bui[rsl, cols] = n_i
            return n_r, n_i

        f_r, f_i = lax.fori_loop(0, tt, step, (sr[:, cols], si[:, cols]), unroll=True)
        sr[:, cols] = f_r
        si[:, cols] = f_i

    ys = []
    for c in range(S5_N_CHUNKS):
        cols = pl.ds(c * S5_CHUNK_STATE, S5_CHUNK_STATE)
        xri = jnp.concatenate([bur[:, cols].astype(BF16), bui[:, cols].astype(BF16)], axis=-1)
        ys.append(_dot(xri, cblk_ref[c]))
    y = jnp.concatenate(ys, axis=-1) + dsk_ref[...] * u
    yb = _gelu(y).astype(BF16)
    hmix = _dot(yb, wval_ref[...]) * _sigmoid(_dot(yb, wgate_ref[...]))
    o_ref[...] = _layer_norm(ALPHA * x + hmix, g_ref[...], b_ref[...])


def _s5_discretize(log_dt, lambda_re, lambda_im, b_re, b_im, c_re, c_im):
    dt = jnp.exp(log_dt)[:, None]
    mag = jnp.exp(lambda_re * dt)
    ar = mag * jnp.cos(lambda_im * dt)
    ai = mag * jnp.sin(lambda_im * dt)
    den = lambda_re * lambda_re + lambda_im * lambda_im
    zr = ((ar - 1.0) * lambda_re + ai * lambda_im) / den
    zi = (ai * lambda_re - (ar - 1.0) * lambda_im) / den
    bbar_re = zr[..., None] * b_re - zi[..., None] * b_im
    bbar_im = zr[..., None] * b_im + zi[..., None] * b_re
    ng, gc, p, gw = S5_N_CHUNKS, S5_CHUNK_GROUPS, S5_STATE, S5_GROUP
    eye = jnp.eye(gc, dtype=F32)
    bre = bbar_re.reshape(ng, gc, p, gw).transpose(0, 1, 3, 2)
    bim = bbar_im.reshape(ng, gc, p, gw).transpose(0, 1, 3, 2)
    blk_re = jnp.einsum('cgip,gh->cgihp', bre, eye).reshape(ng, gc * gw, gc * p)
    blk_im = jnp.einsum('cgip,gh->cgihp', bim, eye).reshape(ng, gc * gw, gc * p)
    bblk = jnp.concatenate([blk_re, blk_im], axis=-1).astype(BF16)
    cre = c_re.reshape(ng, gc, gw, p).transpose(0, 1, 3, 2)
    cim = c_im.reshape(ng, gc, gw, p).transpose(0, 1, 3, 2)
    cblk_re = jnp.einsum('cgpi,gh->cgphi', cre, eye).reshape(ng, gc * p, gc * gw)
    cblk_im = jnp.einsum('cgpi,gh->cgphi', cim, eye).reshape(ng, gc * p, gc * gw)
    cblk = jnp.concatenate([cblk_re, -cblk_im], axis=1).astype(BF16)
    return bblk, ar.reshape(1, S5_STATES), ai.reshape(1, S5_STATES), cblk


def _s5_mixer_ln(x_tm, w_in, log_dt, lambda_re, lambda_im, b_re, b_im, c_re, c_im, d_skip, w_val, w_gate,
                 ln_g, ln_b, batch, tt=16):
    n, d = x_tm.shape
    rows = tt * batch
    bblk, ar, ai, cblk = _s5_discretize(log_dt, lambda_re, lambda_im, b_re, b_im, c_re, c_im)
    const2 = lambda i: (0, 0)
    const3 = lambda i: (0, 0, 0)
    row = lambda i: (i, 0)
    return pl.pallas_call(
        functools.partial(_s5_kernel, tt=tt, batch=batch),
        out_shape=jax.ShapeDtypeStruct((n, d), F32),
        grid=(n // rows,),
        in_specs=[pl.BlockSpec((rows, d), row),
                  pl.BlockSpec((d, d), const2),
                  pl.BlockSpec(bblk.shape, const3),
                  pl.BlockSpec(ar.shape, const2),
                  pl.BlockSpec(ai.shape, const2),
                  pl.BlockSpec(cblk.shape, const3),
                  pl.BlockSpec((1, d), const2),
                  pl.BlockSpec((d, d), const2),
                  pl.BlockSpec((d, d), const2),
                  pl.BlockSpec((1, d), const2),
                  pl.BlockSpec((1, d), const2)],
        out_specs=pl.BlockSpec((rows, d), row),
        scratch_shapes=[pltpu.VMEM((rows, S5_STATES), F32),
                        pltpu.VMEM((rows, S5_STATES), F32),
                        pltpu.VMEM((batch, S5_STATES), F32),
                        pltpu.VMEM((batch, S5_STATES), F32)],
        compiler_params=_params("arbitrary"),
        name="s5_mixer_ln1",
    )(x_tm, w_in.astype(BF16), bblk, ar, ai, cblk, d_skip.reshape(1, d), w_val.astype(BF16), w_gate.astype(BF16),
      ln_g.reshape(1, d), ln_b.reshape(1, d))


def _diff_lambda_init(layer_idx):
    return 0.8 - 0.6 * math.exp(-0.3 * layer_idx)


def kernel(x, mem, w_mem_kv, l0_w_in, l0_sgu_ln_g, l0_sgu_ln_b, l0_w_spatial, l0_b_spatial, l0_lam_q1, l0_lam_k1, l0_lam_q2, l0_lam_k2, l0_subln_g, l0_w_out, l0_ln1_g, l0_ln1_b, l0_xq, l0_xo, l0_ln2_g, l0_ln2_b, l0_router_w, l0_router_b, l0_exp_w_up, l0_exp_b_up, l0_exp_w_down, l0_exp_b_down, l0_ln3_g, l0_ln3_b, l1_w_in, l1_log_dt, l1_lambda_re, l1_lambda_im, l1_b_re, l1_b_im, l1_c_re, l1_c_im, l1_d_skip, l1_w_val, l1_w_gate, l1_ln1_g, l1_ln1_b, l1_xq, l1_xo, l1_ln2_g, l1_ln2_b, l1_router_w, l1_router_b, l1_exp_w_up, l1_exp_b_up, l1_exp_w_down, l1_exp_b_down, l1_ln3_g, l1_ln3_b):
    batch, seq, d = x.shape
    mem_len = mem.shape[1]
    n = batch * seq
    x0 = x.reshape(n, d)

    kv = _matmul(mem.reshape(batch * mem_len, d), w_mem_kv.astype(BF16), BF16,
                 tm=min(512, batch * mem_len), tn=d)

    a, qkv = _inproj_sgu(x0, l0_w_in, l0_sgu_ln_g, l0_sgu_ln_b, l0_w_spatial, l0_b_spatial)
    dattn = _diff_attention(qkv, l0_lam_q1, l0_lam_k1, l0_lam_q2, l0_lam_k2, l0_subln_g,
                            _diff_lambda_init(0), batch, seq)
    x1 = _outproj_ln(a, dattn, l0_w_out, x0, l0_ln1_g, l0_ln1_b)
    x2, idx, gates, rank, cnt = _cross_router(x1, kv, l0_xq, l0_xo, l0_ln2_g, l0_ln2_b, l0_router_w, l0_router_b,
                                              batch, seq, mem_len)
    x3 = _moe_layer(x2, idx, rank, cnt, gates, l0_exp_w_up, l0_exp_b_up, l0_exp_w_down, l0_exp_b_down,
                    l0_ln3_g, l0_ln3_b)

    x3_tm = x3.reshape(batch, seq, d).transpose(1, 0, 2).reshape(n, d)
    x4_tm = _s5_mixer_ln(x3_tm, l1_w_in, l1_log_dt, l1_lambda_re, l1_lambda_im, l1_b_re, l1_b_im,
                         l1_c_re, l1_c_im, l1_d_skip, l1_w_val, l1_w_gate, l1_ln1_g, l1_ln1_b, batch)
    x4 = x4_tm.reshape(seq, batch, d).transpose(1, 0, 2).reshape(n, d)
    x5, idx, gates, rank, cnt = _cross_router(x4, kv, l1_xq, l1_xo, l1_ln2_g, l1_ln2_b, l1_router_w, l1_router_b,
                                              batch, seq, mem_len)
    x6 = _moe_layer(x5, idx, rank, cnt, gates, l1_exp_w_up, l1_exp_b_up, l1_exp_w_down, l1_exp_b_down,
                    l1_ln3_g, l1_ln3_b)
    return x6.reshape(batch, seq, d)
```

```python
import functools
import math

import jax
import jax.numpy as jnp
from jax import lax
from jax.experimental import pallas as pl
from jax.experimental.pallas import tpu as pltpu

F32 = jnp.float32
BF16 = jnp.bfloat16

D_MODEL = 1024
CHUNK = 128
SGU_GROUPS = 4
SGU_WIDTH = 512
DIFF_HEADS = 4
DIFF_HEAD_DIM = 64
DIFF_V_DIM = 128
S5_GROUP = 16
S5_GROUPS = 64
S5_STATE = 64
X_HEADS = 4
X_HEAD_DIM = 256
N_EXPERTS = 32
TOP_K = 4
D_FF = 1024
SWIGLU_LIMIT = 7.0
SWIGLU_ALPHA = 1.702
MOE_BLOCK = 512
MOE_HALF = MOE_BLOCK // 2
DEPTH = 2
ALPHA = (2 * DEPTH) ** 0.25
LN_EPS = 1e-5
NEG_INF = -1e30

LANES = 128
SUBLANES = 8
VMEM_LIMIT_BYTES = 56 * 1024 * 1024

S5_CHUNK_GROUPS = 8
S5_CHUNK_IN = S5_CHUNK_GROUPS * S5_GROUP
S5_CHUNK_STATE = S5_CHUNK_GROUPS * S5_STATE
S5_N_CHUNKS = S5_GROUPS // S5_CHUNK_GROUPS
S5_STATES = S5_GROUPS * S5_STATE


def _params(*sem):
    return pltpu.CompilerParams(dimension_semantics=sem, vmem_limit_bytes=VMEM_LIMIT_BYTES)


def _gelu(x):
    return 0.5 * x * (1.0 + jnp.tanh(math.sqrt(2.0 / math.pi) * (x + 0.044715 * (x * x * x))))


def _sigmoid(x):
    return 1.0 / (1.0 + jnp.exp(-x))


def _layer_norm(z, g, b):
    mu = jnp.mean(z, axis=-1, keepdims=True)
    zc = z - mu
    var = jnp.mean(zc * zc, axis=-1, keepdims=True)
    return zc * lax.rsqrt(var + LN_EPS) * g + b


def _dot(a, b):
    return jnp.dot(a, b, preferred_element_type=F32)


def _dot_nt(a, b):
    return lax.dot_general(a, b, (((1,), (1,)), ((), ())), preferred_element_type=F32)


def _matmul_kernel(x_ref, w_ref, o_ref):
    o_ref[...] = _dot(x_ref[...].astype(BF16), w_ref[...]).astype(o_ref.dtype)


def _matmul(x, w, out_dtype, tm, tn):
    m, k = x.shape
    n = w.shape[1]
    return pl.pallas_call(
        _matmul_kernel,
        out_shape=jax.ShapeDtypeStruct((m, n), out_dtype),
        grid=(n // tn, m // tm),
        in_specs=[pl.BlockSpec((tm, k), lambda j, i: (i, 0)),
                  pl.BlockSpec((k, tn), lambda j, i: (0, j))],
        out_specs=pl.BlockSpec((tm, tn), lambda j, i: (i, j)),
        compiler_params=_params("parallel", "parallel"),
        name="matmul",
    )(x, w)


def _inproj_sgu_kernel(x_ref, w_ref, lng_ref, lnb_ref, wsp_ref, bsp_ref, a_ref, qkv_ref, *, tm):
    xb = x_ref[...].astype(BF16)
    qkv_ref[...] = _dot(xb, w_ref[:, 2 * SGU_WIDTH:]).astype(BF16)
    h = _dot(xb, w_ref[:, :2 * SGU_WIDTH])
    for g in range(SGU_GROUPS):
        lo = g * CHUNK
        u = _gelu(h[:, lo:lo + CHUNK])
        v = _gelu(h[:, SGU_WIDTH + lo:SGU_WIDTH + lo + CHUNK])
        vn = _layer_norm(v, lng_ref[g:g + 1, :], lnb_ref[g:g + 1, :]).astype(BF16)
        w_g = wsp_ref[g]
        b_g = bsp_ref[:, g:g + 1]
        for c in range(tm // CHUNK):
            r = c * CHUNK
            gate = _dot(w_g, vn[r:r + CHUNK, :]) + b_g
            a_ref[r:r + CHUNK, lo:lo + CHUNK] = (u[r:r + CHUNK, :] * gate).astype(BF16)


def _inproj_sgu(x2d, w_in, ln_g, ln_b, w_spatial, b_spatial, tm=256):
    n, d = x2d.shape
    w = w_in.astype(BF16)
    wsp = jnp.tril(w_spatial).astype(BF16)
    bsp_t = b_spatial.T
    n_qkv = w.shape[1] - 2 * SGU_WIDTH
    const = lambda i: (0, 0)
    return pl.pallas_call(
        functools.partial(_inproj_sgu_kernel, tm=tm),
        out_shape=(jax.ShapeDtypeStruct((n, SGU_WIDTH), BF16),
                   jax.ShapeDtypeStruct((n, n_qkv), BF16)),
        grid=(n // tm,),
        in_specs=[pl.BlockSpec((tm, d), lambda i: (i, 0)),
                  pl.BlockSpec(w.shape, const),
                  pl.BlockSpec((SGU_GROUPS, CHUNK), const),
                  pl.BlockSpec((SGU_GROUPS, CHUNK), const),
                  pl.BlockSpec(wsp.shape, lambda i: (0, 0, 0)),
                  pl.BlockSpec(bsp_t.shape, const)],
        out_specs=(pl.BlockSpec((tm, SGU_WIDTH), lambda i: (i, 0)),
                   pl.BlockSpec((tm, n_qkv), lambda i: (i, 0))),
        compiler_params=_params("parallel"),
        name="inproj_sgu",
    )(x2d, w, ln_g.reshape(SGU_GROUPS, CHUNK), ln_b.reshape(SGU_GROUPS, CHUNK), wsp, bsp_t)


def _diff_attn_kernel(q_ref, k_ref, v_ref, lam_ref, g_ref, o_ref, *, tk, lam_init):
    i = pl.program_id(2)
    lane = lax.broadcasted_iota(jnp.int32, (1, 2 * DIFF_HEAD_DIM), 1)

    def split(q):
        q = q * jnp.asarray(DIFF_HEAD_DIM ** -0.5, BF16)
        zero = jnp.zeros_like(q)
        return jnp.where(lane < DIFF_HEAD_DIM, q, zero), jnp.where(lane >= DIFF_HEAD_DIM, q, zero)

    q_sub = (split(q_ref[0:tk, :]), split(q_ref[tk:2 * tk, :]))

    def kv_block(j):
        off = pl.multiple_of(j * tk, tk)
        return k_ref[pl.ds(off, tk), :], v_ref[pl.ds(off, tk), :]

    def update(state, s, vj):
        m, l, acc = state
        m_new = jnp.maximum(m, jnp.max(s, axis=-1, keepdims=True))
        alpha = jnp.exp(m - m_new)
        p = jnp.exp(s - m_new)
        l_new = alpha * l + jnp.sum(p, axis=-1, keepdims=True)
        acc_new = alpha * acc + _dot(p.astype(BF16), vj)
        return m_new, l_new, acc_new

    def attend(states, q12, kj, vj, masked):
        out = []
        for st, q in zip(states, q12):
            s = _dot_nt(q, kj)
            if masked:
                row = lax.broadcasted_iota(jnp.int32, (tk, tk), 0)
                col = lax.broadcasted_iota(jnp.int32, (tk, tk), 1)
                s = jnp.where(col <= row, s, NEG_INF)
            out.append(update(st, s, vj))
        return tuple(out)

    init = (jnp.full((tk, 1), NEG_INF, F32), jnp.zeros((tk, 1), F32), jnp.zeros((tk, DIFF_V_DIM), F32))

    def body(j, carry):
        kj, vj = kv_block(j)
        return tuple(attend(st, q12, kj, vj, False) for st, q12 in zip(carry, q_sub))

    st_a, st_b = lax.fori_loop(0, 2 * i, body, ((init, init), (init, init)))
    k0, v0 = kv_block(2 * i)
    k1, v1 = kv_block(2 * i + 1)
    st_a = attend(st_a, q_sub[0], k0, v0, True)
    st_b = attend(st_b, q_sub[1], k0, v0, False)
    st_b = attend(st_b, q_sub[1], k1, v1, True)

    lam_v = lam_ref[...]
    s_a = jnp.sum(lam_v[0:1, :] * lam_v[1:2, :], axis=-1, keepdims=True)
    s_b = jnp.sum(lam_v[2:3, :] * lam_v[3:4, :], axis=-1, keepdims=True)
    lam = jnp.exp(s_a) - jnp.exp(s_b) + lam_init
    for h, ((_, l1, a1), (_, l2, a2)) in enumerate((st_a, st_b)):
        o = a1 / l1 - lam * (a2 / l2)
        o = o * lax.rsqrt(jnp.mean(o * o, axis=-1, keepdims=True) + LN_EPS) * g_ref[...] * (1.0 - lam_init)
        o_ref[h * tk:(h + 1) * tk, :] = o.astype(BF16)


def _diff_attention(qkv, lam_q1, lam_k1, lam_q2, lam_k2, subln_g, lam_init, batch, seq, tk=256):
    n = qkv.shape[0]
    tq = 2 * tk
    nq = seq // tq
    lam_v = jnp.stack([lam_q1, lam_k1, lam_q2, lam_k2])
    hw = 2 * DIFF_HEAD_DIM
    return pl.pallas_call(
        functools.partial(_diff_attn_kernel, tk=tk, lam_init=lam_init),
        out_shape=jax.ShapeDtypeStruct((n, DIFF_HEADS * DIFF_V_DIM), BF16),
        grid=(batch, DIFF_HEADS, nq),
        in_specs=[pl.BlockSpec((tq, hw), lambda b, h, i: (b * nq + i, h)),
                  pl.BlockSpec((seq, hw), lambda b, h, i: (b, DIFF_HEADS + h)),
                  pl.BlockSpec((seq, DIFF_V_DIM), lambda b, h, i: (b, 2 * DIFF_HEADS + h)),
                  pl.BlockSpec(lam_v.shape, lambda b, h, i: (0, 0)),
                  pl.BlockSpec((1, DIFF_V_DIM), lambda b, h, i: (0, 0))],
        out_specs=pl.BlockSpec((tq, DIFF_V_DIM), lambda b, h, i: (b * nq + i, h)),
        compiler_params=_params("parallel", "parallel", "parallel"),
        name="diff_attention",
    )(qkv, qkv, qkv, lam_v, subln_g.reshape(1, DIFF_V_DIM))


def _outproj_ln_kernel(a_ref, d_ref, w_ref, x_ref, g_ref, b_ref, o_ref):
    y = _dot(a_ref[...], w_ref[:SGU_WIDTH, :]) + _dot(d_ref[...], w_ref[SGU_WIDTH:, :])
    o_ref[...] = _layer_norm(ALPHA * x_ref[...] + y, g_ref[...], b_ref[...])


def _outproj_ln(a, dattn, w_out, x2d, ln_g, ln_b, tm=512):
    n, d = x2d.shape
    w = w_out.astype(BF16)
    const = lambda i: (0, 0)
    row = lambda i: (i, 0)
    return pl.pallas_call(
        _outproj_ln_kernel,
        out_shape=jax.ShapeDtypeStruct((n, d), F32),
        grid=(n // tm,),
        in_specs=[pl.BlockSpec((tm, a.shape[1]), row),
                  pl.BlockSpec((tm, dattn.shape[1]), row),
                  pl.BlockSpec(w.shape, const),
                  pl.BlockSpec((tm, d), row),
                  pl.BlockSpec((1, d), const),
                  pl.BlockSpec((1, d), const)],
        out_specs=pl.BlockSpec((tm, d), row),
        compiler_params=_params("parallel"),
        name="outproj_ln1",
    )(a, dattn, w, x2d, ln_g.reshape(1, d), ln_b.reshape(1, d))


def _cross_router_kernel(x_ref, wq_ref, wo_ref, k_ref, v_ref, g_ref, b_ref, rwh_ref, rwl_ref, rb_ref,
                         x2_ref, idx_ref, gate_ref, rank_ref, cnt_ref):
    x = x_ref[...]
    q = (_dot(x.astype(BF16), wq_ref[...]) * (X_HEAD_DIM ** -0.5)).astype(BF16)
    heads = []
    for h in range(X_HEADS):
        lo = h * X_HEAD_DIM
        s = _dot_nt(q[:, lo:lo + X_HEAD_DIM], k_ref[:, lo:lo + X_HEAD_DIM])
        p = jnp.exp(s - jnp.max(s, axis=-1, keepdims=True))
        p = p / jnp.sum(p, axis=-1, keepdims=True)
        heads.append(_dot(p.astype(BF16), v_ref[:, lo:lo + X_HEAD_DIM]).astype(BF16))
    o = jnp.concatenate(heads, axis=-1)
    x2 = _layer_norm(ALPHA * x + _dot(o, wo_ref[...]), g_ref[...], b_ref[...])
    x2_ref[...] = x2

    x_hi = x2.astype(BF16)
    x_lo = (x2 - x_hi.astype(F32)).astype(BF16)
    logits = _dot(x_hi, rwh_ref[...]) + _dot(x_lo, rwh_ref[...]) + _dot(x_hi, rwl_ref[...]) + rb_ref[...]
    lane = lax.broadcasted_iota(jnp.int32, logits.shape, 1).astype(F32)
    work = logits
    top_v, top_i = [], []
    for _ in range(TOP_K):
        m = jnp.max(work, axis=-1, keepdims=True)
        sel = jnp.min(jnp.where(work == m, lane, float(LANES)), axis=-1, keepdims=True)
        top_v.append(m)
        top_i.append(sel)
        work = jnp.where(lane == sel, -jnp.inf, work)
    e = [jnp.exp(v - top_v[0]) for v in top_v]
    denom = e[0] + e[1] + e[2] + e[3]
    idx_out = jnp.zeros(logits.shape, F32)
    gate_out = jnp.zeros(logits.shape, F32)
    for k in range(TOP_K):
        idx_out = jnp.where(lane == float(k), top_i[k], idx_out)
        gate_out = jnp.where(lane == float(k), e[k] / denom, gate_out)
    idx_ref[...] = idx_out.astype(jnp.int32)
    gate_ref[...] = gate_out

    tm = logits.shape[0]
    onehot = jnp.zeros(logits.shape, F32)
    for k in range(TOP_K):
        onehot = jnp.where(lane == top_i[k], 1.0, onehot)
    r_i = lax.broadcasted_iota(jnp.int32, (tm, tm), 0)
    c_i = lax.broadcasted_iota(jnp.int32, (tm, tm), 1)
    earlier = jnp.where(c_i < r_i, 1.0, 0.0).astype(BF16)
    pos = _dot(earlier, onehot.astype(BF16))
    rank_out = jnp.zeros(logits.shape, F32)
    for k in range(TOP_K):
        rk = jnp.sum(jnp.where(lane == top_i[k], pos, 0.0), axis=-1, keepdims=True)
        rank_out = jnp.where(lane == float(k), rk, rank_out)
    rank_ref[...] = rank_out.astype(jnp.int32)
    total = jnp.sum(onehot, axis=0, keepdims=True)
    cnt_ref[...] = jnp.broadcast_to(total, cnt_ref.shape).astype(jnp.int32)


def _cross_router(x1, kv, w_q, w_o, ln_g, ln_b, router_w, router_b, batch, seq, mem_len, time_major_in=False,
                  tm=256):
    d = D_MODEL
    n = batch * seq
    nt = seq // tm
    pad = LANES - N_EXPERTS
    rw = jnp.pad(router_w, ((0, 0), (0, pad)))
    rw_hi = rw.astype(BF16)
    rw_lo = (rw - rw_hi.astype(F32)).astype(BF16)
    rb = jnp.pad(router_b, (0, pad), constant_values=-jnp.inf).reshape(1, LANES)
    const = lambda b, i: (0, 0)
    row = lambda b, i: (b * nt + i, 0)
    return pl.pallas_call(
        _cross_router_kernel,
        out_shape=(jax.ShapeDtypeStruct((n, d), F32),
                   jax.ShapeDtypeStruct((n, LANES), jnp.int32),
                   jax.ShapeDtypeStruct((n, LANES), F32),
                   jax.ShapeDtypeStruct((n, LANES), jnp.int32),
                   jax.ShapeDtypeStruct((n // tm * SUBLANES, LANES), jnp.int32)),
        grid=(batch, nt),
        in_specs=[pl.BlockSpec((tm, d), (lambda b, i: (i, b)) if time_major_in else row),
                  pl.BlockSpec((d, d), const),
                  pl.BlockSpec((d, d), const),
                  pl.BlockSpec((mem_len, d), lambda b, i: (b, 0)),
                  pl.BlockSpec((mem_len, d), lambda b, i: (b, 1)),
                  pl.BlockSpec((1, d), const),
                  pl.BlockSpec((1, d), const),
                  pl.BlockSpec((d, LANES), const),
                  pl.BlockSpec((d, LANES), const),
                  pl.BlockSpec((1, LANES), const)],
        out_specs=(pl.BlockSpec((tm, d), row),
                   pl.BlockSpec((tm, LANES), row),
                   pl.BlockSpec((tm, LANES), row),
                   pl.BlockSpec((tm, LANES), row),
                   pl.BlockSpec((SUBLANES, LANES), row)),
        compiler_params=_params("parallel", "parallel"),
        name="cross_attn_router",
    )(x1, w_q.astype(BF16), w_o.astype(BF16), kv, kv, ln_g.reshape(1, d), ln_b.reshape(1, d), rw_hi, rw_lo, rb)


ROW_TILE = D_MODEL // LANES
SEG_ROWS = 32


def _to_row_tiles(dst_ref, x, rows, first_row=0):
    for c in range(ROW_TILE):
        dst_ref[pl.ds(first_row * ROW_TILE + c, rows, stride=ROW_TILE), :] = x[:, c * LANES:(c + 1) * LANES]


def _from_row_tiles(src_ref, rows, first_row=0):
    return jnp.concatenate([src_ref[pl.ds(first_row * ROW_TILE + c, rows, stride=ROW_TILE), :]
                            for c in range(ROW_TILE)], axis=-1)


def _tile_rows(row):
    return pl.multiple_of(row * ROW_TILE, ROW_TILE)


def _num_segments(count):
    return lax.shift_right_logical(count + (SEG_ROWS - 1), SEG_ROWS.bit_length() - 1)


def _tile_segments(tcnt_ref):
    return lax.fori_loop(0, N_EXPERTS, lambda e, t: t + _num_segments(tcnt_ref[0, 0, e]), 0)


def _dispatch_kernel(pstart_ref, counts_ref, padded_ref, ldest_ref, tcnt_ref, tbase_ref, lstart_ref, x_ref,
                     xs_hbm, xrt, stage, zeros, nseg, sems, zsem, rsem, *, tm, nt):
    i = pl.program_id(0)
    slot = lax.rem(i, 2)

    @pl.when(i == 0)
    def _():
        stage[...] = jnp.zeros_like(stage)
        zeros[...] = jnp.zeros_like(zeros)

    _to_row_tiles(xrt, x_ref[...], tm)
    st = stage.at[slot]
    for k in range(TOP_K):
        def move(r, c, k=k):
            dst = _tile_rows(ldest_ref[0, 0, k * tm + r])
            st[pl.ds(dst, ROW_TILE), :] = xrt[pl.ds(_tile_rows(r), ROW_TILE), :]
            return c

        lax.fori_loop(0, tm, move, 0, unroll=8)

    def seg_copy(s, src_row, dst_row):
        return pltpu.make_async_copy(stage.at[s, pl.ds(_tile_rows(src_row), SEG_ROWS * ROW_TILE), :],
                                     xs_hbm.at[pl.ds(_tile_rows(dst_row), SEG_ROWS * ROW_TILE), :], sems.at[s])

    def wait_segments(s, count):
        def wait(j, c):
            seg_copy(s, 0, 0).wait()
            return c

        lax.fori_loop(0, count, wait, 0)

    @pl.when(i > 0)
    def _():
        wait_segments(1 - slot, nseg[1 - slot])

    def issue(e, total):
        n = _num_segments(tcnt_ref[0, 0, e])
        src0 = lstart_ref[0, 0, e]
        dst0 = pstart_ref[e] + tbase_ref[0, 0, e]

        def one(j, c):
            seg_copy(slot, src0 + j * SEG_ROWS, dst0 + j * SEG_ROWS).start()
            return c

        lax.fori_loop(0, n, one, 0)
        return total + n

    total = lax.fori_loop(0, N_EXPERTS, issue, 0)
    nseg[slot] = total

    @pl.when(i == nt - 1)
    def _():
        wait_segments(slot, total)

        def gap(e):
            lo = pstart_ref[e] + counts_ref[e]
            width = padded_ref[e] - counts_ref[e]
            full = lax.shift_right_logical(width, SEG_ROWS.bit_length() - 1)
            return lo, full, width - full * SEG_ROWS

        def seg_zero(row):
            return pltpu.make_async_copy(zeros, xs_hbm.at[pl.ds(_tile_rows(row), SEG_ROWS * ROW_TILE), :], zsem)

        def row_zero(row):
            return pltpu.make_async_copy(zeros.at[pl.ds(0, ROW_TILE), :],
                                         xs_hbm.at[pl.ds(_tile_rows(row), ROW_TILE), :], rsem)

        def fill(e, c):
            lo, full, rest = gap(e)
            lax.fori_loop(0, full, lambda j, c2: (seg_zero(lo + j * SEG_ROWS).start(), c2)[1], 0)
            lax.fori_loop(0, rest, lambda j, c2: (row_zero(lo + full * SEG_ROWS + j).start(), c2)[1], 0)
            return c

        def fill_wait(e, c):
            _, full, rest = gap(e)
            lax.fori_loop(0, full, lambda j, c2: (seg_zero(0).wait(), c2)[1], 0)
            lax.fori_loop(0, rest, lambda j, c2: (row_zero(0).wait(), c2)[1], 0)
            return c

        lax.fori_loop(0, N_EXPERTS, fill, 0)
        lax.fori_loop(0, N_EXPERTS, fill_wait, 0)


def _stage_rows(tm):
    return TOP_K * tm + N_EXPERTS * SEG_ROWS


def _dispatch(x2, pstart, counts, padded, ldest_t, tcnt_t, tbase_t, lstart_t, n_pad, tm):
    n, d = x2.shape
    nt = n // tm
    smem_tile = lambda width: pl.BlockSpec((1, 1, width), lambda i, *_: (i, 0, 0), memory_space=pltpu.SMEM)
    grid_spec = pltpu.PrefetchScalarGridSpec(
        num_scalar_prefetch=3,
        grid=(nt,),
        in_specs=[smem_tile(TOP_K * tm), smem_tile(LANES), smem_tile(LANES), smem_tile(LANES),
                  pl.BlockSpec((tm, d), lambda i, *_: (i, 0))],
        out_specs=pl.BlockSpec(memory_space=pl.ANY),
        scratch_shapes=[pltpu.VMEM((tm * ROW_TILE, LANES), F32),
                        pltpu.VMEM((2, _stage_rows(tm) * ROW_TILE, LANES), F32),
                        pltpu.VMEM((SEG_ROWS * ROW_TILE, LANES), F32),
                        pltpu.SMEM((2,), jnp.int32),
                        pltpu.SemaphoreType.DMA((2,)),
                        pltpu.SemaphoreType.DMA,
                        pltpu.SemaphoreType.DMA],
    )
    return pl.pallas_call(
        functools.partial(_dispatch_kernel, tm=tm, nt=nt),
        out_shape=jax.ShapeDtypeStruct((n_pad * ROW_TILE, LANES), F32),
        grid_spec=grid_spec,
        compiler_params=_params("arbitrary"),
        name="moe_dispatch",
    )(pstart, counts, padded, ldest_t, tcnt_t, tbase_t, lstart_t, x2)


def _expert_kernel(be_ref, valid_ref, xs_ref, wup_ref, bup_ref, wdn_ref, bdn_ref, y_ref, wup_bf, wdn_bf):
    i = pl.program_id(0)

    @pl.when(valid_ref[i] == 0)
    def _():
        y_ref[...] = jnp.zeros_like(y_ref)

    @pl.when(valid_ref[i] > 0)
    def _():
        prev = be_ref[jnp.maximum(i - 1, 0)]

        @pl.when(jnp.logical_or(i == 0, be_ref[i] != prev))
        def _():
            wup_bf[...] = wup_ref[0].astype(BF16)
            wdn_bf[...] = wdn_ref[0].astype(BF16)

        def expert_rows(first_row):
            xb = _from_row_tiles(xs_ref, MOE_HALF, first_row).astype(BF16)
            h = _dot(xb, wup_bf[...]) + bup_ref[0]
            glu = jnp.minimum(h[:, :D_FF], SWIGLU_LIMIT)
            lin = jnp.clip(h[:, D_FF:], -SWIGLU_LIMIT, SWIGLU_LIMIT)
            act = glu * _sigmoid(SWIGLU_ALPHA * glu) * (lin + 1.0)
            _to_row_tiles(y_ref, _dot(act.astype(BF16), wdn_bf[...]) + bdn_ref[0], MOE_HALF, first_row)

        @pl.when(valid_ref[i] > MOE_HALF)
        def _():
            expert_rows(0)
            expert_rows(MOE_HALF)

        @pl.when(valid_ref[i] <= MOE_HALF)
        def _():
            expert_rows(0)
            y_ref[MOE_HALF * ROW_TILE:, :] = jnp.zeros((MOE_HALF * ROW_TILE, LANES), F32)


def _experts(xs, blk_expert, blk_valid, w_up, b_up, w_down, b_down):
    n_blocks = blk_expert.shape[0]
    d = w_up.shape[1]
    f2 = w_up.shape[2]
    blk_rows = MOE_BLOCK * ROW_TILE
    grid_spec = pltpu.PrefetchScalarGridSpec(
        num_scalar_prefetch=2,
        grid=(n_blocks,),
        in_specs=[pl.BlockSpec((blk_rows, LANES), lambda i, be, va: (jnp.where(va[i] > 0, i, 0), 0)),
                  pl.BlockSpec((1, d, f2), lambda i, be, va: (be[i], 0, 0)),
                  pl.BlockSpec((1, 1, f2), lambda i, be, va: (be[i], 0, 0)),
                  pl.BlockSpec((1, D_FF, d), lambda i, be, va: (be[i], 0, 0)),
                  pl.BlockSpec((1, 1, d), lambda i, be, va: (be[i], 0, 0))],
        out_specs=pl.BlockSpec((blk_rows, LANES), lambda i, be, va: (i, 0)),
        scratch_shapes=[pltpu.VMEM((d, f2), BF16),
                        pltpu.VMEM((D_FF, d), BF16)],
    )
    return pl.pallas_call(
        _expert_kernel,
        out_shape=jax.ShapeDtypeStruct((n_blocks * blk_rows, LANES), F32),
        grid_spec=grid_spec,
        compiler_params=_params("arbitrary"),
        name="moe_experts",
    )(blk_expert, blk_valid, xs, w_up, b_up.reshape(N_EXPERTS, 1, f2), w_down, b_down.reshape(N_EXPERTS, 1, d))


def _combine_ln_kernel(pstart_ref, ldest_ref, gate_ref, tcnt_ref, tbase_ref, lstart_ref,
                       tcnt_next_ref, tbase_next_ref, lstart_next_ref, ys_hbm, x_ref, g_ref, b_ref,
                       o_ref, stage, yrt, sems, *, tm, nt):
    i = pl.program_id(0)
    slot = lax.rem(i, 2)

    def seg_copy(s, src_row, dst_row):
        return pltpu.make_async_copy(ys_hbm.at[pl.ds(_tile_rows(src_row), SEG_ROWS * ROW_TILE), :],
                                     stage.at[s, pl.ds(_tile_rows(dst_row), SEG_ROWS * ROW_TILE), :], sems.at[s])

    def fetch(s, tcnt, tbase, lstart):
        def per_expert(e, c):
            src0 = pstart_ref[e] + tbase[0, 0, e]
            dst0 = lstart[0, 0, e]

            def one(j, c2):
                seg_copy(s, src0 + j * SEG_ROWS, dst0 + j * SEG_ROWS).start()
                return c2

            lax.fori_loop(0, _num_segments(tcnt[0, 0, e]), one, 0)
            return c

        lax.fori_loop(0, N_EXPERTS, per_expert, 0)

    @pl.when(i == 0)
    def _():
        fetch(0, tcnt_ref, tbase_ref, lstart_ref)

    @pl.when(i + 1 < nt)
    def _():
        fetch(1 - slot, tcnt_next_ref, tbase_next_ref, lstart_next_ref)

    def wait(j, c):
        seg_copy(slot, 0, 0).wait()
        return c

    lax.fori_loop(0, _tile_segments(tcnt_ref), wait, 0)

    st = stage.at[slot]

    def reduce(r, c):
        acc = None
        for k in range(TOP_K):
            row = st[pl.ds(_tile_rows(ldest_ref[0, 0, k * tm + r]), ROW_TILE), :]
            term = gate_ref[0, 0, k * tm + r] * row
            acc = term if acc is None else acc + term
        yrt[pl.ds(_tile_rows(r), ROW_TILE), :] = acc
        return c

    lax.fori_loop(0, tm, reduce, 0, unroll=4)
    y = _from_row_tiles(yrt, tm)
    o_ref[...] = _layer_norm(ALPHA * x_ref[...] + y, g_ref[...], b_ref[...])


def _combine_ln(ys, pstart, ldest_t, gates_t, tcnt_t, tbase_t, lstart_t, x2, ln_g, ln_b, tm, time_major_seq):
    n, d = x2.shape
    nt = n // tm
    if time_major_seq is None:
        out_shape, out_index = (n, d), lambda i, *_: (i, 0)
    else:
        tiles_per_seq = time_major_seq // tm
        out_shape = (time_major_seq, n // time_major_seq * d)
        out_index = lambda i, *_: (lax.rem(i, tiles_per_seq), i // tiles_per_seq)
    cur = lambda width: pl.BlockSpec((1, 1, width), lambda i, *_: (i, 0, 0), memory_space=pltpu.SMEM)
    nxt = lambda width: pl.BlockSpec((1, 1, width), lambda i, *_: (jnp.minimum(i + 1, nt - 1), 0, 0),
                                     memory_space=pltpu.SMEM)
    grid_spec = pltpu.PrefetchScalarGridSpec(
        num_scalar_prefetch=1,
        grid=(nt,),
        in_specs=[cur(TOP_K * tm), cur(TOP_K * tm), cur(LANES), cur(LANES), cur(LANES),
                  nxt(LANES), nxt(LANES), nxt(LANES),
                  pl.BlockSpec(memory_space=pl.ANY),
                  pl.BlockSpec((tm, d), lambda i, *_: (i, 0)),
                  pl.BlockSpec((1, d), lambda i, *_: (0, 0)),
                  pl.BlockSpec((1, d), lambda i, *_: (0, 0))],
        out_specs=pl.BlockSpec((tm, d), out_index),
        scratch_shapes=[pltpu.VMEM((2, _stage_rows(tm) * ROW_TILE, LANES), F32),
                        pltpu.VMEM((tm * ROW_TILE, LANES), F32),
                        pltpu.SemaphoreType.DMA((2,))],
    )
    return pl.pallas_call(
        functools.partial(_combine_ln_kernel, tm=tm, nt=nt),
        out_shape=jax.ShapeDtypeStruct(out_shape, F32),
        grid_spec=grid_spec,
        compiler_params=_params("arbitrary"),
        name="moe_combine_ln3",
    )(pstart, ldest_t, gates_t, tcnt_t, tbase_t, lstart_t, tcnt_t, tbase_t, lstart_t, ys, x2,
      ln_g.reshape(1, d), ln_b.reshape(1, d))


def _round_up(v, m):
    return (v + m - 1) // m * m


def _moe_layer(x2, idx_pad, lrank_pad, tcnt_pad, gates_pad, w_up, b_up, w_down, b_down, ln_g, ln_b,
               time_major_seq=None, tm=256):
    n = x2.shape[0]
    nt = n // tm
    i32 = jnp.int32
    tcnt = tcnt_pad.reshape(nt, SUBLANES, LANES)[:, 0, :N_EXPERTS]
    tbase = jnp.cumsum(tcnt, axis=0) - tcnt
    counts = jnp.sum(tcnt, axis=0)
    padded = _round_up(counts + SEG_ROWS, MOE_BLOCK)
    pend = jnp.cumsum(padded)
    pstart = pend - padded
    n_blocks = -(-(n * TOP_K + N_EXPERTS * (SEG_ROWS + MOE_BLOCK - 1)) // MOE_BLOCK)
    block_start = jnp.arange(n_blocks, dtype=i32) * MOE_BLOCK
    blk_expert = jnp.minimum(jnp.sum(pend[None, :] <= block_start[:, None], axis=1), N_EXPERTS - 1).astype(i32)
    blk_valid = jnp.clip(counts[blk_expert] - (block_start - pstart[blk_expert]), 0, MOE_BLOCK).astype(i32)
    lseg = _round_up(tcnt, SEG_ROWS)
    lstart = jnp.cumsum(lseg, axis=1) - lseg
    idx = idx_pad[:, :TOP_K].reshape(nt, tm, TOP_K)
    chosen = idx[..., None] == jnp.arange(N_EXPERTS, dtype=i32)
    ldest = jnp.sum(jnp.where(chosen, lstart[:, None, None, :], 0), axis=-1) + lrank_pad[:, :TOP_K].reshape(nt, tm, TOP_K)
    per_tile = lambda a: a.transpose(0, 2, 1).reshape(nt, 1, TOP_K * tm)
    ldest_t = per_tile(ldest.astype(i32))
    gates_t = per_tile(gates_pad[:, :TOP_K].reshape(nt, tm, TOP_K))
    lane_pad = lambda a: jnp.pad(a.astype(i32), ((0, 0), (0, LANES - N_EXPERTS))).reshape(nt, 1, LANES)
    tcnt_t, tbase_t, lstart_t = lane_pad(tcnt), lane_pad(tbase), lane_pad(lstart)
    pstart, counts, padded = pstart.astype(i32), counts.astype(i32), padded.astype(i32)

    xs = _dispatch(x2, pstart, counts, padded, ldest_t, tcnt_t, tbase_t, lstart_t, n_blocks * MOE_BLOCK, tm)
    ys = _experts(xs, blk_expert, blk_valid, w_up, b_up, w_down, b_down)
    return _combine_ln(ys, pstart, ldest_t, gates_t, tcnt_t, tbase_t, lstart_t, x2, ln_g, ln_b, tm, time_major_seq)


def _s5_kernel(x_ref, win_ref, bblk_ref, ar_ref, ai_ref, cblk_ref, dsk_ref, wval_ref, wgate_ref, g_ref, b_ref,
               o_ref, bur, bui, sr, si, *, tt, batch):
    rows = tt * batch

    @pl.when(pl.program_id(0) == 0)
    def _():
        sr[...] = jnp.zeros_like(sr)
        si[...] = jnp.zeros_like(si)

    x = x_ref[...]
    u = _dot(x.astype(BF16), win_ref[...])
    ub = u.astype(BF16)
    for c in range(S5_N_CHUNKS):
        bu = _dot(ub[:, c * S5_CHUNK_IN:(c + 1) * S5_CHUNK_IN], bblk_ref[c])
        bur[:, c * S5_CHUNK_STATE:(c + 1) * S5_CHUNK_STATE] = bu[:, :S5_CHUNK_STATE]
        bui[:, c * S5_CHUNK_STATE:(c + 1) * S5_CHUNK_STATE] = bu[:, S5_CHUNK_STATE:]

    for c in range(S5_N_CHUNKS):
        cols = pl.ds(c * S5_CHUNK_STATE, S5_CHUNK_STATE)
        a_r = ar_ref[:, cols]
        a_i = ai_ref[:, cols]

        def step(t, carry):
            s_r, s_i = carry
            rsl = pl.ds(pl.multiple_of(t * batch, batch), batch)
            n_r = a_r * s_r - a_i * s_i + bur[rsl, cols]
            n_i = a_r * s_i + a_i * s_r + bui[rsl, cols]
            bur[rsl, cols] = n_r
            bui[rsl, cols] = n_i
            return n_r, n_i

        f_r, f_i = lax.fori_loop(0, tt, step, (sr[:, cols], si[:, cols]), unroll=True)
        sr[:, cols] = f_r
        si[:, cols] = f_i

    ys = []
    for c in range(S5_N_CHUNKS):
        cols = pl.ds(c * S5_CHUNK_STATE, S5_CHUNK_STATE)
        xri = jnp.concatenate([bur[:, cols].astype(BF16), bui[:, cols].astype(BF16)], axis=-1)
        ys.append(_dot(xri, cblk_ref[c]))
    y = jnp.concatenate(ys, axis=-1) + dsk_ref[...] * u
    yb = _gelu(y).astype(BF16)
    hmix = _dot(yb, wval_ref[...]) * _sigmoid(_dot(yb, wgate_ref[...]))
    o_ref[...] = _layer_norm(ALPHA * x + hmix, g_ref[...], b_ref[...])


def _s5_discretize(log_dt, lambda_re, lambda_im, b_re, b_im, c_re, c_im):
    dt = jnp.exp(log_dt)[:, None]
    mag = jnp.exp(lambda_re * dt)
    ar = mag * jnp.cos(lambda_im * dt)
    ai = mag * jnp.sin(lambda_im * dt)
    den = lambda_re * lambda_re + lambda_im * lambda_im
    zr = ((ar - 1.0) * lambda_re + ai * lambda_im) / den
    zi = (ai * lambda_re - (ar - 1.0) * lambda_im) / den
    bbar_re = zr[..., None] * b_re - zi[..., None] * b_im
    bbar_im = zr[..., None] * b_im + zi[..., None] * b_re
    ng, gc, p, gw = S5_N_CHUNKS, S5_CHUNK_GROUPS, S5_STATE, S5_GROUP
    eye = jnp.eye(gc, dtype=F32)
    bre = bbar_re.reshape(ng, gc, p, gw).transpose(0, 1, 3, 2)
    bim = bbar_im.reshape(ng, gc, p, gw).transpose(0, 1, 3, 2)
    blk_re = jnp.einsum('cgip,gh->cgihp', bre, eye).reshape(ng, gc * gw, gc * p)
    blk_im = jnp.einsum('cgip,gh->cgihp', bim, eye).reshape(ng, gc * gw, gc * p)
    bblk = jnp.concatenate([blk_re, blk_im], axis=-1).astype(BF16)
    cre = c_re.reshape(ng, gc, gw, p).transpose(0, 1, 3, 2)
    cim = c_im.reshape(ng, gc, gw, p).transpose(0, 1, 3, 2)
    cblk_re = jnp.einsum('cgpi,gh->cgphi', cre, eye).reshape(ng, gc * p, gc * gw)
    cblk_im = jnp.einsum('cgpi,gh->cgphi', cim, eye).reshape(ng, gc * p, gc * gw)
    cblk = jnp.concatenate([cblk_re, -cblk_im], axis=1).astype(BF16)
    return bblk, ar.reshape(1, S5_STATES), ai.reshape(1, S5_STATES), cblk


def _s5_mixer_ln(x_tm, w_in, log_dt, lambda_re, lambda_im, b_re, b_im, c_re, c_im, d_skip, w_val, w_gate,
                 ln_g, ln_b, batch, tt=16):
    n, d = x_tm.shape
    rows = tt * batch
    bblk, ar, ai, cblk = _s5_discretize(log_dt, lambda_re, lambda_im, b_re, b_im, c_re, c_im)
    const2 = lambda i: (0, 0)
    const3 = lambda i: (0, 0, 0)
    row = lambda i: (i, 0)
    return pl.pallas_call(
        functools.partial(_s5_kernel, tt=tt, batch=batch),
        out_shape=jax.ShapeDtypeStruct((n, d), F32),
        grid=(n // rows,),
        in_specs=[pl.BlockSpec((rows, d), row),
                  pl.BlockSpec((d, d), const2),
                  pl.BlockSpec(bblk.shape, const3),
                  pl.BlockSpec(ar.shape, const2),
                  pl.BlockSpec(ai.shape, const2),
                  pl.BlockSpec(cblk.shape, const3),
                  pl.BlockSpec((1, d), const2),
                  pl.BlockSpec((d, d), const2),
                  pl.BlockSpec((d, d), const2),
                  pl.BlockSpec((1, d), const2),
                  pl.BlockSpec((1, d), const2)],
        out_specs=pl.BlockSpec((rows, d), row),
        scratch_shapes=[pltpu.VMEM((rows, S5_STATES), F32),
                        pltpu.VMEM((rows, S5_STATES), F32),
                        pltpu.VMEM((batch, S5_STATES), F32),
                        pltpu.VMEM((batch, S5_STATES), F32)],
        compiler_params=_params("arbitrary"),
        name="s5_mixer_ln1",
    )(x_tm, w_in.astype(BF16), bblk, ar, ai, cblk, d_skip.reshape(1, d), w_val.astype(BF16), w_gate.astype(BF16),
      ln_g.reshape(1, d), ln_b.reshape(1, d))


def _diff_lambda_init(layer_idx):
    return 0.8 - 0.6 * math.exp(-0.3 * layer_idx)


def kernel(x, mem, w_mem_kv, l0_w_in, l0_sgu_ln_g, l0_sgu_ln_b, l0_w_spatial, l0_b_spatial, l0_lam_q1, l0_lam_k1, l0_lam_q2, l0_lam_k2, l0_subln_g, l0_w_out, l0_ln1_g, l0_ln1_b, l0_xq, l0_xo, l0_ln2_g, l0_ln2_b, l0_router_w, l0_router_b, l0_exp_w_up, l0_exp_b_up, l0_exp_w_down, l0_exp_b_down, l0_ln3_g, l0_ln3_b, l1_w_in, l1_log_dt, l1_lambda_re, l1_lambda_im, l1_b_re, l1_b_im, l1_c_re, l1_c_im, l1_d_skip, l1_w_val, l1_w_gate, l1_ln1_g, l1_ln1_b, l1_xq, l1_xo, l1_ln2_g, l1_ln2_b, l1_router_w, l1_router_b, l1_exp_w_up, l1_exp_b_up, l1_exp_w_down, l1_exp_b_down, l1_ln3_g, l1_ln3_b):
    batch, seq, d = x.shape
    mem_len = mem.shape[1]
    n = batch * seq
    x0 = x.reshape(n, d)

    kv = _matmul(mem.reshape(batch * mem_len, d), w_mem_kv.astype(BF16), BF16,
                 tm=min(512, batch * mem_len), tn=d)

    a, qkv = _inproj_sgu(x0, l0_w_in, l0_sgu_ln_g, l0_sgu_ln_b, l0_w_spatial, l0_b_spatial)
    dattn = _diff_attention(qkv, l0_lam_q1, l0_lam_k1, l0_lam_q2, l0_lam_k2, l0_subln_g,
                            _diff_lambda_init(0), batch, seq)
    x1 = _outproj_ln(a, dattn, l0_w_out, x0, l0_ln1_g, l0_ln1_b)
    x2, idx, gates, rank, cnt = _cross_router(x1, kv, l0_xq, l0_xo, l0_ln2_g, l0_ln2_b, l0_router_w, l0_router_b,
                                              batch, seq, mem_len)
    x3_tm = _moe_layer(x2, idx, rank, cnt, gates, l0_exp_w_up, l0_exp_b_up, l0_exp_w_down, l0_exp_b_down,
                       l0_ln3_g, l0_ln3_b, time_major_seq=seq).reshape(n, d)

    x4_tm = _s5_mixer_ln(x3_tm, l1_w_in, l1_log_dt, l1_lambda_re, l1_lambda_im, l1_b_re, l1_b_im,
                         l1_c_re, l1_c_im, l1_d_skip, l1_w_val, l1_w_gate, l1_ln1_g, l1_ln1_b, batch)
    x5, idx, gates, rank, cnt = _cross_router(x4_tm.reshape(seq, batch * d), kv, l1_xq, l1_xo, l1_ln2_g, l1_ln2_b,
                                              l1_router_w, l1_router_b, batch, seq, mem_len, time_major_in=True)
    x6 = _moe_layer(x5, idx, rank, cnt, gates, l1_exp_w_up, l1_exp_b_up, l1_exp_w_down, l1_exp_b_down,
                    l1_ln3_g, l1_ln3_b)
    return x6.reshape(batch, seq, d)
```

```python
import functools
import math

import jax
import jax.numpy as jnp
from jax import lax
from jax.experimental import pallas as pl
from jax.experimental.pallas import tpu as pltpu

F32 = jnp.float32
BF16 = jnp.bfloat16

D_MODEL = 1024
CHUNK = 128
SGU_GROUPS = 4
SGU_WIDTH = 512
DIFF_HEADS = 4
DIFF_HEAD_DIM = 64
DIFF_V_DIM = 128
DIFF_QK_WIDTH = DIFF_HEADS * 2 * DIFF_HEAD_DIM
S5_GROUP = 16
S5_GROUPS = 64
S5_STATE = 64
X_HEADS = 4
X_HEAD_DIM = 256
N_EXPERTS = 32
TOP_K = 4
D_FF = 1024
SWIGLU_LIMIT = 7.0
SWIGLU_ALPHA = 1.702
MOE_BLOCK = 512
MOE_HALF = MOE_BLOCK // 2
DEPTH = 2
ALPHA = (2 * DEPTH) ** 0.25
LN_EPS = 1e-5
NEG_INF = -1e30

LANES = 128
SUBLANES = 8
VMEM_LIMIT_BYTES = 56 * 1024 * 1024

S5_CHUNK_GROUPS = 8
S5_CHUNK_IN = S5_CHUNK_GROUPS * S5_GROUP
S5_CHUNK_STATE = S5_CHUNK_GROUPS * S5_STATE
S5_N_CHUNKS = S5_GROUPS // S5_CHUNK_GROUPS
S5_STATES = S5_GROUPS * S5_STATE


def _params(*sem):
    return pltpu.CompilerParams(dimension_semantics=sem, vmem_limit_bytes=VMEM_LIMIT_BYTES)


def _gelu(x):
    return 0.5 * x * (1.0 + jnp.tanh(math.sqrt(2.0 / math.pi) * (x + 0.044715 * (x * x * x))))


def _sigmoid(x):
    return 1.0 / (1.0 + jnp.exp(-x))


def _layer_norm(z, g, b):
    mu = jnp.mean(z, axis=-1, keepdims=True)
    zc = z - mu
    var = jnp.mean(zc * zc, axis=-1, keepdims=True)
    return zc * lax.rsqrt(var + LN_EPS) * g + b


def _dot(a, b):
    return jnp.dot(a, b, preferred_element_type=F32)


def _dot_nt(a, b):
    return lax.dot_general(a, b, (((1,), (1,)), ((), ())), preferred_element_type=F32)


def _matmul_kernel(x_ref, w_ref, o_ref):
    o_ref[...] = _dot(x_ref[...].astype(BF16), w_ref[...]).astype(o_ref.dtype)


def _matmul(x, w, out_dtype, tm, tn):
    m, k = x.shape
    n = w.shape[1]
    return pl.pallas_call(
        _matmul_kernel,
        out_shape=jax.ShapeDtypeStruct((m, n), out_dtype),
        grid=(n // tn, m // tm),
        in_specs=[pl.BlockSpec((tm, k), lambda j, i: (i, 0)),
                  pl.BlockSpec((k, tn), lambda j, i: (0, j))],
        out_specs=pl.BlockSpec((tm, tn), lambda j, i: (i, j)),
        compiler_params=_params("parallel", "parallel"),
        name="matmul",
    )(x, w)


def _inproj_sgu_kernel(x_ref, w_ref, lng_ref, lnb_ref, wsp_ref, bsp_ref, a_ref, qk_ref, vt_ref, *, tm):
    xb = x_ref[...].astype(BF16)
    qk_ref[...] = _dot(xb, w_ref[:, 2 * SGU_WIDTH:2 * SGU_WIDTH + 2 * DIFF_QK_WIDTH]).astype(BF16)
    vt_ref[...] = _dot(xb, w_ref[:, 2 * SGU_WIDTH + 2 * DIFF_QK_WIDTH:]).T.astype(BF16)
    h = _dot(xb, w_ref[:, :2 * SGU_WIDTH])
    for g in range(SGU_GROUPS):
        lo = g * CHUNK
        u = _gelu(h[:, lo:lo + CHUNK])
        v = _gelu(h[:, SGU_WIDTH + lo:SGU_WIDTH + lo + CHUNK])
        vn = _layer_norm(v, lng_ref[g:g + 1, :], lnb_ref[g:g + 1, :]).astype(BF16)
        w_g = wsp_ref[g]
        b_g = bsp_ref[:, g:g + 1]
        for c in range(tm // CHUNK):
            r = c * CHUNK
            gate = _dot(w_g, vn[r:r + CHUNK, :]) + b_g
            a_ref[r:r + CHUNK, lo:lo + CHUNK] = (u[r:r + CHUNK, :] * gate).astype(BF16)


def _inproj_sgu(x2d, w_in, ln_g, ln_b, w_spatial, b_spatial, tm=256):
    n, d = x2d.shape
    w = w_in.astype(BF16)
    wsp = jnp.tril(w_spatial).astype(BF16)
    bsp_t = b_spatial.T
    v_width = w.shape[1] - 2 * SGU_WIDTH - 2 * DIFF_QK_WIDTH
    const = lambda i: (0, 0)
    return pl.pallas_call(
        functools.partial(_inproj_sgu_kernel, tm=tm),
        out_shape=(jax.ShapeDtypeStruct((n, SGU_WIDTH), BF16),
                   jax.ShapeDtypeStruct((n, 2 * DIFF_QK_WIDTH), BF16),
                   jax.ShapeDtypeStruct((v_width, n), BF16)),
        grid=(n // tm,),
        in_specs=[pl.BlockSpec((tm, d), lambda i: (i, 0)),
                  pl.BlockSpec(w.shape, const),
                  pl.BlockSpec((SGU_GROUPS, CHUNK), const),
                  pl.BlockSpec((SGU_GROUPS, CHUNK), const),
                  pl.BlockSpec(wsp.shape, lambda i: (0, 0, 0)),
                  pl.BlockSpec(bsp_t.shape, const)],
        out_specs=(pl.BlockSpec((tm, SGU_WIDTH), lambda i: (i, 0)),
                   pl.BlockSpec((tm, 2 * DIFF_QK_WIDTH), lambda i: (i, 0)),
                   pl.BlockSpec((v_width, tm), lambda i: (0, i))),
        compiler_params=_params("parallel"),
        name="inproj_sgu",
    )(x2d, w, ln_g.reshape(SGU_GROUPS, CHUNK), ln_b.reshape(SGU_GROUPS, CHUNK), wsp, bsp_t)


def _diff_attn_kernel(q_ref, k_ref, vt_ref, lam_ref, g_ref, o_ref, *, tq, lam_init):
    i = pl.program_id(1)
    hw = 2 * DIFF_HEAD_DIM
    lane = lax.broadcasted_iota(jnp.int32, (1, hw), 1)
    queries = []
    for h in range(DIFF_HEADS):
        q = q_ref[:, h * hw:(h + 1) * hw] * jnp.asarray(DIFF_HEAD_DIM ** -0.5, BF16)
        zero = jnp.zeros_like(q)
        queries.append((h, jnp.where(lane < DIFF_HEAD_DIM, q, zero)))
        queries.append((h, jnp.where(lane >= DIFF_HEAD_DIM, q, zero)))

    def block(j, carry, masked):
        off = pl.multiple_of(j * tq, tq)
        scores = [_dot_nt(k_ref[pl.ds(off, tq), h * hw:(h + 1) * hw], qc) for h, qc in queries]
        if masked:
            key = lax.broadcasted_iota(jnp.int32, (tq, tq), 0)
            qry = lax.broadcasted_iota(jnp.int32, (tq, tq), 1)
            scores = [jnp.where(key <= qry, st, NEG_INF) for st in scores]
        stats, probs = [], []
        for (m, l, _), st in zip(carry, scores):
            m_new = jnp.maximum(m, jnp.max(st, axis=0, keepdims=True))
            alpha = jnp.exp(m - m_new)
            p = jnp.exp(st - m_new)
            stats.append((m_new, alpha * l + jnp.sum(p, axis=0, keepdims=True), alpha))
            probs.append(p.astype(BF16))
        pv = [_dot(vt_ref[h * DIFF_V_DIM:(h + 1) * DIFF_V_DIM, pl.ds(off, tq)], p)
              for (h, _), p in zip(queries, probs)]
        return tuple((m_new, l_new, alpha * acc + o)
                     for (m_new, l_new, alpha), (_, _, acc), o in zip(stats, carry, pv))

    init = (jnp.full((1, tq), NEG_INF, F32), jnp.zeros((1, tq), F32), jnp.zeros((DIFF_V_DIM, tq), F32))
    carry = lax.fori_loop(0, i, lambda j, c: block(j, c, False), (init,) * len(queries))
    final = block(i, carry, True)

    lam_v = lam_ref[...]
    s_a = jnp.sum(lam_v[0:1, :] * lam_v[1:2, :], axis=-1, keepdims=True)
    s_b = jnp.sum(lam_v[2:3, :] * lam_v[3:4, :], axis=-1, keepdims=True)
    lam = jnp.exp(s_a) - jnp.exp(s_b) + lam_init
    for h in range(DIFF_HEADS):
        (_, l1, a1), (_, l2, a2) = final[2 * h], final[2 * h + 1]
        o_t = a1 / l1 - lam * (a2 / l2)
        o_t = o_t * lax.rsqrt(jnp.mean(o_t * o_t, axis=0, keepdims=True) + LN_EPS) * g_ref[...] * (1.0 - lam_init)
        o_ref[:, h * DIFF_V_DIM:(h + 1) * DIFF_V_DIM] = o_t.T.astype(BF16)


def _diff_attention(qk, vt, lam_q1, lam_k1, lam_q2, lam_k2, subln_g, lam_init, batch, seq, tq=256):
    n = qk.shape[0]
    nq = seq // tq
    lam_v = jnp.stack([lam_q1, lam_k1, lam_q2, lam_k2])
    v_width = DIFF_HEADS * DIFF_V_DIM
    return pl.pallas_call(
        functools.partial(_diff_attn_kernel, tq=tq, lam_init=lam_init),
        out_shape=jax.ShapeDtypeStruct((n, v_width), BF16),
        grid=(batch, nq),
        in_specs=[pl.BlockSpec((tq, DIFF_QK_WIDTH), lambda b, i: (b * nq + i, 0)),
                  pl.BlockSpec((seq, DIFF_QK_WIDTH), lambda b, i: (b, 1)),
                  pl.BlockSpec((v_width, seq), lambda b, i: (0, b)),
                  pl.BlockSpec(lam_v.shape, lambda b, i: (0, 0)),
                  pl.BlockSpec((DIFF_V_DIM, 1), lambda b, i: (0, 0))],
        out_specs=pl.BlockSpec((tq, v_width), lambda b, i: (b * nq + i, 0)),
        compiler_params=_params("parallel", "parallel"),
        name="diff_attention",
    )(qk, qk, vt, lam_v, subln_g.reshape(DIFF_V_DIM, 1))


def _outproj_ln_kernel(a_ref, d_ref, w_ref, x_ref, g_ref, b_ref, o_ref):
    y = _dot(a_ref[...], w_ref[:SGU_WIDTH, :]) + _dot(d_ref[...], w_ref[SGU_WIDTH:, :])
    o_ref[...] = _layer_norm(ALPHA * x_ref[...] + y, g_ref[...], b_ref[...])


def _outproj_ln(a, dattn, w_out, x2d, ln_g, ln_b, tm=512):
    n, d = x2d.shape
    w = w_out.astype(BF16)
    const = lambda i: (0, 0)
    row = lambda i: (i, 0)
    return pl.pallas_call(
        _outproj_ln_kernel,
        out_shape=jax.ShapeDtypeStruct((n, d), F32),
        grid=(n // tm,),
        in_specs=[pl.BlockSpec((tm, a.shape[1]), row),
                  pl.BlockSpec((tm, dattn.shape[1]), row),
                  pl.BlockSpec(w.shape, const),
                  pl.BlockSpec((tm, d), row),
                  pl.BlockSpec((1, d), const),
                  pl.BlockSpec((1, d), const)],
        out_specs=pl.BlockSpec((tm, d), row),
        compiler_params=_params("parallel"),
        name="outproj_ln1",
    )(a, dattn, w, x2d, ln_g.reshape(1, d), ln_b.reshape(1, d))


def _cross_router_kernel(x_ref, wq_ref, wo_ref, k_ref, v_ref, g_ref, b_ref, rwh_ref, rwl_ref, rb_ref,
                         x2_ref, idx_ref, gate_ref, rank_ref, cnt_ref, *, tile, tiles_per_step):
    for t in range(tiles_per_step):
        _cross_router_tile(x_ref, wq_ref, wo_ref, k_ref, v_ref, g_ref, b_ref, rwh_ref, rwl_ref, rb_ref,
                           x2_ref, idx_ref, gate_ref, rank_ref, cnt_ref, slice(t * tile, (t + 1) * tile),
                           slice(t * SUBLANES, (t + 1) * SUBLANES))


def _cross_router_tile(x_ref, wq_ref, wo_ref, k_ref, v_ref, g_ref, b_ref, rwh_ref, rwl_ref, rb_ref,
                       x2_ref, idx_ref, gate_ref, rank_ref, cnt_ref, rows, cnt_rows):
    x = x_ref[rows, :]
    q = (_dot(x.astype(BF16), wq_ref[...]) * (X_HEAD_DIM ** -0.5)).astype(BF16)
    heads = []
    for h in range(X_HEADS):
        lo = h * X_HEAD_DIM
        s = _dot_nt(q[:, lo:lo + X_HEAD_DIM], k_ref[:, lo:lo + X_HEAD_DIM])
        p = jnp.exp(s - jnp.max(s, axis=-1, keepdims=True))
        p = p / jnp.sum(p, axis=-1, keepdims=True)
        heads.append(_dot(p.astype(BF16), v_ref[:, lo:lo + X_HEAD_DIM]).astype(BF16))
    o = jnp.concatenate(heads, axis=-1)
    x2 = _layer_norm(ALPHA * x + _dot(o, wo_ref[...]), g_ref[...], b_ref[...])
    x2_ref[rows, :] = x2

    x_hi = x2.astype(BF16)
    x_lo = (x2 - x_hi.astype(F32)).astype(BF16)
    logits = _dot(x_hi, rwh_ref[...]) + _dot(x_lo, rwh_ref[...]) + _dot(x_hi, rwl_ref[...]) + rb_ref[...]
    lane = lax.broadcasted_iota(jnp.int32, logits.shape, 1).astype(F32)
    work = logits
    top_v, top_i = [], []
    for _ in range(TOP_K):
        m = jnp.max(work, axis=-1, keepdims=True)
        sel = jnp.min(jnp.where(work == m, lane, float(LANES)), axis=-1, keepdims=True)
        top_v.append(m)
        top_i.append(sel)
        work = jnp.where(lane == sel, -jnp.inf, work)
    e = [jnp.exp(v - top_v[0]) for v in top_v]
    denom = e[0] + e[1] + e[2] + e[3]
    idx_out = jnp.zeros(logits.shape, F32)
    gate_out = jnp.zeros(logits.shape, F32)
    for k in range(TOP_K):
        idx_out = jnp.where(lane == float(k), top_i[k], idx_out)
        gate_out = jnp.where(lane == float(k), e[k] / denom, gate_out)
    idx_ref[rows, :] = idx_out.astype(jnp.int32)
    gate_ref[rows, :] = gate_out

    tm = logits.shape[0]
    onehot = jnp.zeros(logits.shape, F32)
    for k in range(TOP_K):
        onehot = jnp.where(lane == top_i[k], 1.0, onehot)
    r_i = lax.broadcasted_iota(jnp.int32, (tm, tm), 0)
    c_i = lax.broadcasted_iota(jnp.int32, (tm, tm), 1)
    earlier = jnp.where(c_i < r_i, 1.0, 0.0).astype(BF16)
    pos = _dot(earlier, onehot.astype(BF16))
    rank_out = jnp.zeros(logits.shape, F32)
    for k in range(TOP_K):
        rk = jnp.sum(jnp.where(lane == top_i[k], pos, 0.0), axis=-1, keepdims=True)
        rank_out = jnp.where(lane == float(k), rk, rank_out)
    rank_ref[rows, :] = rank_out.astype(jnp.int32)
    total = jnp.sum(onehot, axis=0, keepdims=True)
    cnt_ref[cnt_rows, :] = jnp.broadcast_to(total, (SUBLANES, LANES)).astype(jnp.int32)


def _cross_router(x1, kv, w_q, w_o, ln_g, ln_b, router_w, router_b, batch, seq, mem_len, tile=256, tiles_per_step=1):
    n, d = x1.shape
    tm = tile * tiles_per_step
    nt = seq // tm
    pad = LANES - N_EXPERTS
    rw = jnp.pad(router_w, ((0, 0), (0, pad)))
    rw_hi = rw.astype(BF16)
    rw_lo = (rw - rw_hi.astype(F32)).astype(BF16)
    rb = jnp.pad(router_b, (0, pad), constant_values=-jnp.inf).reshape(1, LANES)
    const = lambda b, i: (0, 0)
    row = lambda b, i: (b * nt + i, 0)
    return pl.pallas_call(
        functools.partial(_cross_router_kernel, tile=tile, tiles_per_step=tiles_per_step),
        out_shape=(jax.ShapeDtypeStruct((n, d), F32),
                   jax.ShapeDtypeStruct((n, LANES), jnp.int32),
                   jax.ShapeDtypeStruct((n, LANES), F32),
                   jax.ShapeDtypeStruct((n, LANES), jnp.int32),
                   jax.ShapeDtypeStruct((n // tile * SUBLANES, LANES), jnp.int32)),
        grid=(batch, nt),
        in_specs=[pl.BlockSpec((tm, d), row),
                  pl.BlockSpec((d, d), const),
                  pl.BlockSpec((d, d), const),
                  pl.BlockSpec((mem_len, d), lambda b, i: (b, 0)),
                  pl.BlockSpec((mem_len, d), lambda b, i: (b, 1)),
                  pl.BlockSpec((1, d), const),
                  pl.BlockSpec((1, d), const),
                  pl.BlockSpec((d, LANES), const),
                  pl.BlockSpec((d, LANES), const),
                  pl.BlockSpec((1, LANES), const)],
        out_specs=(pl.BlockSpec((tm, d), row),
                   pl.BlockSpec((tm, LANES), row),
                   pl.BlockSpec((tm, LANES), row),
                   pl.BlockSpec((tm, LANES), row),
                   pl.BlockSpec((tiles_per_step * SUBLANES, LANES), row)),
        compiler_params=_params("parallel", "parallel"),
        name="cross_attn_router",
    )(x1, w_q.astype(BF16), w_o.astype(BF16), kv, kv, ln_g.reshape(1, d), ln_b.reshape(1, d), rw_hi, rw_lo, rb)


ROW_TILE = D_MODEL // LANES
SEG_ROWS = 32


def _to_row_tiles(dst_ref, x, rows, first_row=0):
    for c in range(ROW_TILE):
        dst_ref[pl.ds(first_row * ROW_TILE + c, rows, stride=ROW_TILE), :] = x[:, c * LANES:(c + 1) * LANES]


def _from_row_tiles(src_ref, rows, first_row=0):
    return jnp.concatenate([src_ref[pl.ds(first_row * ROW_TILE + c, rows, stride=ROW_TILE), :]
                            for c in range(ROW_TILE)], axis=-1)


def _tile_rows(row):
    return pl.multiple_of(row * ROW_TILE, ROW_TILE)


def _num_segments(count):
    return lax.shift_right_logical(count + (SEG_ROWS - 1), SEG_ROWS.bit_length() - 1)


def _tile_segments(tcnt_ref):
    return lax.fori_loop(0, N_EXPERTS, lambda e, t: t + _num_segments(tcnt_ref[0, 0, e]), 0)


def _dispatch_kernel(pstart_ref, counts_ref, padded_ref, ldest_ref, tcnt_ref, tbase_ref, lstart_ref, x_ref,
                     xs_hbm, xrt, stage, zeros, nseg, sems, zsem, rsem, *, tm, nt):
    i = pl.program_id(0)
    slot = lax.rem(i, 2)

    @pl.when(i == 0)
    def _():
        stage[...] = jnp.zeros_like(stage)
        zeros[...] = jnp.zeros_like(zeros)

    _to_row_tiles(xrt, x_ref[...], tm)
    st = stage.at[slot]
    for k in range(TOP_K):
        def move(r, c, k=k):
            dst = _tile_rows(ldest_ref[0, 0, k * tm + r])
            st[pl.ds(dst, ROW_TILE), :] = xrt[pl.ds(_tile_rows(r), ROW_TILE), :]
            return c

        lax.fori_loop(0, tm, move, 0, unroll=8)

    def seg_copy(s, src_row, dst_row):
        return pltpu.make_async_copy(stage.at[s, pl.ds(_tile_rows(src_row), SEG_ROWS * ROW_TILE), :],
                                     xs_hbm.at[pl.ds(_tile_rows(dst_row), SEG_ROWS * ROW_TILE), :], sems.at[s])

    def wait_segments(s, count):
        def wait(j, c):
            seg_copy(s, 0, 0).wait()
            return c

        lax.fori_loop(0, count, wait, 0)

    @pl.when(i > 0)
    def _():
        wait_segments(1 - slot, nseg[1 - slot])

    def issue(e, total):
        n = _num_segments(tcnt_ref[0, 0, e])
        src0 = lstart_ref[0, 0, e]
        dst0 = pstart_ref[e] + tbase_ref[0, 0, e]

        def one(j, c):
            seg_copy(slot, src0 + j * SEG_ROWS, dst0 + j * SEG_ROWS).start()
            return c

        lax.fori_loop(0, n, one, 0)
        return total + n

    total = lax.fori_loop(0, N_EXPERTS, issue, 0)
    nseg[slot] = total

    @pl.when(i == nt - 1)
    def _():
        wait_segments(slot, total)

        def gap(e):
            lo = pstart_ref[e] + counts_ref[e]
            width = padded_ref[e] - counts_ref[e]
            full = lax.shift_right_logical(width, SEG_ROWS.bit_length() - 1)
            return lo, full, width - full * SEG_ROWS

        def seg_zero(row):
            return pltpu.make_async_copy(zeros, xs_hbm.at[pl.ds(_tile_rows(row), SEG_ROWS * ROW_TILE), :], zsem)

        def row_zero(row):
            return pltpu.make_async_copy(zeros.at[pl.ds(0, ROW_TILE), :],
                                         xs_hbm.at[pl.ds(_tile_rows(row), ROW_TILE), :], rsem)

        def fill(e, c):
            lo, full, rest = gap(e)
            lax.fori_loop(0, full, lambda j, c2: (seg_zero(lo + j * SEG_ROWS).start(), c2)[1], 0)
            lax.fori_loop(0, rest, lambda j, c2: (row_zero(lo + full * SEG_ROWS + j).start(), c2)[1], 0)
            return c

        def fill_wait(e, c):
            _, full, rest = gap(e)
            lax.fori_loop(0, full, lambda j, c2: (seg_zero(0).wait(), c2)[1], 0)
            lax.fori_loop(0, rest, lambda j, c2: (row_zero(0).wait(), c2)[1], 0)
            return c

        lax.fori_loop(0, N_EXPERTS, fill, 0)
        lax.fori_loop(0, N_EXPERTS, fill_wait, 0)


def _stage_rows(tm):
    return TOP_K * tm + N_EXPERTS * SEG_ROWS


def _dispatch(x2, pstart, counts, padded, ldest_t, tcnt_t, tbase_t, lstart_t, n_pad, tm):
    n, d = x2.shape
    nt = n // tm
    smem_tile = lambda width: pl.BlockSpec((1, 1, width), lambda i, *_: (i, 0, 0), memory_space=pltpu.SMEM)
    grid_spec = pltpu.PrefetchScalarGridSpec(
        num_scalar_prefetch=3,
        grid=(nt,),
        in_specs=[smem_tile(TOP_K * tm), smem_tile(LANES), smem_tile(LANES), smem_tile(LANES),
                  pl.BlockSpec((tm, d), lambda i, *_: (i, 0))],
        out_specs=pl.BlockSpec(memory_space=pl.ANY),
        scratch_shapes=[pltpu.VMEM((tm * ROW_TILE, LANES), F32),
                        pltpu.VMEM((2, _stage_rows(tm) * ROW_TILE, LANES), F32),
                        pltpu.VMEM((SEG_ROWS * ROW_TILE, LANES), F32),
                        pltpu.SMEM((2,), jnp.int32),
                        pltpu.SemaphoreType.DMA((2,)),
                        pltpu.SemaphoreType.DMA,
                        pltpu.SemaphoreType.DMA],
    )
    return pl.pallas_call(
        functools.partial(_dispatch_kernel, tm=tm, nt=nt),
        out_shape=jax.ShapeDtypeStruct((n_pad * ROW_TILE, LANES), F32),
        grid_spec=grid_spec,
        compiler_params=_params("arbitrary"),
        name="moe_dispatch",
    )(pstart, counts, padded, ldest_t, tcnt_t, tbase_t, lstart_t, x2)


def _expert_kernel(be_ref, valid_ref, xs_ref, wup_ref, bup_ref, wdn_ref, bdn_ref, y_ref, wup_bf, wdn_bf):
    i = pl.program_id(0)

    @pl.when(valid_ref[i] == 0)
    def _():
        y_ref[...] = jnp.zeros_like(y_ref)

    @pl.when(valid_ref[i] > 0)
    def _():
        prev = be_ref[jnp.maximum(i - 1, 0)]

        @pl.when(jnp.logical_or(i == 0, be_ref[i] != prev))
        def _():
            wup_bf[...] = wup_ref[0].astype(BF16)
            wdn_bf[...] = wdn_ref[0].astype(BF16)

        def expert_rows(first_row):
            xb = _from_row_tiles(xs_ref, MOE_HALF, first_row).astype(BF16)
            h = _dot(xb, wup_bf[...]) + bup_ref[0]
            glu = jnp.minimum(h[:, :D_FF], SWIGLU_LIMIT)
            lin = jnp.clip(h[:, D_FF:], -SWIGLU_LIMIT, SWIGLU_LIMIT)
            act = glu * _sigmoid(SWIGLU_ALPHA * glu) * (lin + 1.0)
            _to_row_tiles(y_ref, _dot(act.astype(BF16), wdn_bf[...]) + bdn_ref[0], MOE_HALF, first_row)

        @pl.when(valid_ref[i] > MOE_HALF)
        def _():
            expert_rows(0)
            expert_rows(MOE_HALF)

        @pl.when(valid_ref[i] <= MOE_HALF)
        def _():
            expert_rows(0)
            y_ref[MOE_HALF * ROW_TILE:, :] = jnp.zeros((MOE_HALF * ROW_TILE, LANES), F32)


def _experts(xs, blk_expert, blk_valid, w_up, b_up, w_down, b_down):
    n_blocks = blk_expert.shape[0]
    d = w_up.shape[1]
    f2 = w_up.shape[2]
    blk_rows = MOE_BLOCK * ROW_TILE
    grid_spec = pltpu.PrefetchScalarGridSpec(
        num_scalar_prefetch=2,
        grid=(n_blocks,),
        in_specs=[pl.BlockSpec((blk_rows, LANES), lambda i, be, va: (jnp.where(va[i] > 0, i, 0), 0)),
                  pl.BlockSpec((1, d, f2), lambda i, be, va: (be[i], 0, 0)),
                  pl.BlockSpec((1, 1, f2), lambda i, be, va: (be[i], 0, 0)),
                  pl.BlockSpec((1, D_FF, d), lambda i, be, va: (be[i], 0, 0)),
                  pl.BlockSpec((1, 1, d), lambda i, be, va: (be[i], 0, 0))],
        out_specs=pl.BlockSpec((blk_rows, LANES), lambda i, be, va: (i, 0)),
        scratch_shapes=[pltpu.VMEM((d, f2), BF16),
                        pltpu.VMEM((D_FF, d), BF16)],
    )
    return pl.pallas_call(
        _expert_kernel,
        out_shape=jax.ShapeDtypeStruct((n_blocks * blk_rows, LANES), F32),
        grid_spec=grid_spec,
        compiler_params=_params("arbitrary"),
        name="moe_experts",
    )(blk_expert, blk_valid, xs, w_up, b_up.reshape(N_EXPERTS, 1, f2), w_down, b_down.reshape(N_EXPERTS, 1, d))


def _combine_ln_kernel(pstart_ref, ldest_ref, gate_ref, tcnt_ref, tbase_ref, lstart_ref,
                       tcnt_next_ref, tbase_next_ref, lstart_next_ref, ys_hbm, x_ref, g_ref, b_ref,
                       o_ref, stage, yrt, sems, *, tm, nt):
    i = pl.program_id(0)
    slot = lax.rem(i, 2)

    def seg_copy(s, src_row, dst_row):
        return pltpu.make_async_copy(ys_hbm.at[pl.ds(_tile_rows(src_row), SEG_ROWS * ROW_TILE), :],
                                     stage.at[s, pl.ds(_tile_rows(dst_row), SEG_ROWS * ROW_TILE), :], sems.at[s])

    def fetch(s, tcnt, tbase, lstart):
        def per_expert(e, c):
            src0 = pstart_ref[e] + tbase[0, 0, e]
            dst0 = lstart[0, 0, e]

            def one(j, c2):
                seg_copy(s, src0 + j * SEG_ROWS, dst0 + j * SEG_ROWS).start()
                return c2

            lax.fori_loop(0, _num_segments(tcnt[0, 0, e]), one, 0)
            return c

        lax.fori_loop(0, N_EXPERTS, per_expert, 0)

    @pl.when(i == 0)
    def _():
        fetch(0, tcnt_ref, tbase_ref, lstart_ref)

    @pl.when(i + 1 < nt)
    def _():
        fetch(1 - slot, tcnt_next_ref, tbase_next_ref, lstart_next_ref)

    def wait(j, c):
        seg_copy(slot, 0, 0).wait()
        return c

    lax.fori_loop(0, _tile_segments(tcnt_ref), wait, 0)

    st = stage.at[slot]

    def reduce(r, c):
        acc = None
        for k in range(TOP_K):
            row = st[pl.ds(_tile_rows(ldest_ref[0, 0, k * tm + r]), ROW_TILE), :]
            term = gate_ref[0, 0, k * tm + r] * row
            acc = term if acc is None else acc + term
        yrt[pl.ds(_tile_rows(r), ROW_TILE), :] = acc
        return c

    lax.fori_loop(0, tm, reduce, 0, unroll=4)
    y = _from_row_tiles(yrt, tm)
    o_ref[...] = _layer_norm(ALPHA * x_ref[...] + y, g_ref[...], b_ref[...])


def _combine_ln(ys, pstart, ldest_t, gates_t, tcnt_t, tbase_t, lstart_t, x2, ln_g, ln_b, tm):
    n, d = x2.shape
    nt = n // tm
    cur = lambda width: pl.BlockSpec((1, 1, width), lambda i, *_: (i, 0, 0), memory_space=pltpu.SMEM)
    nxt = lambda width: pl.BlockSpec((1, 1, width), lambda i, *_: (jnp.minimum(i + 1, nt - 1), 0, 0),
                                     memory_space=pltpu.SMEM)
    grid_spec = pltpu.PrefetchScalarGridSpec(
        num_scalar_prefetch=1,
        grid=(nt,),
        in_specs=[cur(TOP_K * tm), cur(TOP_K * tm), cur(LANES), cur(LANES), cur(LANES),
                  nxt(LANES), nxt(LANES), nxt(LANES),
                  pl.BlockSpec(memory_space=pl.ANY),
                  pl.BlockSpec((tm, d), lambda i, *_: (i, 0)),
                  pl.BlockSpec((1, d), lambda i, *_: (0, 0)),
                  pl.BlockSpec((1, d), lambda i, *_: (0, 0))],
        out_specs=pl.BlockSpec((tm, d), lambda i, *_: (i, 0)),
        scratch_shapes=[pltpu.VMEM((2, _stage_rows(tm) * ROW_TILE, LANES), F32),
                        pltpu.VMEM((tm * ROW_TILE, LANES), F32),
                        pltpu.SemaphoreType.DMA((2,))],
    )
    return pl.pallas_call(
        functools.partial(_combine_ln_kernel, tm=tm, nt=nt),
        out_shape=jax.ShapeDtypeStruct((n, d), F32),
        grid_spec=grid_spec,
        compiler_params=_params("arbitrary"),
        name="moe_combine_ln3",
    )(pstart, ldest_t, gates_t, tcnt_t, tbase_t, lstart_t, tcnt_t, tbase_t, lstart_t, ys, x2,
      ln_g.reshape(1, d), ln_b.reshape(1, d))


def _round_up(v, m):
    return (v + m - 1) // m * m


def _moe_layer(x2, idx_pad, lrank_pad, tcnt_pad, gates_pad, w_up, b_up, w_down, b_down, ln_g, ln_b, tm=256):
    n = x2.shape[0]
    nt = n // tm
    i32 = jnp.int32
    tcnt = tcnt_pad.reshape(nt, SUBLANES, LANES)[:, 0, :N_EXPERTS]
    tbase = jnp.cumsum(tcnt, axis=0) - tcnt
    counts = jnp.sum(tcnt, axis=0)
    padded = _round_up(counts + SEG_ROWS, MOE_BLOCK)
    pend = jnp.cumsum(padded)
    pstart = pend - padded
    n_blocks = -(-(n * TOP_K + N_EXPERTS * (SEG_ROWS + MOE_BLOCK - 1)) // MOE_BLOCK)
    block_start = jnp.arange(n_blocks, dtype=i32) * MOE_BLOCK
    blk_expert = jnp.minimum(jnp.sum(pend[None, :] <= block_start[:, None], axis=1), N_EXPERTS - 1).astype(i32)
    blk_valid = jnp.clip(counts[blk_expert] - (block_start - pstart[blk_expert]), 0, MOE_BLOCK).astype(i32)
    lseg = _round_up(tcnt, SEG_ROWS)
    lstart = jnp.cumsum(lseg, axis=1) - lseg
    idx = idx_pad[:, :TOP_K].reshape(nt, tm, TOP_K)
    chosen = idx[..., None] == jnp.arange(N_EXPERTS, dtype=i32)
    ldest = jnp.sum(jnp.where(chosen, lstart[:, None, None, :], 0), axis=-1) + lrank_pad[:, :TOP_K].reshape(nt, tm, TOP_K)
    per_tile = lambda a: a.transpose(0, 2, 1).reshape(nt, 1, TOP_K * tm)
    ldest_t = per_tile(ldest.astype(i32))
    gates_t = per_tile(gates_pad[:, :TOP_K].reshape(nt, tm, TOP_K))
    lane_pad = lambda a: jnp.pad(a.astype(i32), ((0, 0), (0, LANES - N_EXPERTS))).reshape(nt, 1, LANES)
    tcnt_t, tbase_t, lstart_t = lane_pad(tcnt), lane_pad(tbase), lane_pad(lstart)
    pstart, counts, padded = pstart.astype(i32), counts.astype(i32), padded.astype(i32)

    xs = _dispatch(x2, pstart, counts, padded, ldest_t, tcnt_t, tbase_t, lstart_t, n_blocks * MOE_BLOCK, tm)
    ys = _experts(xs, blk_expert, blk_valid, w_up, b_up, w_down, b_down)
    return _combine_ln(ys, pstart, ldest_t, gates_t, tcnt_t, tbase_t, lstart_t, x2, ln_g, ln_b, tm)


def _s5_kernel(x_ref, win_ref, bblk_ref, ar_ref, ai_ref, cblk_ref, dsk_ref, wval_ref, wgate_ref, g_ref, b_ref,
               o_ref, bur, bui, sr, si, *, tt, batch):
    rows = tt * batch

    @pl.when(pl.program_id(0) == 0)
    def _():
        sr[...] = jnp.zeros_like(sr)
        si[...] = jnp.zeros_like(si)

    x = x_ref[...]
    u = _dot(x.astype(BF16), win_ref[...])
    ub = u.astype(BF16)
    for c in range(S5_N_CHUNKS):
        bu = _dot(ub[:, c * S5_CHUNK_IN:(c + 1) * S5_CHUNK_IN], bblk_ref[c])
        bur[:, c * S5_CHUNK_STATE:(c + 1) * S5_CHUNK_STATE] = bu[:, :S5_CHUNK_STATE]
        bui[:, c * S5_CHUNK_STATE:(c + 1) * S5_CHUNK_STATE] = bu[:, S5_CHUNK_STATE:]

    for c in range(S5_N_CHUNKS):
        cols = pl.ds(c * S5_CHUNK_STATE, S5_CHUNK_STATE)
        a_r = ar_ref[:, cols]
        a_i = ai_ref[:, cols]

        def step(t, carry):
            s_r, s_i = carry
            rsl = pl.ds(pl.multiple_of(t * batch, batch), batch)
            n_r = a_r * s_r - a_i * s_i + bur[rsl, cols]
            n_i = a_r * s_i + a_i * s_r + bui[rsl, cols]
            bur[rsl, cols] = n_r
            bui[rsl, cols] = n_i
            return n_r, n_i

        f_r, f_i = lax.fori_loop(0, tt, step, (sr[:, cols], si[:, cols]), unroll=True)
        sr[:, cols] = f_r
        si[:, cols] = f_i

    ys = []
    for c in range(S5_N_CHUNKS):
        cols = pl.ds(c * S5_CHUNK_STATE, S5_CHUNK_STATE)
        xri = jnp.concatenate([bur[:, cols].astype(BF16), bui[:, cols].astype(BF16)], axis=-1)
        ys.append(_dot(xri, cblk_ref[c]))
    y = jnp.concatenate(ys, axis=-1) + dsk_ref[...] * u
    yb = _gelu(y).astype(BF16)
    hmix = _dot(yb, wval_ref[...]) * _sigmoid(_dot(yb, wgate_ref[...]))
    o_ref[...] = _layer_norm(ALPHA * x + hmix, g_ref[...], b_ref[...])


def _s5_discretize(log_dt, lambda_re, lambda_im, b_re, b_im, c_re, c_im):
    dt = jnp.exp(log_dt)[:, None]
    mag = jnp.exp(lambda_re * dt)
    ar = mag * jnp.cos(lambda_im * dt)
    ai = mag * jnp.sin(lambda_im * dt)
    den = lambda_re * lambda_re + lambda_im * lambda_im
    zr = ((ar - 1.0) * lambda_re + ai * lambda_im) / den
    zi = (ai * lambda_re - (ar - 1.0) * lambda_im) / den
    bbar_re = zr[..., None] * b_re - zi[..., None] * b_im
    bbar_im = zr[..., None] * b_im + zi[..., None] * b_re
    ng, gc, p, gw = S5_N_CHUNKS, S5_CHUNK_GROUPS, S5_STATE, S5_GROUP
    eye = jnp.eye(gc, dtype=F32)
    bre = bbar_re.reshape(ng, gc, p, gw).transpose(0, 1, 3, 2)
    bim = bbar_im.reshape(ng, gc, p, gw).transpose(0, 1, 3, 2)
    blk_re = jnp.einsum('cgip,gh->cgihp', bre, eye).reshape(ng, gc * gw, gc * p)
    blk_im = jnp.einsum('cgip,gh->cgihp', bim, eye).reshape(ng, gc * gw, gc * p)
    bblk = jnp.concatenate([blk_re, blk_im], axis=-1).astype(BF16)
    cre = c_re.reshape(ng, gc, gw, p).transpose(0, 1, 3, 2)
    cim = c_im.reshape(ng, gc, gw, p).transpose(0, 1, 3, 2)
    cblk_re = jnp.einsum('cgpi,gh->cgphi', cre, eye).reshape(ng, gc * p, gc * gw)
    cblk_im = jnp.einsum('cgpi,gh->cgphi', cim, eye).reshape(ng, gc * p, gc * gw)
    cblk = jnp.concatenate([cblk_re, -cblk_im], axis=1).astype(BF16)
    return bblk, ar.reshape(1, S5_STATES), ai.reshape(1, S5_STATES), cblk


def _s5_mixer_ln(x_tm, w_in, log_dt, lambda_re, lambda_im, b_re, b_im, c_re, c_im, d_skip, w_val, w_gate,
                 ln_g, ln_b, batch, tt=16):
    n, d = x_tm.shape
    rows = tt * batch
    bblk, ar, ai, cblk = _s5_discretize(log_dt, lambda_re, lambda_im, b_re, b_im, c_re, c_im)
    const2 = lambda i: (0, 0)
    const3 = lambda i: (0, 0, 0)
    row = lambda i: (i, 0)
    return pl.pallas_call(
        functools.partial(_s5_kernel, tt=tt, batch=batch),
        out_shape=jax.ShapeDtypeStruct((n, d), F32),
        grid=(n // rows,),
        in_specs=[pl.BlockSpec((rows, d), row),
                  pl.BlockSpec((d, d), const2),
                  pl.BlockSpec(bblk.shape, const3),
                  pl.BlockSpec(ar.shape, const2),
                  pl.BlockSpec(ai.shape, const2),
                  pl.BlockSpec(cblk.shape, const3),
                  pl.BlockSpec((1, d), const2),
                  pl.BlockSpec((d, d), const2),
                  pl.BlockSpec((d, d), const2),
                  pl.BlockSpec((1, d), const2),
                  pl.BlockSpec((1, d), const2)],
        out_specs=pl.BlockSpec((rows, d), row),
        scratch_shapes=[pltpu.VMEM((rows, S5_STATES), F32),
                        pltpu.VMEM((rows, S5_STATES), F32),
                        pltpu.VMEM((batch, S5_STATES), F32),
                        pltpu.VMEM((batch, S5_STATES), F32)],
        compiler_params=_params("arbitrary"),
        name="s5_mixer_ln1",
    )(x_tm, w_in.astype(BF16), bblk, ar, ai, cblk, d_skip.reshape(1, d), w_val.astype(BF16), w_gate.astype(BF16),
      ln_g.reshape(1, d), ln_b.reshape(1, d))


def _diff_lambda_init(layer_idx):
    return 0.8 - 0.6 * math.exp(-0.3 * layer_idx)


def kernel(x, mem, w_mem_kv, l0_w_in, l0_sgu_ln_g, l0_sgu_ln_b, l0_w_spatial, l0_b_spatial, l0_lam_q1, l0_lam_k1, l0_lam_q2, l0_lam_k2, l0_subln_g, l0_w_out, l0_ln1_g, l0_ln1_b, l0_xq, l0_xo, l0_ln2_g, l0_ln2_b, l0_router_w, l0_router_b, l0_exp_w_up, l0_exp_b_up, l0_exp_w_down, l0_exp_b_down, l0_ln3_g, l0_ln3_b, l1_w_in, l1_log_dt, l1_lambda_re, l1_lambda_im, l1_b_re, l1_b_im, l1_c_re, l1_c_im, l1_d_skip, l1_w_val, l1_w_gate, l1_ln1_g, l1_ln1_b, l1_xq, l1_xo, l1_ln2_g, l1_ln2_b, l1_router_w, l1_router_b, l1_exp_w_up, l1_exp_b_up, l1_exp_w_down, l1_exp_b_down, l1_ln3_g, l1_ln3_b):
    batch, seq, d = x.shape
    mem_len = mem.shape[1]
    n = batch * seq
    x0 = x.reshape(n, d)

    kv = _matmul(mem.reshape(batch * mem_len, d), w_mem_kv.astype(BF16), BF16,
                 tm=min(512, batch * mem_len), tn=d)

    a, qk, vt = _inproj_sgu(x0, l0_w_in, l0_sgu_ln_g, l0_sgu_ln_b, l0_w_spatial, l0_b_spatial)
    dattn = _diff_attention(qk, vt, l0_lam_q1, l0_lam_k1, l0_lam_q2, l0_lam_k2, l0_subln_g,
                            _diff_lambda_init(0), batch, seq)
    x1 = _outproj_ln(a, dattn, l0_w_out, x0, l0_ln1_g, l0_ln1_b)
    x2, idx, gates, rank, cnt = _cross_router(x1, kv, l0_xq, l0_xo, l0_ln2_g, l0_ln2_b, l0_router_w, l0_router_b,
                                              batch, seq, mem_len)
    x3 = _moe_layer(x2, idx, rank, cnt, gates, l0_exp_w_up, l0_exp_b_up, l0_exp_w_down, l0_exp_b_down,
                    l0_ln3_g, l0_ln3_b)

    x3_tm = x3.reshape(batch, seq, d).transpose(1, 0, 2).reshape(n, d)
    x4_tm = _s5_mixer_ln(x3_tm, l1_w_in, l1_log_dt, l1_lambda_re, l1_lambda_im, l1_b_re, l1_b_im,
                         l1_c_re, l1_c_im, l1_d_skip, l1_w_val, l1_w_gate, l1_ln1_g, l1_ln1_b, batch)
    x4 = x4_tm.reshape(seq, batch, d).transpose(1, 0, 2).reshape(n, d)
    x5, idx, gates, rank, cnt = _cross_router(x4, kv, l1_xq, l1_xo, l1_ln2_g, l1_ln2_b, l1_router_w, l1_router_b,
                                              batch, seq, mem_len)
    x6 = _moe_layer(x5, idx, rank, cnt, gates, l1_exp_w_up, l1_exp_b_up, l1_exp_w_down, l1_exp_b_down,
                    l1_ln3_g, l1_ln3_b)
    return x6.reshape(batch, seq, d)
```

```python
import functools
import math

import jax
import jax.numpy as jnp
from jax import lax
from jax.experimental import pallas as pl
from jax.experimental.pallas import tpu as pltpu

F32 = jnp.float32
BF16 = jnp.bfloat16

D_MODEL = 1024
CHUNK = 128
SGU_GROUPS = 4
SGU_WIDTH = 512
DIFF_HEADS = 4
DIFF_HEAD_DIM = 64
DIFF_V_DIM = 128
DIFF_QK_WIDTH = DIFF_HEADS * 2 * DIFF_HEAD_DIM
S5_GROUP = 16
S5_GROUPS = 64
S5_STATE = 64
X_HEADS = 4
X_HEAD_DIM = 256
N_EXPERTS = 32
TOP_K = 4
D_FF = 1024
SWIGLU_LIMIT = 7.0
SWIGLU_ALPHA = 1.702
MOE_BLOCK = 512
MOE_HALF = MOE_BLOCK // 2
MOE_TILE = 512
DEPTH = 2
ALPHA = (2 * DEPTH) ** 0.25
LN_EPS = 1e-5
NEG_INF = -1e30

LANES = 128
SUBLANES = 8
VMEM_LIMIT_BYTES = 56 * 1024 * 1024

S5_CHUNK_GROUPS = 8
S5_CHUNK_IN = S5_CHUNK_GROUPS * S5_GROUP
S5_CHUNK_STATE = S5_CHUNK_GROUPS * S5_STATE
S5_N_CHUNKS = S5_GROUPS // S5_CHUNK_GROUPS
S5_STATES = S5_GROUPS * S5_STATE


def _params(*sem):
    return pltpu.CompilerParams(dimension_semantics=sem, vmem_limit_bytes=VMEM_LIMIT_BYTES)


def _gelu(x):
    return 0.5 * x * (1.0 + jnp.tanh(math.sqrt(2.0 / math.pi) * (x + 0.044715 * (x * x * x))))


def _sigmoid(x):
    return 1.0 / (1.0 + jnp.exp(-x))


def _layer_norm(z, g, b):
    mu = jnp.mean(z, axis=-1, keepdims=True)
    zc = z - mu
    var = jnp.mean(zc * zc, axis=-1, keepdims=True)
    return zc * lax.rsqrt(var + LN_EPS) * g + b


def _dot(a, b):
    return jnp.dot(a, b, preferred_element_type=F32)


def _dot_nt(a, b):
    return lax.dot_general(a, b, (((1,), (1,)), ((), ())), preferred_element_type=F32)


def _matmul_kernel(x_ref, w_ref, o_ref):
    o_ref[...] = _dot(x_ref[...].astype(BF16), w_ref[...]).astype(o_ref.dtype)


def _matmul(x, w, out_dtype, tm, tn):
    m, k = x.shape
    n = w.shape[1]
    return pl.pallas_call(
        _matmul_kernel,
        out_shape=jax.ShapeDtypeStruct((m, n), out_dtype),
        grid=(n // tn, m // tm),
        in_specs=[pl.BlockSpec((tm, k), lambda j, i: (i, 0)),
                  pl.BlockSpec((k, tn), lambda j, i: (0, j))],
        out_specs=pl.BlockSpec((tm, tn), lambda j, i: (i, j)),
        compiler_params=_params("parallel", "parallel"),
        name="matmul",
    )(x, w)


def _inproj_sgu_kernel(x_ref, w_ref, lng_ref, lnb_ref, wsp_ref, bsp_ref, a_ref, qk_ref, vt_ref, *, tm):
    xb = x_ref[...].astype(BF16)
    qk_ref[...] = _dot(xb, w_ref[:, 2 * SGU_WIDTH:2 * SGU_WIDTH + 2 * DIFF_QK_WIDTH]).astype(BF16)
    vt_ref[...] = _dot(xb, w_ref[:, 2 * SGU_WIDTH + 2 * DIFF_QK_WIDTH:]).T.astype(BF16)
    h = _dot(xb, w_ref[:, :2 * SGU_WIDTH])
    for g in range(SGU_GROUPS):
        lo = g * CHUNK
        u = _gelu(h[:, lo:lo + CHUNK])
        v = _gelu(h[:, SGU_WIDTH + lo:SGU_WIDTH + lo + CHUNK])
        vn = _layer_norm(v, lng_ref[g:g + 1, :], lnb_ref[g:g + 1, :]).astype(BF16)
        w_g = wsp_ref[g]
        b_g = bsp_ref[:, g:g + 1]
        for c in range(tm // CHUNK):
            r = c * CHUNK
            gate = _dot(w_g, vn[r:r + CHUNK, :]) + b_g
            a_ref[r:r + CHUNK, lo:lo + CHUNK] = (u[r:r + CHUNK, :] * gate).astype(BF16)


def _inproj_sgu(x2d, w_in, ln_g, ln_b, w_spatial, b_spatial, tm=256):
    n, d = x2d.shape
    w = w_in.astype(BF16)
    wsp = jnp.tril(w_spatial).astype(BF16)
    bsp_t = b_spatial.T
    v_width = w.shape[1] - 2 * SGU_WIDTH - 2 * DIFF_QK_WIDTH
    const = lambda i: (0, 0)
    return pl.pallas_call(
        functools.partial(_inproj_sgu_kernel, tm=tm),
        out_shape=(jax.ShapeDtypeStruct((n, SGU_WIDTH), BF16),
                   jax.ShapeDtypeStruct((n, 2 * DIFF_QK_WIDTH), BF16),
                   jax.ShapeDtypeStruct((v_width, n), BF16)),
        grid=(n // tm,),
        in_specs=[pl.BlockSpec((tm, d), lambda i: (i, 0)),
                  pl.BlockSpec(w.shape, const),
                  pl.BlockSpec((SGU_GROUPS, CHUNK), const),
                  pl.BlockSpec((SGU_GROUPS, CHUNK), const),
                  pl.BlockSpec(wsp.shape, lambda i: (0, 0, 0)),
                  pl.BlockSpec(bsp_t.shape, const)],
        out_specs=(pl.BlockSpec((tm, SGU_WIDTH), lambda i: (i, 0)),
                   pl.BlockSpec((tm, 2 * DIFF_QK_WIDTH), lambda i: (i, 0)),
                   pl.BlockSpec((v_width, tm), lambda i: (0, i))),
        compiler_params=_params("parallel"),
        name="inproj_sgu",
    )(x2d, w, ln_g.reshape(SGU_GROUPS, CHUNK), ln_b.reshape(SGU_GROUPS, CHUNK), wsp, bsp_t)


def _diff_attn_kernel(q_ref, k_ref, vt_ref, lam_ref, g_ref, o_ref, *, tq, lam_init):
    i = pl.program_id(1)
    hw = 2 * DIFF_HEAD_DIM
    lane = lax.broadcasted_iota(jnp.int32, (1, hw), 1)
    queries = []
    for h in range(DIFF_HEADS):
        q = q_ref[:, h * hw:(h + 1) * hw] * jnp.asarray(DIFF_HEAD_DIM ** -0.5, BF16)
        zero = jnp.zeros_like(q)
        queries.append((h, jnp.where(lane < DIFF_HEAD_DIM, q, zero)))
        queries.append((h, jnp.where(lane >= DIFF_HEAD_DIM, q, zero)))

    def block(j, carry, masked):
        off = pl.multiple_of(j * tq, tq)
        scores = [_dot_nt(k_ref[pl.ds(off, tq), h * hw:(h + 1) * hw], qc) for h, qc in queries]
        if masked:
            key = lax.broadcasted_iota(jnp.int32, (tq, tq), 0)
            qry = lax.broadcasted_iota(jnp.int32, (tq, tq), 1)
            scores = [jnp.where(key <= qry, st, NEG_INF) for st in scores]
        stats, probs = [], []
        for (m, l, _), st in zip(carry, scores):
            m_new = jnp.maximum(m, jnp.max(st, axis=0, keepdims=True))
            alpha = jnp.exp(m - m_new)
            p = jnp.exp(st - m_new)
            stats.append((m_new, alpha * l + jnp.sum(p, axis=0, keepdims=True), alpha))
            probs.append(p.astype(BF16))
        pv = [_dot(vt_ref[h * DIFF_V_DIM:(h + 1) * DIFF_V_DIM, pl.ds(off, tq)], p)
              for (h, _), p in zip(queries, probs)]
        return tuple((m_new, l_new, alpha * acc + o)
                     for (m_new, l_new, alpha), (_, _, acc), o in zip(stats, carry, pv))

    init = (jnp.full((1, tq), NEG_INF, F32), jnp.zeros((1, tq), F32), jnp.zeros((DIFF_V_DIM, tq), F32))
    carry = lax.fori_loop(0, i, lambda j, c: block(j, c, False), (init,) * len(queries))
    final = block(i, carry, True)

    lam_v = lam_ref[...]
    s_a = jnp.sum(lam_v[0:1, :] * lam_v[1:2, :], axis=-1, keepdims=True)
    s_b = jnp.sum(lam_v[2:3, :] * lam_v[3:4, :], axis=-1, keepdims=True)
    lam = jnp.exp(s_a) - jnp.exp(s_b) + lam_init
    for h in range(DIFF_HEADS):
        (_, l1, a1), (_, l2, a2) = final[2 * h], final[2 * h + 1]
        o_t = a1 / l1 - lam * (a2 / l2)
        o_t = o_t * lax.rsqrt(jnp.mean(o_t * o_t, axis=0, keepdims=True) + LN_EPS) * g_ref[...] * (1.0 - lam_init)
        o_ref[:, h * DIFF_V_DIM:(h + 1) * DIFF_V_DIM] = o_t.T.astype(BF16)


def _diff_attention(qk, vt, lam_q1, lam_k1, lam_q2, lam_k2, subln_g, lam_init, batch, seq, tq=256):
    n = qk.shape[0]
    nq = seq // tq
    lam_v = jnp.stack([lam_q1, lam_k1, lam_q2, lam_k2])
    v_width = DIFF_HEADS * DIFF_V_DIM
    return pl.pallas_call(
        functools.partial(_diff_attn_kernel, tq=tq, lam_init=lam_init),
        out_shape=jax.ShapeDtypeStruct((n, v_width), BF16),
        grid=(batch, nq),
        in_specs=[pl.BlockSpec((tq, DIFF_QK_WIDTH), lambda b, i: (b * nq + i, 0)),
                  pl.BlockSpec((seq, DIFF_QK_WIDTH), lambda b, i: (b, 1)),
                  pl.BlockSpec((v_width, seq), lambda b, i: (0, b)),
                  pl.BlockSpec(lam_v.shape, lambda b, i: (0, 0)),
                  pl.BlockSpec((DIFF_V_DIM, 1), lambda b, i: (0, 0))],
        out_specs=pl.BlockSpec((tq, v_width), lambda b, i: (b * nq + i, 0)),
        compiler_params=_params("parallel", "parallel"),
        name="diff_attention",
    )(qk, qk, vt, lam_v, subln_g.reshape(DIFF_V_DIM, 1))


def _outproj_ln_kernel(a_ref, d_ref, w_ref, x_ref, g_ref, b_ref, o_ref):
    y = _dot(a_ref[...], w_ref[:SGU_WIDTH, :]) + _dot(d_ref[...], w_ref[SGU_WIDTH:, :])
    o_ref[...] = _layer_norm(ALPHA * x_ref[...] + y, g_ref[...], b_ref[...])


def _outproj_ln(a, dattn, w_out, x2d, ln_g, ln_b, tm=512):
    n, d = x2d.shape
    w = w_out.astype(BF16)
    const = lambda i: (0, 0)
    row = lambda i: (i, 0)
    return pl.pallas_call(
        _outproj_ln_kernel,
        out_shape=jax.ShapeDtypeStruct((n, d), F32),
        grid=(n // tm,),
        in_specs=[pl.BlockSpec((tm, a.shape[1]), row),
                  pl.BlockSpec((tm, dattn.shape[1]), row),
                  pl.BlockSpec(w.shape, const),
                  pl.BlockSpec((tm, d), row),
                  pl.BlockSpec((1, d), const),
                  pl.BlockSpec((1, d), const)],
        out_specs=pl.BlockSpec((tm, d), row),
        compiler_params=_params("parallel"),
        name="outproj_ln1",
    )(a, dattn, w, x2d, ln_g.reshape(1, d), ln_b.reshape(1, d))


def _cross_router_kernel(x_ref, wq_ref, wo_ref, k_ref, v_ref, g_ref, b_ref, rwh_ref, rwl_ref, rb_ref,
                         x2_ref, idx_ref, gate_ref, rank_ref, cnt_ref, *, tile, tiles_per_step):
    for t in range(tiles_per_step):
        _cross_router_tile(x_ref, wq_ref, wo_ref, k_ref, v_ref, g_ref, b_ref, rwh_ref, rwl_ref, rb_ref,
                           x2_ref, idx_ref, gate_ref, rank_ref, cnt_ref, slice(t * tile, (t + 1) * tile),
                           slice(t * SUBLANES, (t + 1) * SUBLANES))


def _cross_router_tile(x_ref, wq_ref, wo_ref, k_ref, v_ref, g_ref, b_ref, rwh_ref, rwl_ref, rb_ref,
                       x2_ref, idx_ref, gate_ref, rank_ref, cnt_ref, rows, cnt_rows):
    x = x_ref[rows, :]
    q = (_dot(x.astype(BF16), wq_ref[...]) * (X_HEAD_DIM ** -0.5)).astype(BF16)
    heads = []
    for h in range(X_HEADS):
        lo = h * X_HEAD_DIM
        s = _dot_nt(q[:, lo:lo + X_HEAD_DIM], k_ref[:, lo:lo + X_HEAD_DIM])
        p = jnp.exp(s - jnp.max(s, axis=-1, keepdims=True))
        p = p / jnp.sum(p, axis=-1, keepdims=True)
        heads.append(_dot(p.astype(BF16), v_ref[:, lo:lo + X_HEAD_DIM]).astype(BF16))
    o = jnp.concatenate(heads, axis=-1)
    x2 = _layer_norm(ALPHA * x + _dot(o, wo_ref[...]), g_ref[...], b_ref[...])
    x2_ref[rows, :] = x2

    x_hi = x2.astype(BF16)
    x_lo = (x2 - x_hi.astype(F32)).astype(BF16)
    logits = _dot(x_hi, rwh_ref[...]) + _dot(x_lo, rwh_ref[...]) + _dot(x_hi, rwl_ref[...]) + rb_ref[...]
    lane = lax.broadcasted_iota(jnp.int32, logits.shape, 1).astype(F32)
    work = logits
    top_v, top_i = [], []
    for _ in range(TOP_K):
        m = jnp.max(work, axis=-1, keepdims=True)
        sel = jnp.min(jnp.where(work == m, lane, float(LANES)), axis=-1, keepdims=True)
        top_v.append(m)
        top_i.append(sel)
        work = jnp.where(lane == sel, -jnp.inf, work)
    e = [jnp.exp(v - top_v[0]) for v in top_v]
    denom = e[0] + e[1] + e[2] + e[3]
    idx_out = jnp.zeros(logits.shape, F32)
    gate_out = jnp.zeros(logits.shape, F32)
    for k in range(TOP_K):
        idx_out = jnp.where(lane == float(k), top_i[k], idx_out)
        gate_out = jnp.where(lane == float(k), e[k] / denom, gate_out)
    idx_ref[rows, :] = idx_out.astype(jnp.int32)
    gate_ref[rows, :] = gate_out

    tm = logits.shape[0]
    onehot = jnp.zeros(logits.shape, F32)
    for k in range(TOP_K):
        onehot = jnp.where(lane == top_i[k], 1.0, onehot)
    r_i = lax.broadcasted_iota(jnp.int32, (tm, tm), 0)
    c_i = lax.broadcasted_iota(jnp.int32, (tm, tm), 1)
    earlier = jnp.where(c_i < r_i, 1.0, 0.0).astype(BF16)
    pos = _dot(earlier, onehot.astype(BF16))
    rank_out = jnp.zeros(logits.shape, F32)
    for k in range(TOP_K):
        rk = jnp.sum(jnp.where(lane == top_i[k], pos, 0.0), axis=-1, keepdims=True)
        rank_out = jnp.where(lane == float(k), rk, rank_out)
    rank_ref[rows, :] = rank_out.astype(jnp.int32)
    total = jnp.sum(onehot, axis=0, keepdims=True)
    cnt_ref[cnt_rows, :] = jnp.broadcast_to(total, (SUBLANES, LANES)).astype(jnp.int32)


def _cross_router(x1, kv, w_q, w_o, ln_g, ln_b, router_w, router_b, batch, seq, mem_len, tile=MOE_TILE,
                  tiles_per_step=1):
    n, d = x1.shape
    tm = tile * tiles_per_step
    nt = seq // tm
    pad = LANES - N_EXPERTS
    rw = jnp.pad(router_w, ((0, 0), (0, pad)))
    rw_hi = rw.astype(BF16)
    rw_lo = (rw - rw_hi.astype(F32)).astype(BF16)
    rb = jnp.pad(router_b, (0, pad), constant_values=-jnp.inf).reshape(1, LANES)
    const = lambda b, i: (0, 0)
    row = lambda b, i: (b * nt + i, 0)
    return pl.pallas_call(
        functools.partial(_cross_router_kernel, tile=tile, tiles_per_step=tiles_per_step),
        out_shape=(jax.ShapeDtypeStruct((n, d), F32),
                   jax.ShapeDtypeStruct((n, LANES), jnp.int32),
                   jax.ShapeDtypeStruct((n, LANES), F32),
                   jax.ShapeDtypeStruct((n, LANES), jnp.int32),
                   jax.ShapeDtypeStruct((n // tile * SUBLANES, LANES), jnp.int32)),
        grid=(batch, nt),
        in_specs=[pl.BlockSpec((tm, d), row),
                  pl.BlockSpec((d, d), const),
                  pl.BlockSpec((d, d), const),
                  pl.BlockSpec((mem_len, d), lambda b, i: (b, 0)),
                  pl.BlockSpec((mem_len, d), lambda b, i: (b, 1)),
                  pl.BlockSpec((1, d), const),
                  pl.BlockSpec((1, d), const),
                  pl.BlockSpec((d, LANES), const),
                  pl.BlockSpec((d, LANES), const),
                  pl.BlockSpec((1, LANES), const)],
        out_specs=(pl.BlockSpec((tm, d), row),
                   pl.BlockSpec((tm, LANES), row),
                   pl.BlockSpec((tm, LANES), row),
                   pl.BlockSpec((tm, LANES), row),
                   pl.BlockSpec((tiles_per_step * SUBLANES, LANES), row)),
        compiler_params=_params("parallel", "parallel"),
        name="cross_attn_router",
    )(x1, w_q.astype(BF16), w_o.astype(BF16), kv, kv, ln_g.reshape(1, d), ln_b.reshape(1, d), rw_hi, rw_lo, rb)


ROW_TILE = D_MODEL // LANES
SEG_ROWS = 32


def _to_row_tiles(dst_ref, x, rows, first_row=0):
    for c in range(ROW_TILE):
        dst_ref[pl.ds(first_row * ROW_TILE + c, rows, stride=ROW_TILE), :] = x[:, c * LANES:(c + 1) * LANES]


def _from_row_tiles(src_ref, rows, first_row=0):
    return jnp.concatenate([src_ref[pl.ds(first_row * ROW_TILE + c, rows, stride=ROW_TILE), :]
                            for c in range(ROW_TILE)], axis=-1)


def _tile_rows(row):
    return pl.multiple_of(row * ROW_TILE, ROW_TILE)


def _num_segments(count):
    return lax.shift_right_logical(count + (SEG_ROWS - 1), SEG_ROWS.bit_length() - 1)


def _tile_segments(tcnt_ref):
    return lax.fori_loop(0, N_EXPERTS, lambda e, t: t + _num_segments(tcnt_ref[0, 0, e]), 0)


def _dispatch_kernel(pstart_ref, counts_ref, padded_ref, ldest_ref, tcnt_ref, tbase_ref, lstart_ref, x_ref,
                     xs_hbm, xrt, stage, zeros, nseg, sems, zsem, rsem, *, tm, nt):
    i = pl.program_id(0)
    slot = lax.rem(i, 2)

    @pl.when(i == 0)
    def _():
        stage[...] = jnp.zeros_like(stage)
        zeros[...] = jnp.zeros_like(zeros)

    _to_row_tiles(xrt, x_ref[...], tm)
    st = stage.at[slot]
    for k in range(TOP_K):
        def move(r, c, k=k):
            dst = pl.multiple_of(ldest_ref[0, 0, k * tm + r], ROW_TILE)
            st[pl.ds(dst, ROW_TILE), :] = xrt[pl.ds(_tile_rows(r), ROW_TILE), :]
            return c

        lax.fori_loop(0, tm, move, 0, unroll=8)

    def seg_copy(s, src_row, dst_row):
        return pltpu.make_async_copy(stage.at[s, pl.ds(_tile_rows(src_row), SEG_ROWS * ROW_TILE), :],
                                     xs_hbm.at[pl.ds(_tile_rows(dst_row), SEG_ROWS * ROW_TILE), :], sems.at[s])

    def wait_segments(s, count):
        def wait(j, c):
            seg_copy(s, 0, 0).wait()
            return c

        lax.fori_loop(0, count, wait, 0)

    @pl.when(i > 0)
    def _():
        wait_segments(1 - slot, nseg[1 - slot])

    def issue(e, total):
        n = _num_segments(tcnt_ref[0, 0, e])
        src0 = lstart_ref[0, 0, e]
        dst0 = pstart_ref[e] + tbase_ref[0, 0, e]

        def one(j, c):
            seg_copy(slot, src0 + j * SEG_ROWS, dst0 + j * SEG_ROWS).start()
            return c

        lax.fori_loop(0, n, one, 0)
        return total + n

    total = lax.fori_loop(0, N_EXPERTS, issue, 0)
    nseg[slot] = total

    @pl.when(i == nt - 1)
    def _():
        wait_segments(slot, total)

        def gap(e):
            lo = pstart_ref[e] + counts_ref[e]
            width = padded_ref[e] - counts_ref[e]
            full = lax.shift_right_logical(width, SEG_ROWS.bit_length() - 1)
            return lo, full, width - full * SEG_ROWS

        def seg_zero(row):
            return pltpu.make_async_copy(zeros, xs_hbm.at[pl.ds(_tile_rows(row), SEG_ROWS * ROW_TILE), :], zsem)

        def row_zero(row):
            return pltpu.make_async_copy(zeros.at[pl.ds(0, ROW_TILE), :],
                                         xs_hbm.at[pl.ds(_tile_rows(row), ROW_TILE), :], rsem)

        def fill(e, c):
            lo, full, rest = gap(e)
            lax.fori_loop(0, full, lambda j, c2: (seg_zero(lo + j * SEG_ROWS).start(), c2)[1], 0)
            lax.fori_loop(0, rest, lambda j, c2: (row_zero(lo + full * SEG_ROWS + j).start(), c2)[1], 0)
            return c

        def fill_wait(e, c):
            _, full, rest = gap(e)
            lax.fori_loop(0, full, lambda j, c2: (seg_zero(0).wait(), c2)[1], 0)
            lax.fori_loop(0, rest, lambda j, c2: (row_zero(0).wait(), c2)[1], 0)
            return c

        lax.fori_loop(0, N_EXPERTS, fill, 0)
        lax.fori_loop(0, N_EXPERTS, fill_wait, 0)


def _stage_rows(tm):
    return TOP_K * tm + N_EXPERTS * SEG_ROWS


def _dispatch(x2, pstart, counts, padded, ldest_t, tcnt_t, tbase_t, lstart_t, n_pad, tm):
    n, d = x2.shape
    nt = n // tm
    smem_tile = lambda width: pl.BlockSpec((1, 1, width), lambda i, *_: (i, 0, 0), memory_space=pltpu.SMEM)
    grid_spec = pltpu.PrefetchScalarGridSpec(
        num_scalar_prefetch=3,
        grid=(nt,),
        in_specs=[smem_tile(TOP_K * tm), smem_tile(LANES), smem_tile(LANES), smem_tile(LANES),
                  pl.BlockSpec((tm, d), lambda i, *_: (i, 0))],
        out_specs=pl.BlockSpec(memory_space=pl.ANY),
        scratch_shapes=[pltpu.VMEM((tm * ROW_TILE, LANES), F32),
                        pltpu.VMEM((2, _stage_rows(tm) * ROW_TILE, LANES), F32),
                        pltpu.VMEM((SEG_ROWS * ROW_TILE, LANES), F32),
                        pltpu.SMEM((2,), jnp.int32),
                        pltpu.SemaphoreType.DMA((2,)),
                        pltpu.SemaphoreType.DMA,
                        pltpu.SemaphoreType.DMA],
    )
    return pl.pallas_call(
        functools.partial(_dispatch_kernel, tm=tm, nt=nt),
        out_shape=jax.ShapeDtypeStruct((n_pad * ROW_TILE, LANES), F32),
        grid_spec=grid_spec,
        compiler_params=_params("arbitrary"),
        name="moe_dispatch",
    )(pstart, counts, padded, ldest_t, tcnt_t, tbase_t, lstart_t, x2)


def _expert_kernel(be_ref, valid_ref, xs_ref, wup_ref, bup_ref, wdn_ref, bdn_ref, y_ref, wup_bf, wdn_bf):
    i = pl.program_id(0)

    @pl.when(valid_ref[i] == 0)
    def _():
        y_ref[...] = jnp.zeros_like(y_ref)

    @pl.when(valid_ref[i] > 0)
    def _():
        prev = be_ref[jnp.maximum(i - 1, 0)]

        @pl.when(jnp.logical_or(i == 0, be_ref[i] != prev))
        def _():
            wup_bf[...] = wup_ref[0].astype(BF16)
            wdn_bf[...] = wdn_ref[0].astype(BF16)

        def expert_rows(first_row):
            xb = _from_row_tiles(xs_ref, MOE_HALF, first_row).astype(BF16)
            h = _dot(xb, wup_bf[...]) + bup_ref[0]
            glu = jnp.minimum(h[:, :D_FF], SWIGLU_LIMIT)
            lin = jnp.clip(h[:, D_FF:], -SWIGLU_LIMIT, SWIGLU_LIMIT)
            act = glu * _sigmoid(SWIGLU_ALPHA * glu) * (lin + 1.0)
            _to_row_tiles(y_ref, _dot(act.astype(BF16), wdn_bf[...]) + bdn_ref[0], MOE_HALF, first_row)

        @pl.when(valid_ref[i] > MOE_HALF)
        def _():
            expert_rows(0)
            expert_rows(MOE_HALF)

        @pl.when(valid_ref[i] <= MOE_HALF)
        def _():
            expert_rows(0)
            y_ref[MOE_HALF * ROW_TILE:, :] = jnp.zeros((MOE_HALF * ROW_TILE, LANES), F32)


def _experts(xs, blk_expert, blk_valid, w_up, b_up, w_down, b_down):
    n_blocks = blk_expert.shape[0]
    d = w_up.shape[1]
    f2 = w_up.shape[2]
    blk_rows = MOE_BLOCK * ROW_TILE
    grid_spec = pltpu.PrefetchScalarGridSpec(
        num_scalar_prefetch=2,
        grid=(n_blocks,),
        in_specs=[pl.BlockSpec((blk_rows, LANES), lambda i, be, va: (jnp.where(va[i] > 0, i, 0), 0)),
                  pl.BlockSpec((1, d, f2), lambda i, be, va: (be[i], 0, 0)),
                  pl.BlockSpec((1, 1, f2), lambda i, be, va: (be[i], 0, 0)),
                  pl.BlockSpec((1, D_FF, d), lambda i, be, va: (be[i], 0, 0)),
                  pl.BlockSpec((1, 1, d), lambda i, be, va: (be[i], 0, 0))],
        out_specs=pl.BlockSpec((blk_rows, LANES), lambda i, be, va: (i, 0)),
        scratch_shapes=[pltpu.VMEM((d, f2), BF16),
                        pltpu.VMEM((D_FF, d), BF16)],
    )
    return pl.pallas_call(
        _expert_kernel,
        out_shape=jax.ShapeDtypeStruct((n_blocks * blk_rows, LANES), F32),
        grid_spec=grid_spec,
        compiler_params=_params("arbitrary"),
        name="moe_experts",
    )(blk_expert, blk_valid, xs, w_up, b_up.reshape(N_EXPERTS, 1, f2), w_down, b_down.reshape(N_EXPERTS, 1, d))


def _combine_ln_kernel(pstart_ref, ldest_ref, gate_ref, tcnt_ref, tbase_ref, lstart_ref,
                       tcnt_next_ref, tbase_next_ref, lstart_next_ref, ys_hbm, x_ref, g_ref, b_ref,
                       o_ref, stage, yrt, sems, *, tm, nt):
    i = pl.program_id(0)
    slot = lax.rem(i, 2)

    def seg_copy(s, src_row, dst_row):
        return pltpu.make_async_copy(ys_hbm.at[pl.ds(_tile_rows(src_row), SEG_ROWS * ROW_TILE), :],
                                     stage.at[s, pl.ds(_tile_rows(dst_row), SEG_ROWS * ROW_TILE), :], sems.at[s])

    def fetch(s, tcnt, tbase, lstart):
        def per_expert(e, c):
            src0 = pstart_ref[e] + tbase[0, 0, e]
            dst0 = lstart[0, 0, e]

            def one(j, c2):
                seg_copy(s, src0 + j * SEG_ROWS, dst0 + j * SEG_ROWS).start()
                return c2

            lax.fori_loop(0, _num_segments(tcnt[0, 0, e]), one, 0)
            return c

        lax.fori_loop(0, N_EXPERTS, per_expert, 0)

    @pl.when(i == 0)
    def _():
        fetch(0, tcnt_ref, tbase_ref, lstart_ref)

    @pl.when(i + 1 < nt)
    def _():
        fetch(1 - slot, tcnt_next_ref, tbase_next_ref, lstart_next_ref)

    def wait(j, c):
        seg_copy(slot, 0, 0).wait()
        return c

    lax.fori_loop(0, _tile_segments(tcnt_ref), wait, 0)

    st = stage.at[slot]

    def reduce(r, c):
        acc = None
        for k in range(TOP_K):
            row = st[pl.ds(pl.multiple_of(ldest_ref[0, 0, k * tm + r], ROW_TILE), ROW_TILE), :]
            term = gate_ref[0, 0, k * tm + r] * row
            acc = term if acc is None else acc + term
        yrt[pl.ds(_tile_rows(r), ROW_TILE), :] = acc
        return c

    lax.fori_loop(0, tm, reduce, 0, unroll=8)
    y = _from_row_tiles(yrt, tm)
    o_ref[...] = _layer_norm(ALPHA * x_ref[...] + y, g_ref[...], b_ref[...])


def _combine_ln(ys, pstart, ldest_t, gates_t, tcnt_t, tbase_t, lstart_t, x2, ln_g, ln_b, tm):
    n, d = x2.shape
    nt = n // tm
    cur = lambda width: pl.BlockSpec((1, 1, width), lambda i, *_: (i, 0, 0), memory_space=pltpu.SMEM)
    nxt = lambda width: pl.BlockSpec((1, 1, width), lambda i, *_: (jnp.minimum(i + 1, nt - 1), 0, 0),
                                     memory_space=pltpu.SMEM)
    grid_spec = pltpu.PrefetchScalarGridSpec(
        num_scalar_prefetch=1,
        grid=(nt,),
        in_specs=[cur(TOP_K * tm), cur(TOP_K * tm), cur(LANES), cur(LANES), cur(LANES),
                  nxt(LANES), nxt(LANES), nxt(LANES),
                  pl.BlockSpec(memory_space=pl.ANY),
                  pl.BlockSpec((tm, d), lambda i, *_: (i, 0)),
                  pl.BlockSpec((1, d), lambda i, *_: (0, 0)),
                  pl.BlockSpec((1, d), lambda i, *_: (0, 0))],
        out_specs=pl.BlockSpec((tm, d), lambda i, *_: (i, 0)),
        scratch_shapes=[pltpu.VMEM((2, _stage_rows(tm) * ROW_TILE, LANES), F32),
                        pltpu.VMEM((tm * ROW_TILE, LANES), F32),
                        pltpu.SemaphoreType.DMA((2,))],
    )
    return pl.pallas_call(
        functools.partial(_combine_ln_kernel, tm=tm, nt=nt),
        out_shape=jax.ShapeDtypeStruct((n, d), F32),
        grid_spec=grid_spec,
        compiler_params=_params("arbitrary"),
        name="moe_combine_ln3",
    )(pstart, ldest_t, gates_t, tcnt_t, tbase_t, lstart_t, tcnt_t, tbase_t, lstart_t, ys, x2,
      ln_g.reshape(1, d), ln_b.reshape(1, d))


def _round_up(v, m):
    return (v + m - 1) // m * m


def _moe_layer(x2, idx_pad, lrank_pad, tcnt_pad, gates_pad, w_up, b_up, w_down, b_down, ln_g, ln_b, tm=MOE_TILE):
    n = x2.shape[0]
    nt = n // tm
    i32 = jnp.int32
    tcnt = tcnt_pad.reshape(nt, SUBLANES, LANES)[:, 0, :N_EXPERTS]
    tbase = jnp.cumsum(tcnt, axis=0) - tcnt
    counts = jnp.sum(tcnt, axis=0)
    padded = _round_up(counts + SEG_ROWS, MOE_BLOCK)
    pend = jnp.cumsum(padded)
    pstart = pend - padded
    n_blocks = -(-(n * TOP_K + N_EXPERTS * (SEG_ROWS + MOE_BLOCK - 1)) // MOE_BLOCK)
    block_start = jnp.arange(n_blocks, dtype=i32) * MOE_BLOCK
    blk_expert = jnp.minimum(jnp.sum(pend[None, :] <= block_start[:, None], axis=1), N_EXPERTS - 1).astype(i32)
    blk_valid = jnp.clip(counts[blk_expert] - (block_start - pstart[blk_expert]), 0, MOE_BLOCK).astype(i32)
    lseg = _round_up(tcnt, SEG_ROWS)
    lstart = jnp.cumsum(lseg, axis=1) - lseg
    idx = idx_pad[:, :TOP_K].reshape(nt, tm, TOP_K)
    chosen = idx[..., None] == jnp.arange(N_EXPERTS, dtype=i32)
    ldest = jnp.sum(jnp.where(chosen, lstart[:, None, None, :], 0), axis=-1) + lrank_pad[:, :TOP_K].reshape(nt, tm, TOP_K)
    per_tile = lambda a: a.transpose(0, 2, 1).reshape(nt, 1, TOP_K * tm)
    ldest_t = per_tile(ldest.astype(i32) * ROW_TILE)
    gates_t = per_tile(gates_pad[:, :TOP_K].reshape(nt, tm, TOP_K))
    lane_pad = lambda a: jnp.pad(a.astype(i32), ((0, 0), (0, LANES - N_EXPERTS))).reshape(nt, 1, LANES)
    tcnt_t, tbase_t, lstart_t = lane_pad(tcnt), lane_pad(tbase), lane_pad(lstart)
    pstart, counts, padded = pstart.astype(i32), counts.astype(i32), padded.astype(i32)

    xs = _dispatch(x2, pstart, counts, padded, ldest_t, tcnt_t, tbase_t, lstart_t, n_blocks * MOE_BLOCK, tm)
    ys = _experts(xs, blk_expert, blk_valid, w_up, b_up, w_down, b_down)
    return _combine_ln(ys, pstart, ldest_t, gates_t, tcnt_t, tbase_t, lstart_t, x2, ln_g, ln_b, tm)


def _s5_kernel(x_ref, win_ref, bblk_ref, ar_ref, ai_ref, cblk_ref, dsk_ref, wval_ref, wgate_ref, g_ref, b_ref,
               o_ref, bur, bui, sr, si, *, tt, batch):
    rows = tt * batch

    @pl.when(pl.program_id(0) == 0)
    def _():
        sr[...] = jnp.zeros_like(sr)
        si[...] = jnp.zeros_like(si)

    x = x_ref[...]
    u = _dot(x.astype(BF16), win_ref[...])
    ub = u.astype(BF16)
    for c in range(S5_N_CHUNKS):
        bu = _dot(ub[:, c * S5_CHUNK_IN:(c + 1) * S5_CHUNK_IN], bblk_ref[c])
        bur[:, c * S5_CHUNK_STATE:(c + 1) * S5_CHUNK_STATE] = bu[:, :S5_CHUNK_STATE]
        bui[:, c * S5_CHUNK_STATE:(c + 1) * S5_CHUNK_STATE] = bu[:, S5_CHUNK_STATE:]

    for c in range(S5_N_CHUNKS):
        cols = pl.ds(c * S5_CHUNK_STATE, S5_CHUNK_STATE)
        a_r = ar_ref[:, cols]
        a_i = ai_ref[:, cols]

        def step(t, carry):
            s_r, s_i = carry
            rsl = pl.ds(pl.multiple_of(t * batch, batch), batch)
            n_r = a_r * s_r - a_i * s_i + bur[rsl, cols]
            n_i = a_r * s_i + a_i * s_r + bui[rsl, cols]
            bur[rsl, cols] = n_r
            bui[rsl, cols] = n_i
            return n_r, n_i

        f_r, f_i = lax.fori_loop(0, tt, step, (sr[:, cols], si[:, cols]), unroll=True)
        sr[:, cols] = f_r
        si[:, cols] = f_i

    ys = []
    for c in range(S5_N_CHUNKS):
        cols = pl.ds(c * S5_CHUNK_STATE, S5_CHUNK_STATE)
        xri = jnp.concatenate([bur[:, cols].astype(BF16), bui[:, cols].astype(BF16)], axis=-1)
        ys.append(_dot(xri, cblk_ref[c]))
    y = jnp.concatenate(ys, axis=-1) + dsk_ref[...] * u
    yb = _gelu(y).astype(BF16)
    hmix = _dot(yb, wval_ref[...]) * _sigmoid(_dot(yb, wgate_ref[...]))
    o_ref[...] = _layer_norm(ALPHA * x + hmix, g_ref[...], b_ref[...])


def _s5_discretize(log_dt, lambda_re, lambda_im, b_re, b_im, c_re, c_im):
    dt = jnp.exp(log_dt)[:, None]
    mag = jnp.exp(lambda_re * dt)
    ar = mag * jnp.cos(lambda_im * dt)
    ai = mag * jnp.sin(lambda_im * dt)
    den = lambda_re * lambda_re + lambda_im * lambda_im
    zr = ((ar - 1.0) * lambda_re + ai * lambda_im) / den
    zi = (ai * lambda_re - (ar - 1.0) * lambda_im) / den
    bbar_re = zr[..., None] * b_re - zi[..., None] * b_im
    bbar_im = zr[..., None] * b_im + zi[..., None] * b_re
    ng, gc, p, gw = S5_N_CHUNKS, S5_CHUNK_GROUPS, S5_STATE, S5_GROUP
    eye = jnp.eye(gc, dtype=F32)
    bre = bbar_re.reshape(ng, gc, p, gw).transpose(0, 1, 3, 2)
    bim = bbar_im.reshape(ng, gc, p, gw).transpose(0, 1, 3, 2)
    blk_re = jnp.einsum('cgip,gh->cgihp', bre, eye).reshape(ng, gc * gw, gc * p)
    blk_im = jnp.einsum('cgip,gh->cgihp', bim, eye).reshape(ng, gc * gw, gc * p)
    bblk = jnp.concatenate([blk_re, blk_im], axis=-1).astype(BF16)
    cre = c_re.reshape(ng, gc, gw, p).transpose(0, 1, 3, 2)
    cim = c_im.reshape(ng, gc, gw, p).transpose(0, 1, 3, 2)
    cblk_re = jnp.einsum('cgpi,gh->cgphi', cre, eye).reshape(ng, gc * p, gc * gw)
    cblk_im = jnp.einsum('cgpi,gh->cgphi', cim, eye).reshape(ng, gc * p, gc * gw)
    cblk = jnp.concatenate([cblk_re, -cblk_im], axis=1).astype(BF16)
    return bblk, ar.reshape(1, S5_STATES), ai.reshape(1, S5_STATES), cblk


def _s5_mixer_ln(x_tm, w_in, log_dt, lambda_re, lambda_im, b_re, b_im, c_re, c_im, d_skip, w_val, w_gate,
                 ln_g, ln_b, batch, tt=16):
    n, d = x_tm.shape
    rows = tt * batch
    bblk, ar, ai, cblk = _s5_discretize(log_dt, lambda_re, lambda_im, b_re, b_im, c_re, c_im)
    const2 = lambda i: (0, 0)
    const3 = lambda i: (0, 0, 0)
    row = lambda i: (i, 0)
    return pl.pallas_call(
        functools.partial(_s5_kernel, tt=tt, batch=batch),
        out_shape=jax.ShapeDtypeStruct((n, d), F32),
        grid=(n // rows,),
        in_specs=[pl.BlockSpec((rows, d), row),
                  pl.BlockSpec((d, d), const2),
                  pl.BlockSpec(bblk.shape, const3),
                  pl.BlockSpec(ar.shape, const2),
                  pl.BlockSpec(ai.shape, const2),
                  pl.BlockSpec(cblk.shape, const3),
                  pl.BlockSpec((1, d), const2),
                  pl.BlockSpec((d, d), const2),
                  pl.BlockSpec((d, d), const2),
                  pl.BlockSpec((1, d), const2),
                  pl.BlockSpec((1, d), const2)],
        out_specs=pl.BlockSpec((rows, d), row),
        scratch_shapes=[pltpu.VMEM((rows, S5_STATES), F32),
                        pltpu.VMEM((rows, S5_STATES), F32),
                        pltpu.VMEM((batch, S5_STATES), F32),
                        pltpu.VMEM((batch, S5_STATES), F32)],
        compiler_params=_params("arbitrary"),
        name="s5_mixer_ln1",
    )(x_tm, w_in.astype(BF16), bblk, ar, ai, cblk, d_skip.reshape(1, d), w_val.astype(BF16), w_gate.astype(BF16),
      ln_g.reshape(1, d), ln_b.reshape(1, d))


def _diff_lambda_init(layer_idx):
    return 0.8 - 0.6 * math.exp(-0.3 * layer_idx)


def kernel(x, mem, w_mem_kv, l0_w_in, l0_sgu_ln_g, l0_sgu_ln_b, l0_w_spatial, l0_b_spatial, l0_lam_q1, l0_lam_k1, l0_lam_q2, l0_lam_k2, l0_subln_g, l0_w_out, l0_ln1_g, l0_ln1_b, l0_xq, l0_xo, l0_ln2_g, l0_ln2_b, l0_router_w, l0_router_b, l0_exp_w_up, l0_exp_b_up, l0_exp_w_down, l0_exp_b_down, l0_ln3_g, l0_ln3_b, l1_w_in, l1_log_dt, l1_lambda_re, l1_lambda_im, l1_b_re, l1_b_im, l1_c_re, l1_c_im, l1_d_skip, l1_w_val, l1_w_gate, l1_ln1_g, l1_ln1_b, l1_xq, l1_xo, l1_ln2_g, l1_ln2_b, l1_router_w, l1_router_b, l1_exp_w_up, l1_exp_b_up, l1_exp_w_down, l1_exp_b_down, l1_ln3_g, l1_ln3_b):
    batch, seq, d = x.shape
    mem_len = mem.shape[1]
    n = batch * seq
    x0 = x.reshape(n, d)

    kv = _matmul(mem.reshape(batch * mem_len, d), w_mem_kv.astype(BF16), BF16,
                 tm=min(512, batch * mem_len), tn=d)

    a, qk, vt = _inproj_sgu(x0, l0_w_in, l0_sgu_ln_g, l0_sgu_ln_b, l0_w_spatial, l0_b_spatial)
    dattn = _diff_attention(qk, vt, l0_lam_q1, l0_lam_k1, l0_lam_q2, l0_lam_k2, l0_subln_g,
                            _diff_lambda_init(0), batch, seq)
    x1 = _outproj_ln(a, dattn, l0_w_out, x0, l0_ln1_g, l0_ln1_b)
    x2, idx, gates, rank, cnt = _cross_router(x1, kv, l0_xq, l0_xo, l0_ln2_g, l0_ln2_b, l0_router_w, l0_router_b,
                                              batch, seq, mem_len)
    x3 = _moe_layer(x2, idx, rank, cnt, gates, l0_exp_w_up, l0_exp_b_up, l0_exp_w_down, l0_exp_b_down,
                    l0_ln3_g, l0_ln3_b)

    x3_tm = x3.reshape(batch, seq, d).transpose(1, 0, 2).reshape(n, d)
    x4_tm = _s5_mixer_ln(x3_tm, l1_w_in, l1_log_dt, l1_lambda_re, l1_lambda_im, l1_b_re, l1_b_im,
                         l1_c_re, l1_c_im, l1_d_skip, l1_w_val, l1_w_gate, l1_ln1_g, l1_ln1_b, batch)
    x4 = x4_tm.reshape(seq, batch, d).transpose(1, 0, 2).reshape(n, d)
    x5, idx, gates, rank, cnt = _cross_router(x4, kv, l1_xq, l1_xo, l1_ln2_g, l1_ln2_b, l1_router_w, l1_router_b,
                                              batch, seq, mem_len)
    x6 = _moe_layer(x5, idx, rank, cnt, gates, l1_exp_w_up, l1_exp_b_up, l1_exp_w_down, l1_exp_b_down,
                    l1_ln3_g, l1_ln3_b)
    return x6.reshape(batch, seq, d)
```

```python
import functools
import math

import jax
import jax.numpy as jnp
from jax import lax
from jax.experimental import pallas as pl
from jax.experimental.pallas import tpu as pltpu

F32 = jnp.float32
BF16 = jnp.bfloat16

D_MODEL = 1024
CHUNK = 128
SGU_GROUPS = 4
SGU_WIDTH = 512
DIFF_HEADS = 4
DIFF_HEAD_DIM = 64
DIFF_V_DIM = 128
DIFF_QK_WIDTH = DIFF_HEADS * 2 * DIFF_HEAD_DIM
S5_GROUP = 16
S5_GROUPS = 64
S5_STATE = 64
X_HEADS = 4
X_HEAD_DIM = 256
N_EXPERTS = 32
TOP_K = 4
D_FF = 1024
SWIGLU_LIMIT = 7.0
SWIGLU_ALPHA = 1.702
MOE_BLOCK = 512
MOE_HALF = MOE_BLOCK // 2
MOE_TILE = 512
DEPTH = 2
ALPHA = (2 * DEPTH) ** 0.25
LN_EPS = 1e-5
NEG_INF = -1e30

LANES = 128
SUBLANES = 8
VMEM_LIMIT_BYTES = 56 * 1024 * 1024
ROW_TILE = D_MODEL // LANES
SEG_ROWS = 32

S5_CHUNK_GROUPS = 8
S5_CHUNK_IN = S5_CHUNK_GROUPS * S5_GROUP
S5_CHUNK_STATE = S5_CHUNK_GROUPS * S5_STATE
S5_N_CHUNKS = S5_GROUPS // S5_CHUNK_GROUPS
S5_STATES = S5_GROUPS * S5_STATE


def _params(*sem):
    return pltpu.CompilerParams(dimension_semantics=sem, vmem_limit_bytes=VMEM_LIMIT_BYTES)


def _gelu(x):
    return 0.5 * x * (1.0 + jnp.tanh(math.sqrt(2.0 / math.pi) * (x + 0.044715 * (x * x * x))))


def _sigmoid(x):
    return 1.0 / (1.0 + jnp.exp(-x))


def _layer_norm(z, g, b):
    mu = jnp.mean(z, axis=-1, keepdims=True)
    zc = z - mu
    var = jnp.mean(zc * zc, axis=-1, keepdims=True)
    return zc * lax.rsqrt(var + LN_EPS) * g + b


def _dot(a, b):
    return jnp.dot(a, b, preferred_element_type=F32)


def _dot_nt(a, b):
    return lax.dot_general(a, b, (((1,), (1,)), ((), ())), preferred_element_type=F32)


def _matmul_kernel(x_ref, w_ref, o_ref):
    o_ref[...] = _dot(x_ref[...].astype(BF16), w_ref[...]).astype(o_ref.dtype)


def _matmul(x, w, out_dtype, tm, tn):
    m, k = x.shape
    n = w.shape[1]
    return pl.pallas_call(
        _matmul_kernel,
        out_shape=jax.ShapeDtypeStruct((m, n), out_dtype),
        grid=(n // tn, m // tm),
        in_specs=[pl.BlockSpec((tm, k), lambda j, i: (i, 0)),
                  pl.BlockSpec((k, tn), lambda j, i: (0, j))],
        out_specs=pl.BlockSpec((tm, tn), lambda j, i: (i, j)),
        compiler_params=_params("parallel", "parallel"),
        name="matmul",
    )(x, w)


def _inproj_sgu_kernel(x_ref, w_ref, lng_ref, lnb_ref, wsp_ref, bsp_ref, a_ref, qk_ref, vt_ref, *, tm):
    xb = x_ref[...].astype(BF16)
    qk_ref[...] = _dot(xb, w_ref[:, 2 * SGU_WIDTH:2 * SGU_WIDTH + 2 * DIFF_QK_WIDTH]).astype(BF16)
    vt_ref[...] = _dot(xb, w_ref[:, 2 * SGU_WIDTH + 2 * DIFF_QK_WIDTH:]).T.astype(BF16)
    h = _dot(xb, w_ref[:, :2 * SGU_WIDTH])
    for g in range(SGU_GROUPS):
        lo = g * CHUNK
        u = _gelu(h[:, lo:lo + CHUNK])
        v = _gelu(h[:, SGU_WIDTH + lo:SGU_WIDTH + lo + CHUNK])
        vn = _layer_norm(v, lng_ref[g:g + 1, :], lnb_ref[g:g + 1, :]).astype(BF16)
        w_g = wsp_ref[g]
        b_g = bsp_ref[:, g:g + 1]
        for c in range(tm // CHUNK):
            r = c * CHUNK
            gate = _dot(w_g, vn[r:r + CHUNK, :]) + b_g
            a_ref[r:r + CHUNK, lo:lo + CHUNK] = (u[r:r + CHUNK, :] * gate).astype(BF16)


def _inproj_sgu(x2d, w_in, ln_g, ln_b, w_spatial, b_spatial, tm=512):
    n, d = x2d.shape
    w = w_in.astype(BF16)
    wsp = jnp.tril(w_spatial).astype(BF16)
    bsp_t = b_spatial.T
    v_width = w.shape[1] - 2 * SGU_WIDTH - 2 * DIFF_QK_WIDTH
    const = lambda i: (0, 0)
    return pl.pallas_call(
        functools.partial(_inproj_sgu_kernel, tm=tm),
        out_shape=(jax.ShapeDtypeStruct((n, SGU_WIDTH), BF16),
                   jax.ShapeDtypeStruct((n, 2 * DIFF_QK_WIDTH), BF16),
                   jax.ShapeDtypeStruct((v_width, n), BF16)),
        grid=(n // tm,),
        in_specs=[pl.BlockSpec((tm, d), lambda i: (i, 0)),
                  pl.BlockSpec(w.shape, const),
                  pl.BlockSpec((SGU_GROUPS, CHUNK), const),
                  pl.BlockSpec((SGU_GROUPS, CHUNK), const),
                  pl.BlockSpec(wsp.shape, lambda i: (0, 0, 0)),
                  pl.BlockSpec(bsp_t.shape, const)],
        out_specs=(pl.BlockSpec((tm, SGU_WIDTH), lambda i: (i, 0)),
                   pl.BlockSpec((tm, 2 * DIFF_QK_WIDTH), lambda i: (i, 0)),
                   pl.BlockSpec((v_width, tm), lambda i: (0, i))),
        compiler_params=_params("parallel"),
        name="inproj_sgu",
    )(x2d, w, ln_g.reshape(SGU_GROUPS, CHUNK), ln_b.reshape(SGU_GROUPS, CHUNK), wsp, bsp_t)


def _diff_attn_kernel(q_ref, k_ref, vt_ref, lam_ref, g_ref, o_ref, *, tq, lam_init):
    i = pl.program_id(1)
    hw = 2 * DIFF_HEAD_DIM
    lane = lax.broadcasted_iota(jnp.int32, (1, hw), 1)
    queries = []
    for h in range(DIFF_HEADS):
        q = q_ref[:, h * hw:(h + 1) * hw] * jnp.asarray(DIFF_HEAD_DIM ** -0.5, BF16)
        zero = jnp.zeros_like(q)
        queries.append((h, jnp.where(lane < DIFF_HEAD_DIM, q, zero)))
        queries.append((h, jnp.where(lane >= DIFF_HEAD_DIM, q, zero)))

    def block(j, carry, masked):
        off = pl.multiple_of(j * tq, tq)
        scores = [_dot_nt(k_ref[pl.ds(off, tq), h * hw:(h + 1) * hw], qc) for h, qc in queries]
        if masked:
            key = lax.broadcasted_iota(jnp.int32, (tq, tq), 0)
            qry = lax.broadcasted_iota(jnp.int32, (tq, tq), 1)
            scores = [jnp.where(key <= qry, st, NEG_INF) for st in scores]
        stats, probs = [], []
        for (m, l, _), st in zip(carry, scores):
            m_new = jnp.maximum(m, jnp.max(st, axis=0, keepdims=True))
            alpha = jnp.exp(m - m_new)
            p = jnp.exp(st - m_new)
            stats.append((m_new, alpha * l + jnp.sum(p, axis=0, keepdims=True), alpha))
            probs.append(p.astype(BF16))
        pv = [_dot(vt_ref[h * DIFF_V_DIM:(h + 1) * DIFF_V_DIM, pl.ds(off, tq)], p)
              for (h, _), p in zip(queries, probs)]
        return tuple((m_new, l_new, alpha * acc + o)
                     for (m_new, l_new, alpha), (_, _, acc), o in zip(stats, carry, pv))

    init = (jnp.full((1, tq), NEG_INF, F32), jnp.zeros((1, tq), F32), jnp.zeros((DIFF_V_DIM, tq), F32))
    carry = lax.fori_loop(0, i, lambda j, c: block(j, c, False), (init,) * len(queries))
    final = block(i, carry, True)

    lam_v = lam_ref[...]
    s_a = jnp.sum(lam_v[0:1, :] * lam_v[1:2, :], axis=-1, keepdims=True)
    s_b = jnp.sum(lam_v[2:3, :] * lam_v[3:4, :], axis=-1, keepdims=True)
    lam = jnp.exp(s_a) - jnp.exp(s_b) + lam_init
    for h in range(DIFF_HEADS):
        (_, l1, a1), (_, l2, a2) = final[2 * h], final[2 * h + 1]
        o_t = a1 / l1 - lam * (a2 / l2)
        o_t = o_t * lax.rsqrt(jnp.mean(o_t * o_t, axis=0, keepdims=True) + LN_EPS) * g_ref[...] * (1.0 - lam_init)
        o_ref[:, h * DIFF_V_DIM:(h + 1) * DIFF_V_DIM] = o_t.T.astype(BF16)


def _diff_attention(qk, vt, lam_q1, lam_k1, lam_q2, lam_k2, subln_g, lam_init, batch, seq, tq=256):
    n = qk.shape[0]
    nq = seq // tq
    lam_v = jnp.stack([lam_q1, lam_k1, lam_q2, lam_k2])
    v_width = DIFF_HEADS * DIFF_V_DIM
    return pl.pallas_call(
        functools.partial(_diff_attn_kernel, tq=tq, lam_init=lam_init),
        out_shape=jax.ShapeDtypeStruct((n, v_width), BF16),
        grid=(batch, nq),
        in_specs=[pl.BlockSpec((tq, DIFF_QK_WIDTH), lambda b, i: (b * nq + i, 0)),
                  pl.BlockSpec((seq, DIFF_QK_WIDTH), lambda b, i: (b, 1)),
                  pl.BlockSpec((v_width, seq), lambda b, i: (0, b)),
                  pl.BlockSpec(lam_v.shape, lambda b, i: (0, 0)),
                  pl.BlockSpec((DIFF_V_DIM, 1), lambda b, i: (0, 0))],
        out_specs=pl.BlockSpec((tq, v_width), lambda b, i: (b * nq + i, 0)),
        compiler_params=_params("parallel", "parallel"),
        name="diff_attention",
    )(qk, qk, vt, lam_v, subln_g.reshape(DIFF_V_DIM, 1))


def _outproj_ln_kernel(a_ref, d_ref, w_ref, x_ref, g_ref, b_ref, o_ref):
    y = _dot(a_ref[...], w_ref[:SGU_WIDTH, :]) + _dot(d_ref[...], w_ref[SGU_WIDTH:, :])
    o_ref[...] = _layer_norm(ALPHA * x_ref[...] + y, g_ref[...], b_ref[...])


def _outproj_ln(a, dattn, w_out, x2d, ln_g, ln_b, tm=512):
    n, d = x2d.shape
    w = w_out.astype(BF16)
    const = lambda i: (0, 0)
    row = lambda i: (i, 0)
    return pl.pallas_call(
        _outproj_ln_kernel,
        out_shape=jax.ShapeDtypeStruct((n, d), F32),
        grid=(n // tm,),
        in_specs=[pl.BlockSpec((tm, a.shape[1]), row),
                  pl.BlockSpec((tm, dattn.shape[1]), row),
                  pl.BlockSpec(w.shape, const),
                  pl.BlockSpec((tm, d), row),
                  pl.BlockSpec((1, d), const),
                  pl.BlockSpec((1, d), const)],
        out_specs=pl.BlockSpec((tm, d), row),
        compiler_params=_params("parallel"),
        name="outproj_ln1",
    )(a, dattn, w, x2d, ln_g.reshape(1, d), ln_b.reshape(1, d))


def _cross_router_kernel(x_ref, wq_ref, wo_ref, k_ref, v_ref, g_ref, b_ref, rwh_ref, rwl_ref, rb_ref,
                         x2_ref, gate_ref, pos_ref, cnt_ref, *, tile, tiles_per_step):
    for t in range(tiles_per_step):
        _cross_router_tile(x_ref, wq_ref, wo_ref, k_ref, v_ref, g_ref, b_ref, rwh_ref, rwl_ref, rb_ref,
                           x2_ref, gate_ref, pos_ref, cnt_ref, slice(t * tile, (t + 1) * tile),
                           slice(t * SUBLANES, (t + 1) * SUBLANES))


def _cross_router_tile(x_ref, wq_ref, wo_ref, k_ref, v_ref, g_ref, b_ref, rwh_ref, rwl_ref, rb_ref,
                       x2_ref, gate_ref, pos_ref, cnt_ref, rows, cnt_rows):
    x = x_ref[rows, :]
    q = (_dot(x.astype(BF16), wq_ref[...]) * (X_HEAD_DIM ** -0.5)).astype(BF16)
    heads = []
    for h in range(X_HEADS):
        lo = h * X_HEAD_DIM
        s = _dot_nt(q[:, lo:lo + X_HEAD_DIM], k_ref[:, lo:lo + X_HEAD_DIM])
        p = jnp.exp(s - jnp.max(s, axis=-1, keepdims=True))
        p = p / jnp.sum(p, axis=-1, keepdims=True)
        heads.append(_dot(p.astype(BF16), v_ref[:, lo:lo + X_HEAD_DIM]).astype(BF16))
    o = jnp.concatenate(heads, axis=-1)
    x2 = _layer_norm(ALPHA * x + _dot(o, wo_ref[...]), g_ref[...], b_ref[...])
    x2_ref[rows, :] = x2

    x_hi = x2.astype(BF16)
    x_lo = (x2 - x_hi.astype(F32)).astype(BF16)
    logits = _dot(x_hi, rwh_ref[...]) + _dot(x_lo, rwh_ref[...]) + _dot(x_hi, rwl_ref[...]) + rb_ref[...]
    lane = lax.broadcasted_iota(jnp.int32, logits.shape, 1).astype(F32)
    work = logits
    top_v, top_i = [], []
    for _ in range(TOP_K):
        m = jnp.max(work, axis=-1, keepdims=True)
        sel = jnp.min(jnp.where(work == m, lane, float(LANES)), axis=-1, keepdims=True)
        top_v.append(m)
        top_i.append(sel)
        work = jnp.where(lane == sel, -jnp.inf, work)
    e = [jnp.exp(v - top_v[0]) for v in top_v]
    denom = e[0] + e[1] + e[2] + e[3]
    gate_out = jnp.zeros(logits.shape, F32)
    for k in range(TOP_K):
        gate_out = jnp.where(lane == float(k), e[k] / denom, gate_out)
    gate_ref[rows, :] = gate_out

    tm = logits.shape[0]
    onehot = jnp.zeros(logits.shape, F32)
    for k in range(TOP_K):
        onehot = jnp.where(lane == top_i[k], 1.0, onehot)
    total = jnp.sum(onehot, axis=0, keepdims=True)
    segments = jnp.floor((total + (SEG_ROWS - 1)) * (1.0 / SEG_ROWS))
    e_r = lax.broadcasted_iota(jnp.int32, (LANES, LANES), 0)
    e_c = lax.broadcasted_iota(jnp.int32, (LANES, LANES), 1)
    before = jnp.where(e_r < e_c, 1.0, 0.0).astype(BF16)
    run_start = SEG_ROWS * _dot(jnp.broadcast_to(segments, (SUBLANES, LANES)).astype(BF16), before)[0:1, :]
    r_i = lax.broadcasted_iota(jnp.int32, (tm, tm), 0)
    c_i = lax.broadcasted_iota(jnp.int32, (tm, tm), 1)
    earlier = jnp.where(c_i < r_i, 1.0, 0.0).astype(BF16)
    pos = _dot(earlier, onehot.astype(BF16)) + run_start
    pos_out = jnp.zeros(logits.shape, F32)
    for k in range(TOP_K):
        pk = jnp.sum(jnp.where(lane == top_i[k], pos, 0.0), axis=-1, keepdims=True)
        pos_out = jnp.where(lane == float(k), pk, pos_out)
    pos_ref[rows, :] = (pos_out * ROW_TILE).astype(jnp.int32)
    sub = lax.broadcasted_iota(jnp.int32, (SUBLANES, LANES), 0)
    cnt_ref[cnt_rows, :] = jnp.where(sub == 1, run_start, total).astype(jnp.int32)


def _cross_router(x1, kv, w_q, w_o, ln_g, ln_b, router_w, router_b, batch, seq, mem_len, tile=MOE_TILE,
                  tiles_per_step=1):
    n, d = x1.shape
    tm = tile * tiles_per_step
    nt = seq // tm
    pad = LANES - N_EXPERTS
    rw = jnp.pad(router_w, ((0, 0), (0, pad)))
    rw_hi = rw.astype(BF16)
    rw_lo = (rw - rw_hi.astype(F32)).astype(BF16)
    rb = jnp.pad(router_b, (0, pad), constant_values=-jnp.inf).reshape(1, LANES)
    const = lambda b, i: (0, 0)
    row = lambda b, i: (b * nt + i, 0)
    return pl.pallas_call(
        functools.partial(_cross_router_kernel, tile=tile, tiles_per_step=tiles_per_step),
        out_shape=(jax.ShapeDtypeStruct((n, d), F32),
                   jax.ShapeDtypeStruct((n, LANES), F32),
                   jax.ShapeDtypeStruct((n, LANES), jnp.int32),
                   jax.ShapeDtypeStruct((n // tile * SUBLANES, LANES), jnp.int32)),
        grid=(batch, nt),
        in_specs=[pl.BlockSpec((tm, d), row),
                  pl.BlockSpec((d, d), const),
                  pl.BlockSpec((d, d), const),
                  pl.BlockSpec((mem_len, d), lambda b, i: (b, 0)),
                  pl.BlockSpec((mem_len, d), lambda b, i: (b, 1)),
                  pl.BlockSpec((1, d), const),
                  pl.BlockSpec((1, d), const),
                  pl.BlockSpec((d, LANES), const),
                  pl.BlockSpec((d, LANES), const),
                  pl.BlockSpec((1, LANES), const)],
        out_specs=(pl.BlockSpec((tm, d), row),
                   pl.BlockSpec((tm, LANES), row),
                   pl.BlockSpec((tm, LANES), row),
                   pl.BlockSpec((tiles_per_step * SUBLANES, LANES), row)),
        compiler_params=_params("parallel", "parallel"),
        name="cross_attn_router",
    )(x1, w_q.astype(BF16), w_o.astype(BF16), kv, kv, ln_g.reshape(1, d), ln_b.reshape(1, d), rw_hi, rw_lo, rb)


def _to_row_tiles(dst_ref, x, rows, first_row=0):
    for c in range(ROW_TILE):
        dst_ref[pl.ds(first_row * ROW_TILE + c, rows, stride=ROW_TILE), :] = x[:, c * LANES:(c + 1) * LANES]


def _from_row_tiles(src_ref, rows, first_row=0):
    return jnp.concatenate([src_ref[pl.ds(first_row * ROW_TILE + c, rows, stride=ROW_TILE), :]
                            for c in range(ROW_TILE)], axis=-1)


def _tile_rows(row):
    return pl.multiple_of(row * ROW_TILE, ROW_TILE)


def _num_segments(count):
    return lax.shift_right_logical(count + (SEG_ROWS - 1), SEG_ROWS.bit_length() - 1)


def _tile_segments(tcnt_ref):
    return lax.fori_loop(0, N_EXPERTS, lambda e, t: t + _num_segments(tcnt_ref[0, 0, e]), 0)


def _dispatch_kernel(pstart_ref, counts_ref, padded_ref, ldest_ref, tcnt_ref, tbase_ref, lstart_ref, x_ref,
                     xs_hbm, xrt, stage, zeros, nseg, sems, zsem, rsem, *, tm, nt):
    i = pl.program_id(0)
    slot = lax.rem(i, 2)

    @pl.when(i == 0)
    def _():
        stage[...] = jnp.zeros_like(stage)
        zeros[...] = jnp.zeros_like(zeros)

    _to_row_tiles(xrt, x_ref[...], tm)
    st = stage.at[slot]
    for k in range(TOP_K):
        def move(r, c, k=k):
            dst = pl.multiple_of(ldest_ref[0, 0, k * tm + r], ROW_TILE)
            st[pl.ds(dst, ROW_TILE), :] = xrt[pl.ds(_tile_rows(r), ROW_TILE), :]
            return c

        lax.fori_loop(0, tm, move, 0, unroll=8)

    def seg_copy(s, src_row, dst_row):
        return pltpu.make_async_copy(stage.at[s, pl.ds(_tile_rows(src_row), SEG_ROWS * ROW_TILE), :],
                                     xs_hbm.at[pl.ds(_tile_rows(dst_row), SEG_ROWS * ROW_TILE), :], sems.at[s])

    def wait_segments(s, count):
        def wait(j, c):
            seg_copy(s, 0, 0).wait()
            return c

        lax.fori_loop(0, count, wait, 0)

    @pl.when(i > 0)
    def _():
        wait_segments(1 - slot, nseg[1 - slot])

    def issue(e, total):
        n = _num_segments(tcnt_ref[0, 0, e])
        src0 = lstart_ref[0, 0, e]
        dst0 = pstart_ref[e] + tbase_ref[0, 0, e]

        def one(j, c):
            seg_copy(slot, src0 + j * SEG_ROWS, dst0 + j * SEG_ROWS).start()
            return c

        lax.fori_loop(0, n, one, 0)
        return total + n

    total = lax.fori_loop(0, N_EXPERTS, issue, 0)
    nseg[slot] = total

    @pl.when(i == nt - 1)
    def _():
        wait_segments(slot, total)

        def gap(e):
            lo = pstart_ref[e] + counts_ref[e]
            width = padded_ref[e] - counts_ref[e]
            full = lax.shift_right_logical(width, SEG_ROWS.bit_length() - 1)
            return lo, full, width - full * SEG_ROWS

        def seg_zero(row):
            return pltpu.make_async_copy(zeros, xs_hbm.at[pl.ds(_tile_rows(row), SEG_ROWS * ROW_TILE), :], zsem)

        def row_zero(row):
            return pltpu.make_async_copy(zeros.at[pl.ds(0, ROW_TILE), :],
                                         xs_hbm.at[pl.ds(_tile_rows(row), ROW_TILE), :], rsem)

        def fill(e, c):
            lo, full, rest = gap(e)
            lax.fori_loop(0, full, lambda j, c2: (seg_zero(lo + j * SEG_ROWS).start(), c2)[1], 0)
            lax.fori_loop(0, rest, lambda j, c2: (row_zero(lo + full * SEG_ROWS + j).start(), c2)[1], 0)
            return c

        def fill_wait(e, c):
            _, full, rest = gap(e)
            lax.fori_loop(0, full, lambda j, c2: (seg_zero(0).wait(), c2)[1], 0)
            lax.fori_loop(0, rest, lambda j, c2: (row_zero(0).wait(), c2)[1], 0)
            return c

        lax.fori_loop(0, N_EXPERTS, fill, 0)
        lax.fori_loop(0, N_EXPERTS, fill_wait, 0)


def _stage_rows(tm):
    return TOP_K * tm + N_EXPERTS * SEG_ROWS


def _dispatch(x2, pstart, counts, padded, ldest_t, tcnt_t, tbase_t, lstart_t, n_pad, tm):
    n, d = x2.shape
    nt = n // tm
    smem_tile = lambda width: pl.BlockSpec((1, 1, width), lambda i, *_: (i, 0, 0), memory_space=pltpu.SMEM)
    grid_spec = pltpu.PrefetchScalarGridSpec(
        num_scalar_prefetch=3,
        grid=(nt,),
        in_specs=[smem_tile(TOP_K * tm), smem_tile(LANES), smem_tile(LANES), smem_tile(LANES),
                  pl.BlockSpec((tm, d), lambda i, *_: (i, 0))],
        out_specs=pl.BlockSpec(memory_space=pl.ANY),
        scratch_shapes=[pltpu.VMEM((tm * ROW_TILE, LANES), F32),
                        pltpu.VMEM((2, _stage_rows(tm) * ROW_TILE, LANES), F32),
                        pltpu.VMEM((SEG_ROWS * ROW_TILE, LANES), F32),
                        pltpu.SMEM((2,), jnp.int32),
                        pltpu.SemaphoreType.DMA((2,)),
                        pltpu.SemaphoreType.DMA,
                        pltpu.SemaphoreType.DMA],
    )
    return pl.pallas_call(
        functools.partial(_dispatch_kernel, tm=tm, nt=nt),
        out_shape=jax.ShapeDtypeStruct((n_pad * ROW_TILE, LANES), F32),
        grid_spec=grid_spec,
        compiler_params=_params("arbitrary"),
        name="moe_dispatch",
    )(pstart, counts, padded, ldest_t, tcnt_t, tbase_t, lstart_t, x2)


def _expert_kernel(be_ref, valid_ref, xs_ref, wup_ref, bup_ref, wdn_ref, bdn_ref, y_ref, wup_bf, wdn_bf):
    i = pl.program_id(0)

    @pl.when(valid_ref[i] == 0)
    def _():
        y_ref[...] = jnp.zeros_like(y_ref)

    @pl.when(valid_ref[i] > 0)
    def _():
        prev = be_ref[jnp.maximum(i - 1, 0)]

        @pl.when(jnp.logical_or(i == 0, be_ref[i] != prev))
        def _():
            wup_bf[...] = wup_ref[0].astype(BF16)
            wdn_bf[...] = wdn_ref[0].astype(BF16)

        def expert_rows(first_row):
            xb = _from_row_tiles(xs_ref, MOE_HALF, first_row).astype(BF16)
            h = _dot(xb, wup_bf[...]) + bup_ref[0]
            glu = jnp.minimum(h[:, :D_FF], SWIGLU_LIMIT)
            lin = jnp.clip(h[:, D_FF:], -SWIGLU_LIMIT, SWIGLU_LIMIT)
            act = glu * _sigmoid(SWIGLU_ALPHA * glu) * (lin + 1.0)
            _to_row_tiles(y_ref, _dot(act.astype(BF16), wdn_bf[...]) + bdn_ref[0], MOE_HALF, first_row)

        @pl.when(valid_ref[i] > MOE_HALF)
        def _():
            expert_rows(0)
            expert_rows(MOE_HALF)

        @pl.when(valid_ref[i] <= MOE_HALF)
        def _():
            expert_rows(0)
            y_ref[MOE_HALF * ROW_TILE:, :] = jnp.zeros((MOE_HALF * ROW_TILE, LANES), F32)


def _experts(xs, blk_expert, blk_valid, w_up, b_up, w_down, b_down):
    n_blocks = blk_expert.shape[0]
    d = w_up.shape[1]
    f2 = w_up.shape[2]
    blk_rows = MOE_BLOCK * ROW_TILE
    grid_spec = pltpu.PrefetchScalarGridSpec(
        num_scalar_prefetch=2,
        grid=(n_blocks,),
        in_specs=[pl.BlockSpec((blk_rows, LANES), lambda i, be, va: (jnp.where(va[i] > 0, i, 0), 0)),
                  pl.BlockSpec((1, d, f2), lambda i, be, va: (be[i], 0, 0)),
                  pl.BlockSpec((1, 1, f2), lambda i, be, va: (be[i], 0, 0)),
                  pl.BlockSpec((1, D_FF, d), lambda i, be, va: (be[i], 0, 0)),
                  pl.BlockSpec((1, 1, d), lambda i, be, va: (be[i], 0, 0))],
        out_specs=pl.BlockSpec((blk_rows, LANES), lambda i, be, va: (i, 0)),
        scratch_shapes=[pltpu.VMEM((d, f2), BF16),
                        pltpu.VMEM((D_FF, d), BF16)],
    )
    return pl.pallas_call(
        _expert_kernel,
        out_shape=jax.ShapeDtypeStruct((n_blocks * blk_rows, LANES), F32),
        grid_spec=grid_spec,
        compiler_params=_params("arbitrary"),
        name="moe_experts",
    )(blk_expert, blk_valid, xs, w_up, b_up.reshape(N_EXPERTS, 1, f2), w_down, b_down.reshape(N_EXPERTS, 1, d))


def _combine_ln_kernel(pstart_ref, ldest_ref, gate_ref, tcnt_ref, tbase_ref, lstart_ref,
                       tcnt_next_ref, tbase_next_ref, lstart_next_ref, ys_hbm, x_ref, g_ref, b_ref,
                       o_ref, stage, yrt, sems, *, tm, nt):
    i = pl.program_id(0)
    slot = lax.rem(i, 2)

    def seg_copy(s, src_row, dst_row):
        return pltpu.make_async_copy(ys_hbm.at[pl.ds(_tile_rows(src_row), SEG_ROWS * ROW_TILE), :],
                                     stage.at[s, pl.ds(_tile_rows(dst_row), SEG_ROWS * ROW_TILE), :], sems.at[s])

    def fetch(s, tcnt, tbase, lstart):
        def per_expert(e, c):
            src0 = pstart_ref[e] + tbase[0, 0, e]
            dst0 = lstart[0, 0, e]

            def one(j, c2):
                seg_copy(s, src0 + j * SEG_ROWS, dst0 + j * SEG_ROWS).start()
                return c2

            lax.fori_loop(0, _num_segments(tcnt[0, 0, e]), one, 0)
            return c

        lax.fori_loop(0, N_EXPERTS, per_expert, 0)

    @pl.when(i == 0)
    def _():
        fetch(0, tcnt_ref, tbase_ref, lstart_ref)

    @pl.when(i + 1 < nt)
    def _():
        fetch(1 - slot, tcnt_next_ref, tbase_next_ref, lstart_next_ref)

    def wait(j, c):
        seg_copy(slot, 0, 0).wait()
        return c

    lax.fori_loop(0, _tile_segments(tcnt_ref), wait, 0)

    st = stage.at[slot]

    def reduce(r, c):
        acc = None
        for k in range(TOP_K):
            row = st[pl.ds(pl.multiple_of(ldest_ref[0, 0, k * tm + r], ROW_TILE), ROW_TILE), :]
            term = gate_ref[0, 0, k * tm + r] * row
            acc = term if acc is None else acc + term
        yrt[pl.ds(_tile_rows(r), ROW_TILE), :] = acc
        return c

    lax.fori_loop(0, tm, reduce, 0, unroll=8)
    y = _from_row_tiles(yrt, tm)
    o_ref[...] = _layer_norm(ALPHA * x_ref[...] + y, g_ref[...], b_ref[...])


def _combine_ln(ys, pstart, ldest_t, gates_t, tcnt_t, tbase_t, lstart_t, x2, ln_g, ln_b, tm):
    n, d = x2.shape
    nt = n // tm
    cur = lambda width: pl.BlockSpec((1, 1, width), lambda i, *_: (i, 0, 0), memory_space=pltpu.SMEM)
    nxt = lambda width: pl.BlockSpec((1, 1, width), lambda i, *_: (jnp.minimum(i + 1, nt - 1), 0, 0),
                                     memory_space=pltpu.SMEM)
    grid_spec = pltpu.PrefetchScalarGridSpec(
        num_scalar_prefetch=1,
        grid=(nt,),
        in_specs=[cur(TOP_K * tm), cur(TOP_K * tm), cur(LANES), cur(LANES), cur(LANES),
                  nxt(LANES), nxt(LANES), nxt(LANES),
                  pl.BlockSpec(memory_space=pl.ANY),
                  pl.BlockSpec((tm, d), lambda i, *_: (i, 0)),
                  pl.BlockSpec((1, d), lambda i, *_: (0, 0)),
                  pl.BlockSpec((1, d), lambda i, *_: (0, 0))],
        out_specs=pl.BlockSpec((tm, d), lambda i, *_: (i, 0)),
        scratch_shapes=[pltpu.VMEM((2, _stage_rows(tm) * ROW_TILE, LANES), F32),
                        pltpu.VMEM((tm * ROW_TILE, LANES), F32),
                        pltpu.SemaphoreType.DMA((2,))],
    )
    return pl.pallas_call(
        functools.partial(_combine_ln_kernel, tm=tm, nt=nt),
        out_shape=jax.ShapeDtypeStruct((n, d), F32),
        grid_spec=grid_spec,
        compiler_params=_params("arbitrary"),
        name="moe_combine_ln3",
    )(pstart, ldest_t, gates_t, tcnt_t, tbase_t, lstart_t, tcnt_t, tbase_t, lstart_t, ys, x2,
      ln_g.reshape(1, d), ln_b.reshape(1, d))


def _round_up(v, m):
    return (v + m - 1) // m * m


def _moe_layer(x2, pos_pad, tile_pad, gates_pad, w_up, b_up, w_down, b_down, ln_g, ln_b, tm=MOE_TILE):
    n = x2.shape[0]
    nt = n // tm
    i32 = jnp.int32
    tile_info = tile_pad.reshape(nt, SUBLANES, LANES)
    tcnt = tile_info[:, 0, :N_EXPERTS]
    lstart = tile_info[:, 1, :N_EXPERTS]
    tbase = jnp.cumsum(tcnt, axis=0) - tcnt
    counts = jnp.sum(tcnt, axis=0)
    padded = _round_up(counts + SEG_ROWS, MOE_BLOCK)
    pend = jnp.cumsum(padded)
    pstart = pend - padded
    n_blocks = -(-(n * TOP_K + N_EXPERTS * (SEG_ROWS + MOE_BLOCK - 1)) // MOE_BLOCK)
    block_start = jnp.arange(n_blocks, dtype=i32) * MOE_BLOCK
    blk_expert = jnp.minimum(jnp.sum(pend[None, :] <= block_start[:, None], axis=1), N_EXPERTS - 1).astype(i32)
    blk_valid = jnp.clip(counts[blk_expert] - (block_start - pstart[blk_expert]), 0, MOE_BLOCK).astype(i32)
    per_tile = lambda a: a[:, :TOP_K].reshape(nt, tm, TOP_K).transpose(0, 2, 1).reshape(nt, 1, TOP_K * tm)
    ldest_t = per_tile(pos_pad)
    gates_t = per_tile(gates_pad)
    lane_pad = lambda a: jnp.pad(a.astype(i32), ((0, 0), (0, LANES - N_EXPERTS))).reshape(nt, 1, LANES)
    tcnt_t, tbase_t, lstart_t = lane_pad(tcnt), lane_pad(tbase), lane_pad(lstart)
    pstart, counts, padded = pstart.astype(i32), counts.astype(i32), padded.astype(i32)

    xs = _dispatch(x2, pstart, counts, padded, ldest_t, tcnt_t, tbase_t, lstart_t, n_blocks * MOE_BLOCK, tm)
    ys = _experts(xs, blk_expert, blk_valid, w_up, b_up, w_down, b_down)
    return _combine_ln(ys, pstart, ldest_t, gates_t, tcnt_t, tbase_t, lstart_t, x2, ln_g, ln_b, tm)


def _s5_kernel(x_ref, win_ref, bblk_ref, ar_ref, ai_ref, cblk_ref, dsk_ref, wval_ref, wgate_ref, g_ref, b_ref,
               o_ref, bur, bui, sr, si, *, tt, batch):
    rows = tt * batch

    @pl.when(pl.program_id(0) == 0)
    def _():
        sr[...] = jnp.zeros_like(sr)
        si[...] = jnp.zeros_like(si)

    x = x_ref[...]
    u = _dot(x.astype(BF16), win_ref[...])
    ub = u.astype(BF16)
    for c in range(S5_N_CHUNKS):
        bu = _dot(ub[:, c * S5_CHUNK_IN:(c + 1) * S5_CHUNK_IN], bblk_ref[c])
        bur[:, c * S5_CHUNK_STATE:(c + 1) * S5_CHUNK_STATE] = bu[:, :S5_CHUNK_STATE]
        bui[:, c * S5_CHUNK_STATE:(c + 1) * S5_CHUNK_STATE] = bu[:, S5_CHUNK_STATE:]

    for c in range(S5_N_CHUNKS):
        cols = pl.ds(c * S5_CHUNK_STATE, S5_CHUNK_STATE)
        a_r = ar_ref[:, cols]
        a_i = ai_ref[:, cols]

        def step(t, carry):
            s_r, s_i = carry
            rsl = pl.ds(pl.multiple_of(t * batch, batch), batch)
            n_r = a_r * s_r - a_i * s_i + bur[rsl, cols]
            n_i = a_r * s_i + a_i * s_r + bui[rsl, cols]
            bur[rsl, cols] = n_r
            bui[rsl, cols] = n_i
            return n_r, n_i

        f_r, f_i = lax.fori_loop(0, tt, step, (sr[:, cols], si[:, cols]), unroll=True)
        sr[:, cols] = f_r
        si[:, cols] = f_i

    ys = []
    for c in range(S5_N_CHUNKS):
        cols = pl.ds(c * S5_CHUNK_STATE, S5_CHUNK_STATE)
        xri = jnp.concatenate([bur[:, cols].astype(BF16), bui[:, cols].astype(BF16)], axis=-1)
        ys.append(_dot(xri, cblk_ref[c]))
    y = jnp.concatenate(ys, axis=-1) + dsk_ref[...] * u
    yb = _gelu(y).astype(BF16)
    hmix = _dot(yb, wval_ref[...]) * _sigmoid(_dot(yb, wgate_ref[...]))
    o_ref[...] = _layer_norm(ALPHA * x + hmix, g_ref[...], b_ref[...])


def _s5_discretize(log_dt, lambda_re, lambda_im, b_re, b_im, c_re, c_im):
    dt = jnp.exp(log_dt)[:, None]
    mag = jnp.exp(lambda_re * dt)
    ar = mag * jnp.cos(lambda_im * dt)
    ai = mag * jnp.sin(lambda_im * dt)
    den = lambda_re * lambda_re + lambda_im * lambda_im
    zr = ((ar - 1.0) * lambda_re + ai * lambda_im) / den
    zi = (ai * lambda_re - (ar - 1.0) * lambda_im) / den
    bbar_re = zr[..., None] * b_re - zi[..., None] * b_im
    bbar_im = zr[..., None] * b_im + zi[..., None] * b_re
    ng, gc, p, gw = S5_N_CHUNKS, S5_CHUNK_GROUPS, S5_STATE, S5_GROUP
    eye = jnp.eye(gc, dtype=F32)
    bre = bbar_re.reshape(ng, gc, p, gw).transpose(0, 1, 3, 2)
    bim = bbar_im.reshape(ng, gc, p, gw).transpose(0, 1, 3, 2)
    blk_re = jnp.einsum('cgip,gh->cgihp', bre, eye).reshape(ng, gc * gw, gc * p)
    blk_im = jnp.einsum('cgip,gh->cgihp', bim, eye).reshape(ng, gc * gw, gc * p)
    bblk = jnp.concatenate([blk_re, blk_im], axis=-1).astype(BF16)
    cre = c_re.reshape(ng, gc, gw, p).transpose(0, 1, 3, 2)
    cim = c_im.reshape(ng, gc, gw, p).transpose(0, 1, 3, 2)
    cblk_re = jnp.einsum('cgpi,gh->cgphi', cre, eye).reshape(ng, gc * p, gc * gw)
    cblk_im = jnp.einsum('cgpi,gh->cgphi', cim, eye).reshape(ng, gc * p, gc * gw)
    cblk = jnp.concatenate([cblk_re, -cblk_im], axis=1).astype(BF16)
    return bblk, ar.reshape(1, S5_STATES), ai.reshape(1, S5_STATES), cblk


def _s5_mixer_ln(x_tm, w_in, log_dt, lambda_re, lambda_im, b_re, b_im, c_re, c_im, d_skip, w_val, w_gate,
                 ln_g, ln_b, batch, tt=16):
    n, d = x_tm.shape
    rows = tt * batch
    bblk, ar, ai, cblk = _s5_discretize(log_dt, lambda_re, lambda_im, b_re, b_im, c_re, c_im)
    const2 = lambda i: (0, 0)
    const3 = lambda i: (0, 0, 0)
    row = lambda i: (i, 0)
    return pl.pallas_call(
        functools.partial(_s5_kernel, tt=tt, batch=batch),
        out_shape=jax.ShapeDtypeStruct((n, d), F32),
        grid=(n // rows,),
        in_specs=[pl.BlockSpec((rows, d), row),
                  pl.BlockSpec((d, d), const2),
                  pl.BlockSpec(bblk.shape, const3),
                  pl.BlockSpec(ar.shape, const2),
                  pl.BlockSpec(ai.shape, const2),
                  pl.BlockSpec(cblk.shape, const3),
                  pl.BlockSpec((1, d), const2),
                  pl.BlockSpec((d, d), const2),
                  pl.BlockSpec((d, d), const2),
                  pl.BlockSpec((1, d), const2),
                  pl.BlockSpec((1, d), const2)],
        out_specs=pl.BlockSpec((rows, d), row),
        scratch_shapes=[pltpu.VMEM((rows, S5_STATES), F32),
                        pltpu.VMEM((rows, S5_STATES), F32),
                        pltpu.VMEM((batch, S5_STATES), F32),
                        pltpu.VMEM((batch, S5_STATES), F32)],
        compiler_params=_params("arbitrary"),
        name="s5_mixer_ln1",
    )(x_tm, w_in.astype(BF16), bblk, ar, ai, cblk, d_skip.reshape(1, d), w_val.astype(BF16), w_gate.astype(BF16),
      ln_g.reshape(1, d), ln_b.reshape(1, d))


def _diff_lambda_init(layer_idx):
    return 0.8 - 0.6 * math.exp(-0.3 * layer_idx)


def kernel(x, mem, w_mem_kv, l0_w_in, l0_sgu_ln_g, l0_sgu_ln_b, l0_w_spatial, l0_b_spatial, l0_lam_q1, l0_lam_k1, l0_lam_q2, l0_lam_k2, l0_subln_g, l0_w_out, l0_ln1_g, l0_ln1_b, l0_xq, l0_xo, l0_ln2_g, l0_ln2_b, l0_router_w, l0_router_b, l0_exp_w_up, l0_exp_b_up, l0_exp_w_down, l0_exp_b_down, l0_ln3_g, l0_ln3_b, l1_w_in, l1_log_dt, l1_lambda_re, l1_lambda_im, l1_b_re, l1_b_im, l1_c_re, l1_c_im, l1_d_skip, l1_w_val, l1_w_gate, l1_ln1_g, l1_ln1_b, l1_xq, l1_xo, l1_ln2_g, l1_ln2_b, l1_router_w, l1_router_b, l1_exp_w_up, l1_exp_b_up, l1_exp_w_down, l1_exp_b_down, l1_ln3_g, l1_ln3_b):
    batch, seq, d = x.shape
    mem_len = mem.shape[1]
    n = batch * seq
    x0 = x.reshape(n, d)

    kv = _matmul(mem.reshape(batch * mem_len, d), w_mem_kv.astype(BF16), BF16,
                 tm=min(512, batch * mem_len), tn=d)

    a, qk, vt = _inproj_sgu(x0, l0_w_in, l0_sgu_ln_g, l0_sgu_ln_b, l0_w_spatial, l0_b_spatial)
    dattn = _diff_attention(qk, vt, l0_lam_q1, l0_lam_k1, l0_lam_q2, l0_lam_k2, l0_subln_g,
                            _diff_lambda_init(0), batch, seq)
    x1 = _outproj_ln(a, dattn, l0_w_out, x0, l0_ln1_g, l0_ln1_b)
    x2, gates, pos, tile_info = _cross_router(x1, kv, l0_xq, l0_xo, l0_ln2_g, l0_ln2_b, l0_router_w, l0_router_b,
                                              batch, seq, mem_len)
    x3 = _moe_layer(x2, pos, tile_info, gates, l0_exp_w_up, l0_exp_b_up, l0_exp_w_down, l0_exp_b_down,
                    l0_ln3_g, l0_ln3_b)

    x3_tm = x3.reshape(batch, seq, d).transpose(1, 0, 2).reshape(n, d)
    x4_tm = _s5_mixer_ln(x3_tm, l1_w_in, l1_log_dt, l1_lambda_re, l1_lambda_im, l1_b_re, l1_b_im,
                         l1_c_re, l1_c_im, l1_d_skip, l1_w_val, l1_w_gate, l1_ln1_g, l1_ln1_b, batch)
    x4 = x4_tm.reshape(seq, batch, d).transpose(1, 0, 2).reshape(n, d)
    x5, gates, pos, tile_info = _cross_router(x4, kv, l1_xq, l1_xo, l1_ln2_g, l1_ln2_b, l1_router_w, l1_router_b,
                                              batch, seq, mem_len)
    x6 = _moe_layer(x5, pos, tile_info, gates, l1_exp_w_up, l1_exp_b_up, l1_exp_w_down, l1_exp_b_down,
                    l1_ln3_g, l1_ln3_b)
    return x6.reshape(batch, seq, d)
```

```python
import functools
import math

import jax
import jax.numpy as jnp
from jax import lax
from jax.experimental import pallas as pl
from jax.experimental.pallas import tpu as pltpu

F32 = jnp.float32
BF16 = jnp.bfloat16

D_MODEL = 1024
CHUNK = 128
SGU_GROUPS = 4
SGU_WIDTH = 512
DIFF_HEADS = 4
DIFF_HEAD_DIM = 64
DIFF_V_DIM = 128
DIFF_QK_WIDTH = DIFF_HEADS * 2 * DIFF_HEAD_DIM
S5_GROUP = 16
S5_GROUPS = 64
S5_STATE = 64
X_HEADS = 4
X_HEAD_DIM = 256
N_EXPERTS = 32
TOP_K = 4
D_FF = 1024
SWIGLU_LIMIT = 7.0
SWIGLU_ALPHA = 1.702
MOE_BLOCK = 512
MOE_HALF = MOE_BLOCK // 2
MOE_TILE = 512
DEPTH = 2
ALPHA = (2 * DEPTH) ** 0.25
LN_EPS = 1e-5
NEG_INF = -1e30

LANES = 128
SUBLANES = 8
VMEM_LIMIT_BYTES = 56 * 1024 * 1024
ROW_TILE = D_MODEL // LANES
SEG_ROWS = 64

S5_CHUNK_GROUPS = 8
S5_CHUNK_IN = S5_CHUNK_GROUPS * S5_GROUP
S5_CHUNK_STATE = S5_CHUNK_GROUPS * S5_STATE
S5_N_CHUNKS = S5_GROUPS // S5_CHUNK_GROUPS
S5_STATES = S5_GROUPS * S5_STATE


def _params(*sem):
    return pltpu.CompilerParams(dimension_semantics=sem, vmem_limit_bytes=VMEM_LIMIT_BYTES)


def _gelu(x):
    return 0.5 * x * (1.0 + jnp.tanh(math.sqrt(2.0 / math.pi) * (x + 0.044715 * (x * x * x))))


def _sigmoid(x):
    return 1.0 / (1.0 + jnp.exp(-x))


def _layer_norm(z, g, b):
    mu = jnp.mean(z, axis=-1, keepdims=True)
    zc = z - mu
    var = jnp.mean(zc * zc, axis=-1, keepdims=True)
    return zc * lax.rsqrt(var + LN_EPS) * g + b


def _dot(a, b):
    return jnp.dot(a, b, preferred_element_type=F32)


def _dot_nt(a, b):
    return lax.dot_general(a, b, (((1,), (1,)), ((), ())), preferred_element_type=F32)


def _matmul_kernel(x_ref, w_ref, o_ref):
    o_ref[...] = _dot(x_ref[...].astype(BF16), w_ref[...]).astype(o_ref.dtype)


def _matmul(x, w, out_dtype, tm, tn):
    m, k = x.shape
    n = w.shape[1]
    return pl.pallas_call(
        _matmul_kernel,
        out_shape=jax.ShapeDtypeStruct((m, n), out_dtype),
        grid=(n // tn, m // tm),
        in_specs=[pl.BlockSpec((tm, k), lambda j, i: (i, 0)),
                  pl.BlockSpec((k, tn), lambda j, i: (0, j))],
        out_specs=pl.BlockSpec((tm, tn), lambda j, i: (i, j)),
        compiler_params=_params("parallel", "parallel"),
        name="matmul",
    )(x, w)


def _inproj_sgu_kernel(x_ref, w_ref, lng_ref, lnb_ref, wsp_ref, bsp_ref, a_ref, qk_ref, vt_ref, *, tm):
    xb = x_ref[...].astype(BF16)
    qk_ref[...] = _dot(xb, w_ref[:, 2 * SGU_WIDTH:2 * SGU_WIDTH + 2 * DIFF_QK_WIDTH]).astype(BF16)
    vt_ref[...] = _dot(xb, w_ref[:, 2 * SGU_WIDTH + 2 * DIFF_QK_WIDTH:]).T.astype(BF16)
    h = _dot(xb, w_ref[:, :2 * SGU_WIDTH])
    for g in range(SGU_GROUPS):
        lo = g * CHUNK
        u = _gelu(h[:, lo:lo + CHUNK])
        v = _gelu(h[:, SGU_WIDTH + lo:SGU_WIDTH + lo + CHUNK])
        vn = _layer_norm(v, lng_ref[g:g + 1, :], lnb_ref[g:g + 1, :]).astype(BF16)
        w_g = wsp_ref[g]
        b_g = bsp_ref[:, g:g + 1]
        for c in range(tm // CHUNK):
            r = c * CHUNK
            gate = _dot(w_g, vn[r:r + CHUNK, :]) + b_g
            a_ref[r:r + CHUNK, lo:lo + CHUNK] = (u[r:r + CHUNK, :] * gate).astype(BF16)


def _inproj_sgu(x2d, w_in, ln_g, ln_b, w_spatial, b_spatial, tm=512):
    n, d = x2d.shape
    w = w_in.astype(BF16)
    wsp = jnp.tril(w_spatial).astype(BF16)
    bsp_t = b_spatial.T
    v_width = w.shape[1] - 2 * SGU_WIDTH - 2 * DIFF_QK_WIDTH
    const = lambda i: (0, 0)
    return pl.pallas_call(
        functools.partial(_inproj_sgu_kernel, tm=tm),
        out_shape=(jax.ShapeDtypeStruct((n, SGU_WIDTH), BF16),
                   jax.ShapeDtypeStruct((n, 2 * DIFF_QK_WIDTH), BF16),
                   jax.ShapeDtypeStruct((v_width, n), BF16)),
        grid=(n // tm,),
        in_specs=[pl.BlockSpec((tm, d), lambda i: (i, 0)),
                  pl.BlockSpec(w.shape, const),
                  pl.BlockSpec((SGU_GROUPS, CHUNK), const),
                  pl.BlockSpec((SGU_GROUPS, CHUNK), const),
                  pl.BlockSpec(wsp.shape, lambda i: (0, 0, 0)),
                  pl.BlockSpec(bsp_t.shape, const)],
        out_specs=(pl.BlockSpec((tm, SGU_WIDTH), lambda i: (i, 0)),
                   pl.BlockSpec((tm, 2 * DIFF_QK_WIDTH), lambda i: (i, 0)),
                   pl.BlockSpec((v_width, tm), lambda i: (0, i))),
        compiler_params=_params("parallel"),
        name="inproj_sgu",
    )(x2d, w, ln_g.reshape(SGU_GROUPS, CHUNK), ln_b.reshape(SGU_GROUPS, CHUNK), wsp, bsp_t)


def _diff_attn_kernel(q_ref, k_ref, vt_ref, lam_ref, g_ref, o_ref, *, tq, lam_init):
    i = pl.program_id(1)
    hw = 2 * DIFF_HEAD_DIM
    lane = lax.broadcasted_iota(jnp.int32, (1, hw), 1)
    queries = []
    for h in range(DIFF_HEADS):
        q = q_ref[:, h * hw:(h + 1) * hw] * jnp.asarray(DIFF_HEAD_DIM ** -0.5, BF16)
        zero = jnp.zeros_like(q)
        queries.append((h, jnp.where(lane < DIFF_HEAD_DIM, q, zero)))
        queries.append((h, jnp.where(lane >= DIFF_HEAD_DIM, q, zero)))

    def block(j, carry, masked):
        off = pl.multiple_of(j * tq, tq)
        scores = [_dot_nt(k_ref[pl.ds(off, tq), h * hw:(h + 1) * hw], qc) for h, qc in queries]
        if masked:
            key = lax.broadcasted_iota(jnp.int32, (tq, tq), 0)
            qry = lax.broadcasted_iota(jnp.int32, (tq, tq), 1)
            scores = [jnp.where(key <= qry, st, NEG_INF) for st in scores]
        stats, probs = [], []
        for (m, l, _), st in zip(carry, scores):
            m_new = jnp.maximum(m, jnp.max(st, axis=0, keepdims=True))
            alpha = jnp.exp(m - m_new)
            p = jnp.exp(st - m_new)
            stats.append((m_new, alpha * l + jnp.sum(p, axis=0, keepdims=True), alpha))
            probs.append(p.astype(BF16))
        pv = [_dot(vt_ref[h * DIFF_V_DIM:(h + 1) * DIFF_V_DIM, pl.ds(off, tq)], p)
              for (h, _), p in zip(queries, probs)]
        return tuple((m_new, l_new, alpha * acc + o)
                     for (m_new, l_new, alpha), (_, _, acc), o in zip(stats, carry, pv))

    init = (jnp.full((1, tq), NEG_INF, F32), jnp.zeros((1, tq), F32), jnp.zeros((DIFF_V_DIM, tq), F32))
    carry = lax.fori_loop(0, i, lambda j, c: block(j, c, False), (init,) * len(queries))
    final = block(i, carry, True)

    lam_v = lam_ref[...]
    s_a = jnp.sum(lam_v[0:1, :] * lam_v[1:2, :], axis=-1, keepdims=True)
    s_b = jnp.sum(lam_v[2:3, :] * lam_v[3:4, :], axis=-1, keepdims=True)
    lam = jnp.exp(s_a) - jnp.exp(s_b) + lam_init
    for h in range(DIFF_HEADS):
        (_, l1, a1), (_, l2, a2) = final[2 * h], final[2 * h + 1]
        o_t = a1 / l1 - lam * (a2 / l2)
        o_t = o_t * lax.rsqrt(jnp.mean(o_t * o_t, axis=0, keepdims=True) + LN_EPS) * g_ref[...] * (1.0 - lam_init)
        o_ref[:, h * DIFF_V_DIM:(h + 1) * DIFF_V_DIM] = o_t.T.astype(BF16)


def _diff_attention(qk, vt, lam_q1, lam_k1, lam_q2, lam_k2, subln_g, lam_init, batch, seq, tq=256):
    n = qk.shape[0]
    nq = seq // tq
    lam_v = jnp.stack([lam_q1, lam_k1, lam_q2, lam_k2])
    v_width = DIFF_HEADS * DIFF_V_DIM
    return pl.pallas_call(
        functools.partial(_diff_attn_kernel, tq=tq, lam_init=lam_init),
        out_shape=jax.ShapeDtypeStruct((n, v_width), BF16),
        grid=(batch, nq),
        in_specs=[pl.BlockSpec((tq, DIFF_QK_WIDTH), lambda b, i: (b * nq + i, 0)),
                  pl.BlockSpec((seq, DIFF_QK_WIDTH), lambda b, i: (b, 1)),
                  pl.BlockSpec((v_width, seq), lambda b, i: (0, b)),
                  pl.BlockSpec(lam_v.shape, lambda b, i: (0, 0)),
                  pl.BlockSpec((DIFF_V_DIM, 1), lambda b, i: (0, 0))],
        out_specs=pl.BlockSpec((tq, v_width), lambda b, i: (b * nq + i, 0)),
        compiler_params=_params("parallel", "parallel"),
        name="diff_attention",
    )(qk, qk, vt, lam_v, subln_g.reshape(DIFF_V_DIM, 1))


def _outproj_ln_kernel(a_ref, d_ref, w_ref, x_ref, g_ref, b_ref, o_ref):
    y = _dot(a_ref[...], w_ref[:SGU_WIDTH, :]) + _dot(d_ref[...], w_ref[SGU_WIDTH:, :])
    o_ref[...] = _layer_norm(ALPHA * x_ref[...] + y, g_ref[...], b_ref[...])


def _outproj_ln(a, dattn, w_out, x2d, ln_g, ln_b, tm=1024):
    n, d = x2d.shape
    w = w_out.astype(BF16)
    const = lambda i: (0, 0)
    row = lambda i: (i, 0)
    return pl.pallas_call(
        _outproj_ln_kernel,
        out_shape=jax.ShapeDtypeStruct((n, d), F32),
        grid=(n // tm,),
        in_specs=[pl.BlockSpec((tm, a.shape[1]), row),
                  pl.BlockSpec((tm, dattn.shape[1]), row),
                  pl.BlockSpec(w.shape, const),
                  pl.BlockSpec((tm, d), row),
                  pl.BlockSpec((1, d), const),
                  pl.BlockSpec((1, d), const)],
        out_specs=pl.BlockSpec((tm, d), row),
        compiler_params=_params("parallel"),
        name="outproj_ln1",
    )(a, dattn, w, x2d, ln_g.reshape(1, d), ln_b.reshape(1, d))


def _cross_router_kernel(x_ref, wq_ref, wo_ref, k_ref, v_ref, g_ref, b_ref, rwh_ref, rwl_ref, rb_ref,
                         x2_ref, gate_ref, pos_ref, cnt_ref):
    x = x_ref[...]
    q = (_dot(x.astype(BF16), wq_ref[...]) * (X_HEAD_DIM ** -0.5)).astype(BF16)
    heads = []
    for h in range(X_HEADS):
        lo = h * X_HEAD_DIM
        s = _dot_nt(q[:, lo:lo + X_HEAD_DIM], k_ref[:, lo:lo + X_HEAD_DIM])
        p = jnp.exp(s - jnp.max(s, axis=-1, keepdims=True))
        p = p / jnp.sum(p, axis=-1, keepdims=True)
        heads.append(_dot(p.astype(BF16), v_ref[:, lo:lo + X_HEAD_DIM]).astype(BF16))
    o = jnp.concatenate(heads, axis=-1)
    x2 = _layer_norm(ALPHA * x + _dot(o, wo_ref[...]), g_ref[...], b_ref[...])
    x2_ref[...] = x2

    x_hi = x2.astype(BF16)
    x_lo = (x2 - x_hi.astype(F32)).astype(BF16)
    logits = _dot(x_hi, rwh_ref[...]) + _dot(x_lo, rwh_ref[...]) + _dot(x_hi, rwl_ref[...]) + rb_ref[...]
    lane = lax.broadcasted_iota(jnp.int32, logits.shape, 1).astype(F32)
    work = logits
    top_v, top_i = [], []
    for _ in range(TOP_K):
        m = jnp.max(work, axis=-1, keepdims=True)
        sel = jnp.min(jnp.where(work == m, lane, float(LANES)), axis=-1, keepdims=True)
        top_v.append(m)
        top_i.append(sel)
        work = jnp.where(lane == sel, -jnp.inf, work)
    e = [jnp.exp(v - top_v[0]) for v in top_v]
    denom = e[0] + e[1] + e[2] + e[3]
    gate_out = jnp.zeros(logits.shape, F32)
    for k in range(TOP_K):
        gate_out = jnp.where(lane == float(k), e[k] / denom, gate_out)
    gate_ref[...] = gate_out

    tm = logits.shape[0]
    onehot = jnp.zeros(logits.shape, F32)
    for k in range(TOP_K):
        onehot = jnp.where(lane == top_i[k], 1.0, onehot)
    total = jnp.sum(onehot, axis=0, keepdims=True)
    segments = jnp.floor((total + (SEG_ROWS - 1)) * (1.0 / SEG_ROWS))
    e_r = lax.broadcasted_iota(jnp.int32, (LANES, LANES), 0)
    e_c = lax.broadcasted_iota(jnp.int32, (LANES, LANES), 1)
    before = jnp.where(e_r < e_c, 1.0, 0.0).astype(BF16)
    run_start = SEG_ROWS * _dot(jnp.broadcast_to(segments, (SUBLANES, LANES)).astype(BF16), before)[0:1, :]
    r_i = lax.broadcasted_iota(jnp.int32, (tm, tm), 0)
    c_i = lax.broadcasted_iota(jnp.int32, (tm, tm), 1)
    earlier = jnp.where(c_i < r_i, 1.0, 0.0).astype(BF16)
    pos = _dot(earlier, onehot.astype(BF16)) + run_start
    pos_out = jnp.zeros(logits.shape, F32)
    for k in range(TOP_K):
        pk = jnp.sum(jnp.where(lane == top_i[k], pos, 0.0), axis=-1, keepdims=True)
        pos_out = jnp.where(lane == float(k), pk, pos_out)
    pos_ref[...] = (pos_out * ROW_TILE).astype(jnp.int32)
    sub = lax.broadcasted_iota(jnp.int32, (SUBLANES, LANES), 0)
    cnt_ref[...] = jnp.where(sub == 1, run_start, total).astype(jnp.int32)


def _cross_router(x1, kv, w_q, w_o, ln_g, ln_b, router_w, router_b, batch, seq, mem_len, tm=MOE_TILE):
    n, d = x1.shape
    nt = seq // tm
    pad = LANES - N_EXPERTS
    rw = jnp.pad(router_w, ((0, 0), (0, pad)))
    rw_hi = rw.astype(BF16)
    rw_lo = (rw - rw_hi.astype(F32)).astype(BF16)
    rb = jnp.pad(router_b, (0, pad), constant_values=-jnp.inf).reshape(1, LANES)
    const = lambda b, i: (0, 0)
    row = lambda b, i: (b * nt + i, 0)
    return pl.pallas_call(
        _cross_router_kernel,
        out_shape=(jax.ShapeDtypeStruct((n, d), F32),
                   jax.ShapeDtypeStruct((n, LANES), F32),
                   jax.ShapeDtypeStruct((n, LANES), jnp.int32),
                   jax.ShapeDtypeStruct((n // tm * SUBLANES, LANES), jnp.int32)),
        grid=(batch, nt),
        in_specs=[pl.BlockSpec((tm, d), row),
                  pl.BlockSpec((d, d), const),
                  pl.BlockSpec((d, d), const),
                  pl.BlockSpec((mem_len, d), lambda b, i: (b, 0)),
                  pl.BlockSpec((mem_len, d), lambda b, i: (b, 1)),
                  pl.BlockSpec((1, d), const),
                  pl.BlockSpec((1, d), const),
                  pl.BlockSpec((d, LANES), const),
                  pl.BlockSpec((d, LANES), const),
                  pl.BlockSpec((1, LANES), const)],
        out_specs=(pl.BlockSpec((tm, d), row),
                   pl.BlockSpec((tm, LANES), row),
                   pl.BlockSpec((tm, LANES), row),
                   pl.BlockSpec((SUBLANES, LANES), row)),
        compiler_params=_params("parallel", "parallel"),
        name="cross_attn_router",
    )(x1, w_q.astype(BF16), w_o.astype(BF16), kv, kv, ln_g.reshape(1, d), ln_b.reshape(1, d), rw_hi, rw_lo, rb)


def _to_row_tiles(dst_ref, x, rows, first_row=0):
    for c in range(ROW_TILE):
        dst_ref[pl.ds(first_row * ROW_TILE + c, rows, stride=ROW_TILE), :] = x[:, c * LANES:(c + 1) * LANES]


def _from_row_tiles(src_ref, rows, first_row=0):
    return jnp.concatenate([src_ref[pl.ds(first_row * ROW_TILE + c, rows, stride=ROW_TILE), :]
                            for c in range(ROW_TILE)], axis=-1)


def _tile_rows(row):
    return pl.multiple_of(row * ROW_TILE, ROW_TILE)


def _num_segments(count):
    return lax.shift_right_logical(count + (SEG_ROWS - 1), SEG_ROWS.bit_length() - 1)


def _tile_segments(tcnt_ref):
    return lax.fori_loop(0, N_EXPERTS, lambda e, t: t + _num_segments(tcnt_ref[0, 0, e]), 0)


def _dispatch_kernel(pstart_ref, counts_ref, padded_ref, ldest_ref, tcnt_ref, tbase_ref, lstart_ref, x_ref,
                     xs_hbm, xrt, stage, zeros, nseg, sems, zsem, rsem, *, tm, nt):
    i = pl.program_id(0)
    slot = lax.rem(i, 2)

    @pl.when(i == 0)
    def _():
        stage[...] = jnp.zeros_like(stage)
        zeros[...] = jnp.zeros_like(zeros)

    _to_row_tiles(xrt, x_ref[...], tm)
    st = stage.at[slot]
    for k in range(TOP_K):
        def move(r, c, k=k):
            dst = pl.multiple_of(ldest_ref[0, 0, k * tm + r], ROW_TILE)
            st[pl.ds(dst, ROW_TILE), :] = xrt[pl.ds(_tile_rows(r), ROW_TILE), :]
            return c

        lax.fori_loop(0, tm, move, 0, unroll=32)

    def seg_copy(s, src_row, dst_row):
        return pltpu.make_async_copy(stage.at[s, pl.ds(_tile_rows(src_row), SEG_ROWS * ROW_TILE), :],
                                     xs_hbm.at[pl.ds(_tile_rows(dst_row), SEG_ROWS * ROW_TILE), :], sems.at[s])

    def wait_segments(s, count):
        def wait(j, c):
            seg_copy(s, 0, 0).wait()
            return c

        lax.fori_loop(0, count, wait, 0)

    @pl.when(i > 0)
    def _():
        wait_segments(1 - slot, nseg[1 - slot])

    def issue(e, total):
        n = _num_segments(tcnt_ref[0, 0, e])
        src0 = lstart_ref[0, 0, e]
        dst0 = pstart_ref[e] + tbase_ref[0, 0, e]

        def one(j, c):
            seg_copy(slot, src0 + j * SEG_ROWS, dst0 + j * SEG_ROWS).start()
            return c

        lax.fori_loop(0, n, one, 0)
        return total + n

    total = lax.fori_loop(0, N_EXPERTS, issue, 0)
    nseg[slot] = total

    @pl.when(i == nt - 1)
    def _():
        wait_segments(slot, total)

        def gap(e):
            lo = pstart_ref[e] + counts_ref[e]
            width = padded_ref[e] - counts_ref[e]
            full = lax.shift_right_logical(width, SEG_ROWS.bit_length() - 1)
            return lo, full, width - full * SEG_ROWS

        def seg_zero(row):
            return pltpu.make_async_copy(zeros, xs_hbm.at[pl.ds(_tile_rows(row), SEG_ROWS * ROW_TILE), :], zsem)

        def row_zero(row):
            return pltpu.make_async_copy(zeros.at[pl.ds(0, ROW_TILE), :],
                                         xs_hbm.at[pl.ds(_tile_rows(row), ROW_TILE), :], rsem)

        def fill(e, c):
            lo, full, rest = gap(e)
            lax.fori_loop(0, full, lambda j, c2: (seg_zero(lo + j * SEG_ROWS).start(), c2)[1], 0)
            lax.fori_loop(0, rest, lambda j, c2: (row_zero(lo + full * SEG_ROWS + j).start(), c2)[1], 0)
            return c

        def fill_wait(e, c):
            _, full, rest = gap(e)
            lax.fori_loop(0, full, lambda j, c2: (seg_zero(0).wait(), c2)[1], 0)
            lax.fori_loop(0, rest, lambda j, c2: (row_zero(0).wait(), c2)[1], 0)
            return c

        lax.fori_loop(0, N_EXPERTS, fill, 0)
        lax.fori_loop(0, N_EXPERTS, fill_wait, 0)


def _stage_rows(tm):
    return TOP_K * tm + N_EXPERTS * SEG_ROWS


def _dispatch(x2, pstart, counts, padded, ldest_t, tcnt_t, tbase_t, lstart_t, n_pad, tm):
    n, d = x2.shape
    nt = n // tm
    smem_tile = lambda width: pl.BlockSpec((1, 1, width), lambda i, *_: (i, 0, 0), memory_space=pltpu.SMEM)
    grid_spec = pltpu.PrefetchScalarGridSpec(
        num_scalar_prefetch=3,
        grid=(nt,),
        in_specs=[smem_tile(TOP_K * tm), smem_tile(LANES), smem_tile(LANES), smem_tile(LANES),
                  pl.BlockSpec((tm, d), lambda i, *_: (i, 0))],
        out_specs=pl.BlockSpec(memory_space=pl.ANY),
        scratch_shapes=[pltpu.VMEM((tm * ROW_TILE, LANES), F32),
                        pltpu.VMEM((2, _stage_rows(tm) * ROW_TILE, LANES), F32),
                        pltpu.VMEM((SEG_ROWS * ROW_TILE, LANES), F32),
                        pltpu.SMEM((2,), jnp.int32),
                        pltpu.SemaphoreType.DMA((2,)),
                        pltpu.SemaphoreType.DMA,
                        pltpu.SemaphoreType.DMA],
    )
    return pl.pallas_call(
        functools.partial(_dispatch_kernel, tm=tm, nt=nt),
        out_shape=jax.ShapeDtypeStruct((n_pad * ROW_TILE, LANES), F32),
        grid_spec=grid_spec,
        compiler_params=_params("arbitrary"),
        name="moe_dispatch",
    )(pstart, counts, padded, ldest_t, tcnt_t, tbase_t, lstart_t, x2)


def _expert_kernel(be_ref, valid_ref, xs_ref, wup_ref, bup_ref, wdn_ref, bdn_ref, y_ref, wup_bf, wdn_bf):
    i = pl.program_id(0)

    @pl.when(valid_ref[i] == 0)
    def _():
        y_ref[...] = jnp.zeros_like(y_ref)

    @pl.when(valid_ref[i] > 0)
    def _():
        prev = be_ref[jnp.maximum(i - 1, 0)]

        @pl.when(jnp.logical_or(i == 0, be_ref[i] != prev))
        def _():
            wup_bf[...] = wup_ref[0].astype(BF16)
            wdn_bf[...] = wdn_ref[0].astype(BF16)

        def expert_rows(first_row):
            xb = _from_row_tiles(xs_ref, MOE_HALF, first_row).astype(BF16)
            h = _dot(xb, wup_bf[...]) + bup_ref[0]
            glu = jnp.minimum(h[:, :D_FF], SWIGLU_LIMIT)
            lin = jnp.clip(h[:, D_FF:], -SWIGLU_LIMIT, SWIGLU_LIMIT)
            act = glu * _sigmoid(SWIGLU_ALPHA * glu) * (lin + 1.0)
            _to_row_tiles(y_ref, _dot(act.astype(BF16), wdn_bf[...]) + bdn_ref[0], MOE_HALF, first_row)

        @pl.when(valid_ref[i] > MOE_HALF)
        def _():
            expert_rows(0)
            expert_rows(MOE_HALF)

        @pl.when(valid_ref[i] <= MOE_HALF)
        def _():
            expert_rows(0)
            y_ref[MOE_HALF * ROW_TILE:, :] = jnp.zeros((MOE_HALF * ROW_TILE, LANES), F32)


def _experts(xs, blk_expert, blk_valid, w_up, b_up, w_down, b_down):
    n_blocks = blk_expert.shape[0]
    d = w_up.shape[1]
    f2 = w_up.shape[2]
    blk_rows = MOE_BLOCK * ROW_TILE
    grid_spec = pltpu.PrefetchScalarGridSpec(
        num_scalar_prefetch=2,
        grid=(n_blocks,),
        in_specs=[pl.BlockSpec((blk_rows, LANES), lambda i, be, va: (jnp.where(va[i] > 0, i, 0), 0)),
                  pl.BlockSpec((1, d, f2), lambda i, be, va: (be[i], 0, 0)),
                  pl.BlockSpec((1, 1, f2), lambda i, be, va: (be[i], 0, 0)),
                  pl.BlockSpec((1, D_FF, d), lambda i, be, va: (be[i], 0, 0)),
                  pl.BlockSpec((1, 1, d), lambda i, be, va: (be[i], 0, 0))],
        out_specs=pl.BlockSpec((blk_rows, LANES), lambda i, be, va: (i, 0)),
        scratch_shapes=[pltpu.VMEM((d, f2), BF16),
                        pltpu.VMEM((D_FF, d), BF16)],
    )
    return pl.pallas_call(
        _expert_kernel,
        out_shape=jax.ShapeDtypeStruct((n_blocks * blk_rows, LANES), F32),
        grid_spec=grid_spec,
        compiler_params=_params("arbitrary"),
        name="moe_experts",
    )(blk_expert, blk_valid, xs, w_up, b_up.reshape(N_EXPERTS, 1, f2), w_down, b_down.reshape(N_EXPERTS, 1, d))


def _combine_ln_kernel(pstart_ref, ldest_ref, gate_ref, tcnt_ref, tbase_ref, lstart_ref,
                       tcnt_next_ref, tbase_next_ref, lstart_next_ref, ys_hbm, x_ref, g_ref, b_ref,
                       o_ref, stage, yrt, sems, *, tm, nt):
    i = pl.program_id(0)
    slot = lax.rem(i, 2)

    def seg_copy(s, src_row, dst_row):
        return pltpu.make_async_copy(ys_hbm.at[pl.ds(_tile_rows(src_row), SEG_ROWS * ROW_TILE), :],
                                     stage.at[s, pl.ds(_tile_rows(dst_row), SEG_ROWS * ROW_TILE), :], sems.at[s])

    def fetch(s, tcnt, tbase, lstart):
        def per_expert(e, c):
            src0 = pstart_ref[e] + tbase[0, 0, e]
            dst0 = lstart[0, 0, e]

            def one(j, c2):
                seg_copy(s, src0 + j * SEG_ROWS, dst0 + j * SEG_ROWS).start()
                return c2

            lax.fori_loop(0, _num_segments(tcnt[0, 0, e]), one, 0)
            return c

        lax.fori_loop(0, N_EXPERTS, per_expert, 0)

    @pl.when(i == 0)
    def _():
        fetch(0, tcnt_ref, tbase_ref, lstart_ref)

    @pl.when(i + 1 < nt)
    def _():
        fetch(1 - slot, tcnt_next_ref, tbase_next_ref, lstart_next_ref)

    def wait(j, c):
        seg_copy(slot, 0, 0).wait()
        return c

    lax.fori_loop(0, _tile_segments(tcnt_ref), wait, 0)

    st = stage.at[slot]

    def reduce(r, c):
        acc = None
        for k in range(TOP_K):
            row = st[pl.ds(pl.multiple_of(ldest_ref[0, 0, k * tm + r], ROW_TILE), ROW_TILE), :]
            term = gate_ref[0, 0, k * tm + r] * row
            acc = term if acc is None else acc + term
        yrt[pl.ds(_tile_rows(r), ROW_TILE), :] = acc
        return c

    lax.fori_loop(0, tm, reduce, 0, unroll=16)
    y = _from_row_tiles(yrt, tm)
    o_ref[...] = _layer_norm(ALPHA * x_ref[...] + y, g_ref[...], b_ref[...])


def _combine_ln(ys, pstart, ldest_t, gates_t, tcnt_t, tbase_t, lstart_t, x2, ln_g, ln_b, tm):
    n, d = x2.shape
    nt = n // tm
    cur = lambda width: pl.BlockSpec((1, 1, width), lambda i, *_: (i, 0, 0), memory_space=pltpu.SMEM)
    nxt = lambda width: pl.BlockSpec((1, 1, width), lambda i, *_: (jnp.minimum(i + 1, nt - 1), 0, 0),
                                     memory_space=pltpu.SMEM)
    grid_spec = pltpu.PrefetchScalarGridSpec(
        num_scalar_prefetch=1,
        grid=(nt,),
        in_specs=[cur(TOP_K * tm), cur(TOP_K * tm), cur(LANES), cur(LANES), cur(LANES),
                  nxt(LANES), nxt(LANES), nxt(LANES),
                  pl.BlockSpec(memory_space=pl.ANY),
                  pl.BlockSpec((tm, d), lambda i, *_: (i, 0)),
                  pl.BlockSpec((1, d), lambda i, *_: (0, 0)),
                  pl.BlockSpec((1, d), lambda i, *_: (0, 0))],
        out_specs=pl.BlockSpec((tm, d), lambda i, *_: (i, 0)),
        scratch_shapes=[pltpu.VMEM((2, _stage_rows(tm) * ROW_TILE, LANES), F32),
                        pltpu.VMEM((tm * ROW_TILE, LANES), F32),
                        pltpu.SemaphoreType.DMA((2,))],
    )
    return pl.pallas_call(
        functools.partial(_combine_ln_kernel, tm=tm, nt=nt),
        out_shape=jax.ShapeDtypeStruct((n, d), F32),
        grid_spec=grid_spec,
        compiler_params=_params("arbitrary"),
        name="moe_combine_ln3",
    )(pstart, ldest_t, gates_t, tcnt_t, tbase_t, lstart_t, tcnt_t, tbase_t, lstart_t, ys, x2,
      ln_g.reshape(1, d), ln_b.reshape(1, d))


def _round_up(v, m):
    return (v + m - 1) // m * m


def _moe_layer(x2, pos_pad, tile_pad, gates_pad, w_up, b_up, w_down, b_down, ln_g, ln_b, tm=MOE_TILE):
    n = x2.shape[0]
    nt = n // tm
    i32 = jnp.int32
    tile_info = tile_pad.reshape(nt, SUBLANES, LANES)
    tcnt = tile_info[:, 0, :N_EXPERTS]
    lstart = tile_info[:, 1, :N_EXPERTS]
    tbase = jnp.cumsum(tcnt, axis=0) - tcnt
    counts = jnp.sum(tcnt, axis=0)
    padded = _round_up(counts + SEG_ROWS, MOE_BLOCK)
    pend = jnp.cumsum(padded)
    pstart = pend - padded
    n_blocks = -(-(n * TOP_K + N_EXPERTS * (SEG_ROWS + MOE_BLOCK - 1)) // MOE_BLOCK)
    block_start = jnp.arange(n_blocks, dtype=i32) * MOE_BLOCK
    blk_expert = jnp.minimum(jnp.sum(pend[None, :] <= block_start[:, None], axis=1), N_EXPERTS - 1).astype(i32)
    blk_valid = jnp.clip(counts[blk_expert] - (block_start - pstart[blk_expert]), 0, MOE_BLOCK).astype(i32)
    per_tile = lambda a: a[:, :TOP_K].reshape(nt, tm, TOP_K).transpose(0, 2, 1).reshape(nt, 1, TOP_K * tm)
    ldest_t = per_tile(pos_pad)
    gates_t = per_tile(gates_pad)
    lane_pad = lambda a: jnp.pad(a.astype(i32), ((0, 0), (0, LANES - N_EXPERTS))).reshape(nt, 1, LANES)
    tcnt_t, tbase_t, lstart_t = lane_pad(tcnt), lane_pad(tbase), lane_pad(lstart)
    pstart, counts, padded = pstart.astype(i32), counts.astype(i32), padded.astype(i32)

    xs = _dispatch(x2, pstart, counts, padded, ldest_t, tcnt_t, tbase_t, lstart_t, n_blocks * MOE_BLOCK, tm)
    ys = _experts(xs, blk_expert, blk_valid, w_up, b_up, w_down, b_down)
    return _combine_ln(ys, pstart, ldest_t, gates_t, tcnt_t, tbase_t, lstart_t, x2, ln_g, ln_b, tm)


def _s5_kernel(x_ref, win_ref, bblk_ref, ar_ref, ai_ref, cblk_ref, dsk_ref, wval_ref, wgate_ref, g_ref, b_ref,
               o_ref, bur, bui, sr, si, *, tt, batch):
    rows = tt * batch

    @pl.when(pl.program_id(0) == 0)
    def _():
        sr[...] = jnp.zeros_like(sr)
        si[...] = jnp.zeros_like(si)

    x = x_ref[...]
    u = _dot(x.astype(BF16), win_ref[...])
    ub = u.astype(BF16)
    for c in range(S5_N_CHUNKS):
        bu = _dot(ub[:, c * S5_CHUNK_IN:(c + 1) * S5_CHUNK_IN], bblk_ref[c])
        bur[:, c * S5_CHUNK_STATE:(c + 1) * S5_CHUNK_STATE] = bu[:, :S5_CHUNK_STATE]
        bui[:, c * S5_CHUNK_STATE:(c + 1) * S5_CHUNK_STATE] = bu[:, S5_CHUNK_STATE:]

    for c in range(S5_N_CHUNKS):
        cols = pl.ds(c * S5_CHUNK_STATE, S5_CHUNK_STATE)
        a_r = ar_ref[:, cols]
        a_i = ai_ref[:, cols]

        def step(t, carry):
            s_r, s_i = carry
            rsl = pl.ds(pl.multiple_of(t * batch, batch), batch)
            n_r = a_r * s_r - a_i * s_i + bur[rsl, cols]
            n_i = a_r * s_i + a_i * s_r + bui[rsl, cols]
            bur[rsl, cols] = n_r
            bui[rsl, cols] = n_i
            return n_r, n_i

        f_r, f_i = lax.fori_loop(0, tt, step, (sr[:, cols], si[:, cols]), unroll=True)
        sr[:, cols] = f_r
        si[:, cols] = f_i

    ys = []
    for c in range(S5_N_CHUNKS):
        cols = pl.ds(c * S5_CHUNK_STATE, S5_CHUNK_STATE)
        xri = jnp.concatenate([bur[:, cols].astype(BF16), bui[:, cols].astype(BF16)], axis=-1)
        ys.append(_dot(xri, cblk_ref[c]))
    y = jnp.concatenate(ys, axis=-1) + dsk_ref[...] * u
    yb = _gelu(y).astype(BF16)
    hmix = _dot(yb, wval_ref[...]) * _sigmoid(_dot(yb, wgate_ref[...]))
    o_ref[...] = _layer_norm(ALPHA * x + hmix, g_ref[...], b_ref[...])


def _s5_discretize(log_dt, lambda_re, lambda_im, b_re, b_im, c_re, c_im):
    dt = jnp.exp(log_dt)[:, None]
    mag = jnp.exp(lambda_re * dt)
    ar = mag * jnp.cos(lambda_im * dt)
    ai = mag * jnp.sin(lambda_im * dt)
    den = lambda_re * lambda_re + lambda_im * lambda_im
    zr = ((ar - 1.0) * lambda_re + ai * lambda_im) / den
    zi = (ai * lambda_re - (ar - 1.0) * lambda_im) / den
    bbar_re = zr[..., None] * b_re - zi[..., None] * b_im
    bbar_im = zr[..., None] * b_im + zi[..., None] * b_re
    ng, gc, p, gw = S5_N_CHUNKS, S5_CHUNK_GROUPS, S5_STATE, S5_GROUP
    eye = jnp.eye(gc, dtype=F32)
    bre = bbar_re.reshape(ng, gc, p, gw).transpose(0, 1, 3, 2)
    bim = bbar_im.reshape(ng, gc, p, gw).transpose(0, 1, 3, 2)
    blk_re = jnp.einsum('cgip,gh->cgihp', bre, eye).reshape(ng, gc * gw, gc * p)
    blk_im = jnp.einsum('cgip,gh->cgihp', bim, eye).reshape(ng, gc * gw, gc * p)
    bblk = jnp.concatenate([blk_re, blk_im], axis=-1).astype(BF16)
    cre = c_re.reshape(ng, gc, gw, p).transpose(0, 1, 3, 2)
    cim = c_im.reshape(ng, gc, gw, p).transpose(0, 1, 3, 2)
    cblk_re = jnp.einsum('cgpi,gh->cgphi', cre, eye).reshape(ng, gc * p, gc * gw)
    cblk_im = jnp.einsum('cgpi,gh->cgphi', cim, eye).reshape(ng, gc * p, gc * gw)
    cblk = jnp.concatenate([cblk_re, -cblk_im], axis=1).astype(BF16)
    return bblk, ar.reshape(1, S5_STATES), ai.reshape(1, S5_STATES), cblk


def _s5_mixer_ln(x_tm, w_in, log_dt, lambda_re, lambda_im, b_re, b_im, c_re, c_im, d_skip, w_val, w_gate,
                 ln_g, ln_b, batch, tt=16):
    n, d = x_tm.shape
    rows = tt * batch
    bblk, ar, ai, cblk = _s5_discretize(log_dt, lambda_re, lambda_im, b_re, b_im, c_re, c_im)
    const2 = lambda i: (0, 0)
    const3 = lambda i: (0, 0, 0)
    row = lambda i: (i, 0)
    return pl.pallas_call(
        functools.partial(_s5_kernel, tt=tt, batch=batch),
        out_shape=jax.ShapeDtypeStruct((n, d), F32),
        grid=(n // rows,),
        in_specs=[pl.BlockSpec((rows, d), row),
                  pl.BlockSpec((d, d), const2),
                  pl.BlockSpec(bblk.shape, const3),
                  pl.BlockSpec(ar.shape, const2),
                  pl.BlockSpec(ai.shape, const2),
                  pl.BlockSpec(cblk.shape, const3),
                  pl.BlockSpec((1, d), const2),
                  pl.BlockSpec((d, d), const2),
                  pl.BlockSpec((d, d), const2),
                  pl.BlockSpec((1, d), const2),
                  pl.BlockSpec((1, d), const2)],
        out_specs=pl.BlockSpec((rows, d), row),
        scratch_shapes=[pltpu.VMEM((rows, S5_STATES), F32),
                        pltpu.VMEM((rows, S5_STATES), F32),
                        pltpu.VMEM((batch, S5_STATES), F32),
                        pltpu.VMEM((batch, S5_STATES), F32)],
        compiler_params=_params("arbitrary"),
        name="s5_mixer_ln1",
    )(x_tm, w_in.astype(BF16), bblk, ar, ai, cblk, d_skip.reshape(1, d), w_val.astype(BF16), w_gate.astype(BF16),
      ln_g.reshape(1, d), ln_b.reshape(1, d))


def _diff_lambda_init(layer_idx):
    return 0.8 - 0.6 * math.exp(-0.3 * layer_idx)


def kernel(x, mem, w_mem_kv, l0_w_in, l0_sgu_ln_g, l0_sgu_ln_b, l0_w_spatial, l0_b_spatial, l0_lam_q1, l0_lam_k1, l0_lam_q2, l0_lam_k2, l0_subln_g, l0_w_out, l0_ln1_g, l0_ln1_b, l0_xq, l0_xo, l0_ln2_g, l0_ln2_b, l0_router_w, l0_router_b, l0_exp_w_up, l0_exp_b_up, l0_exp_w_down, l0_exp_b_down, l0_ln3_g, l0_ln3_b, l1_w_in, l1_log_dt, l1_lambda_re, l1_lambda_im, l1_b_re, l1_b_im, l1_c_re, l1_c_im, l1_d_skip, l1_w_val, l1_w_gate, l1_ln1_g, l1_ln1_b, l1_xq, l1_xo, l1_ln2_g, l1_ln2_b, l1_router_w, l1_router_b, l1_exp_w_up, l1_exp_b_up, l1_exp_w_down, l1_exp_b_down, l1_ln3_g, l1_ln3_b):
    batch, seq, d = x.shape
    mem_len = mem.shape[1]
    n = batch * seq
    x0 = x.reshape(n, d)

    kv = _matmul(mem.reshape(batch * mem_len, d), w_mem_kv.astype(BF16), BF16,
                 tm=min(512, batch * mem_len), tn=d)

    a, qk, vt = _inproj_sgu(x0, l0_w_in, l0_sgu_ln_g, l0_sgu_ln_b, l0_w_spatial, l0_b_spatial)
    dattn = _diff_attention(qk, vt, l0_lam_q1, l0_lam_k1, l0_lam_q2, l0_lam_k2, l0_subln_g,
                            _diff_lambda_init(0), batch, seq)
    x1 = _outproj_ln(a, dattn, l0_w_out, x0, l0_ln1_g, l0_ln1_b)
    x2, gates, pos, tile_info = _cross_router(x1, kv, l0_xq, l0_xo, l0_ln2_g, l0_ln2_b, l0_router_w, l0_router_b,
                                              batch, seq, mem_len)
    x3 = _moe_layer(x2, pos, tile_info, gates, l0_exp_w_up, l0_exp_b_up, l0_exp_w_down, l0_exp_b_down,
                    l0_ln3_g, l0_ln3_b)

    x3_tm = x3.reshape(batch, seq, d).transpose(1, 0, 2).reshape(n, d)
    x4_tm = _s5_mixer_ln(x3_tm, l1_w_in, l1_log_dt, l1_lambda_re, l1_lambda_im, l1_b_re, l1_b_im,
                         l1_c_re, l1_c_im, l1_d_skip, l1_w_val, l1_w_gate, l1_ln1_g, l1_ln1_b, batch)
    x4 = x4_tm.reshape(seq, batch, d).transpose(1, 0, 2).reshape(n, d)
    x5, gates, pos, tile_info = _cross_router(x4, kv, l1_xq, l1_xo, l1_ln2_g, l1_ln2_b, l1_router_w, l1_router_b,
                                              batch, seq, mem_len)
    x6 = _moe_layer(x5, pos, tile_info, gates, l1_exp_w_up, l1_exp_b_up, l1_exp_w_down, l1_exp_b_down,
                    l1_ln3_g, l1_ln3_b)
    return x6.reshape(batch, seq, d)
```

```python
import functools
import math

import jax
import jax.numpy as jnp
from jax import lax
from jax.experimental import pallas as pl
from jax.experimental.pallas import tpu as pltpu

F32 = jnp.float32
BF16 = jnp.bfloat16

D_MODEL = 1024
CHUNK = 128
SGU_GROUPS = 4
SGU_WIDTH = 512
DIFF_HEADS = 4
DIFF_HEAD_DIM = 64
DIFF_V_DIM = 128
DIFF_QK_WIDTH = DIFF_HEADS * 2 * DIFF_HEAD_DIM
S5_GROUP = 16
S5_GROUPS = 64
S5_STATE = 64
X_HEADS = 4
X_HEAD_DIM = 256
N_EXPERTS = 32
TOP_K = 4
D_FF = 1024
SWIGLU_LIMIT = 7.0
SWIGLU_ALPHA = 1.702
MOE_BLOCK = 512
MOE_HALF = MOE_BLOCK // 2
MOE_TILE = 512
DEPTH = 2
ALPHA = (2 * DEPTH) ** 0.25
LN_EPS = 1e-5
NEG_INF = -1e30

LANES = 128
SUBLANES = 8
VMEM_LIMIT_BYTES = 56 * 1024 * 1024
ROW_TILE = D_MODEL // LANES
SEG_ROWS = 64

S5_CHUNK_GROUPS = 8
S5_CHUNK_IN = S5_CHUNK_GROUPS * S5_GROUP
S5_CHUNK_STATE = S5_CHUNK_GROUPS * S5_STATE
S5_N_CHUNKS = S5_GROUPS // S5_CHUNK_GROUPS
S5_STATES = S5_GROUPS * S5_STATE


def _params(*sem):
    return pltpu.CompilerParams(dimension_semantics=sem, vmem_limit_bytes=VMEM_LIMIT_BYTES)


def _gelu(x):
    return 0.5 * x * (1.0 + jnp.tanh(math.sqrt(2.0 / math.pi) * (x + 0.044715 * (x * x * x))))


def _sigmoid(x):
    return 1.0 / (1.0 + jnp.exp(-x))


def _layer_norm(z, g, b):
    mu = jnp.mean(z, axis=-1, keepdims=True)
    zc = z - mu
    var = jnp.mean(zc * zc, axis=-1, keepdims=True)
    return zc * lax.rsqrt(var + LN_EPS) * g + b


def _dot(a, b):
    return jnp.dot(a, b, preferred_element_type=F32)


def _dot_nt(a, b):
    return lax.dot_general(a, b, (((1,), (1,)), ((), ())), preferred_element_type=F32)


def _matmul_kernel(x_ref, w_ref, o_ref):
    o_ref[...] = _dot(x_ref[...].astype(BF16), w_ref[...]).astype(o_ref.dtype)


def _matmul(x, w, out_dtype, tm, tn):
    m, k = x.shape
    n = w.shape[1]
    return pl.pallas_call(
        _matmul_kernel,
        out_shape=jax.ShapeDtypeStruct((m, n), out_dtype),
        grid=(n // tn, m // tm),
        in_specs=[pl.BlockSpec((tm, k), lambda j, i: (i, 0)),
                  pl.BlockSpec((k, tn), lambda j, i: (0, j))],
        out_specs=pl.BlockSpec((tm, tn), lambda j, i: (i, j)),
        compiler_params=_params("parallel", "parallel"),
        name="matmul",
    )(x, w)


def _inproj_sgu_kernel(x_ref, w_ref, lng_ref, lnb_ref, wsp_ref, bsp_ref, a_ref, qk_ref, vt_ref, *, tm):
    xb = x_ref[...].astype(BF16)
    qk_ref[...] = _dot(xb, w_ref[:, 2 * SGU_WIDTH:2 * SGU_WIDTH + 2 * DIFF_QK_WIDTH]).astype(BF16)
    vt_ref[...] = _dot(xb, w_ref[:, 2 * SGU_WIDTH + 2 * DIFF_QK_WIDTH:]).T.astype(BF16)
    h = _dot(xb, w_ref[:, :2 * SGU_WIDTH])
    for g in range(SGU_GROUPS):
        lo = g * CHUNK
        u = _gelu(h[:, lo:lo + CHUNK])
        v = _gelu(h[:, SGU_WIDTH + lo:SGU_WIDTH + lo + CHUNK])
        vn = _layer_norm(v, lng_ref[g:g + 1, :], lnb_ref[g:g + 1, :]).astype(BF16)
        w_g = wsp_ref[g]
        b_g = bsp_ref[:, g:g + 1]
        for c in range(tm // CHUNK):
            r = c * CHUNK
            gate = _dot(w_g, vn[r:r + CHUNK, :]) + b_g
            a_ref[r:r + CHUNK, lo:lo + CHUNK] = (u[r:r + CHUNK, :] * gate).astype(BF16)


def _inproj_sgu(x2d, w_in, ln_g, ln_b, w_spatial, b_spatial, tm=512):
    n, d = x2d.shape
    w = w_in.astype(BF16)
    wsp = jnp.tril(w_spatial).astype(BF16)
    bsp_t = b_spatial.T
    v_width = w.shape[1] - 2 * SGU_WIDTH - 2 * DIFF_QK_WIDTH
    const = lambda i: (0, 0)
    return pl.pallas_call(
        functools.partial(_inproj_sgu_kernel, tm=tm),
        out_shape=(jax.ShapeDtypeStruct((n, SGU_WIDTH), BF16),
                   jax.ShapeDtypeStruct((n, 2 * DIFF_QK_WIDTH), BF16),
                   jax.ShapeDtypeStruct((v_width, n), BF16)),
        grid=(n // tm,),
        in_specs=[pl.BlockSpec((tm, d), lambda i: (i, 0)),
                  pl.BlockSpec(w.shape, const),
                  pl.BlockSpec((SGU_GROUPS, CHUNK), const),
                  pl.BlockSpec((SGU_GROUPS, CHUNK), const),
                  pl.BlockSpec(wsp.shape, lambda i: (0, 0, 0)),
                  pl.BlockSpec(bsp_t.shape, const)],
        out_specs=(pl.BlockSpec((tm, SGU_WIDTH), lambda i: (i, 0)),
                   pl.BlockSpec((tm, 2 * DIFF_QK_WIDTH), lambda i: (i, 0)),
                   pl.BlockSpec((v_width, tm), lambda i: (0, i))),
        compiler_params=_params("parallel"),
        name="inproj_sgu",
    )(x2d, w, ln_g.reshape(SGU_GROUPS, CHUNK), ln_b.reshape(SGU_GROUPS, CHUNK), wsp, bsp_t)


def _diff_attn_kernel(q_ref, k_ref, vt_ref, lam_ref, g_ref, o_ref, *, tq, lam_init):
    i = pl.program_id(1)
    hw = 2 * DIFF_HEAD_DIM
    lane = lax.broadcasted_iota(jnp.int32, (1, hw), 1)
    queries = []
    for h in range(DIFF_HEADS):
        q = q_ref[:, h * hw:(h + 1) * hw] * jnp.asarray(DIFF_HEAD_DIM ** -0.5, BF16)
        zero = jnp.zeros_like(q)
        queries.append((h, jnp.where(lane < DIFF_HEAD_DIM, q, zero)))
        queries.append((h, jnp.where(lane >= DIFF_HEAD_DIM, q, zero)))

    def block(j, carry, masked):
        off = pl.multiple_of(j * tq, tq)
        scores = [_dot_nt(k_ref[pl.ds(off, tq), h * hw:(h + 1) * hw], qc) for h, qc in queries]
        if masked:
            key = lax.broadcasted_iota(jnp.int32, (tq, tq), 0)
            qry = lax.broadcasted_iota(jnp.int32, (tq, tq), 1)
            scores = [jnp.where(key <= qry, st, NEG_INF) for st in scores]
        stats, probs = [], []
        for (m, l, _), st in zip(carry, scores):
            m_new = jnp.maximum(m, jnp.max(st, axis=0, keepdims=True))
            alpha = jnp.exp(m - m_new)
            p = jnp.exp(st - m_new)
            stats.append((m_new, alpha * l + jnp.sum(p, axis=0, keepdims=True), alpha))
            probs.append(p.astype(BF16))
        pv = [_dot(vt_ref[h * DIFF_V_DIM:(h + 1) * DIFF_V_DIM, pl.ds(off, tq)], p)
              for (h, _), p in zip(queries, probs)]
        return tuple((m_new, l_new, alpha * acc + o)
                     for (m_new, l_new, alpha), (_, _, acc), o in zip(stats, carry, pv))

    init = (jnp.full((1, tq), NEG_INF, F32), jnp.zeros((1, tq), F32), jnp.zeros((DIFF_V_DIM, tq), F32))
    carry = lax.fori_loop(0, i, lambda j, c: block(j, c, False), (init,) * len(queries))
    final = block(i, carry, True)

    lam_v = lam_ref[...]
    s_a = jnp.sum(lam_v[0:1, :] * lam_v[1:2, :], axis=-1, keepdims=True)
    s_b = jnp.sum(lam_v[2:3, :] * lam_v[3:4, :], axis=-1, keepdims=True)
    lam = jnp.exp(s_a) - jnp.exp(s_b) + lam_init
    for h in range(DIFF_HEADS):
        (_, l1, a1), (_, l2, a2) = final[2 * h], final[2 * h + 1]
        o_t = a1 / l1 - lam * (a2 / l2)
        o_t = o_t * lax.rsqrt(jnp.mean(o_t * o_t, axis=0, keepdims=True) + LN_EPS) * g_ref[...] * (1.0 - lam_init)
        o_ref[:, h * DIFF_V_DIM:(h + 1) * DIFF_V_DIM] = o_t.T.astype(BF16)


def _diff_attention(qk, vt, lam_q1, lam_k1, lam_q2, lam_k2, subln_g, lam_init, batch, seq, tq=256):
    n = qk.shape[0]
    nq = seq // tq
    lam_v = jnp.stack([lam_q1, lam_k1, lam_q2, lam_k2])
    v_width = DIFF_HEADS * DIFF_V_DIM
    return pl.pallas_call(
        functools.partial(_diff_attn_kernel, tq=tq, lam_init=lam_init),
        out_shape=jax.ShapeDtypeStruct((n, v_width), BF16),
        grid=(batch, nq),
        in_specs=[pl.BlockSpec((tq, DIFF_QK_WIDTH), lambda b, i: (b * nq + i, 0)),
                  pl.BlockSpec((seq, DIFF_QK_WIDTH), lambda b, i: (b, 1)),
                  pl.BlockSpec((v_width, seq), lambda b, i: (0, b)),
                  pl.BlockSpec(lam_v.shape, lambda b, i: (0, 0)),
                  pl.BlockSpec((DIFF_V_DIM, 1), lambda b, i: (0, 0))],
        out_specs=pl.BlockSpec((tq, v_width), lambda b, i: (b * nq + i, 0)),
        compiler_params=_params("parallel", "parallel"),
        name="diff_attention",
    )(qk, qk, vt, lam_v, subln_g.reshape(DIFF_V_DIM, 1))


def _outproj_ln_kernel(a_ref, d_ref, w_ref, x_ref, g_ref, b_ref, o_ref):
    y = _dot(a_ref[...], w_ref[:SGU_WIDTH, :]) + _dot(d_ref[...], w_ref[SGU_WIDTH:, :])
    o_ref[...] = _layer_norm(ALPHA * x_ref[...] + y, g_ref[...], b_ref[...])


def _outproj_ln(a, dattn, w_out, x2d, ln_g, ln_b, tm=1024):
    n, d = x2d.shape
    w = w_out.astype(BF16)
    const = lambda i: (0, 0)
    row = lambda i: (i, 0)
    return pl.pallas_call(
        _outproj_ln_kernel,
        out_shape=jax.ShapeDtypeStruct((n, d), F32),
        grid=(n // tm,),
        in_specs=[pl.BlockSpec((tm, a.shape[1]), row),
                  pl.BlockSpec((tm, dattn.shape[1]), row),
                  pl.BlockSpec(w.shape, const),
                  pl.BlockSpec((tm, d), row),
                  pl.BlockSpec((1, d), const),
                  pl.BlockSpec((1, d), const)],
        out_specs=pl.BlockSpec((tm, d), row),
        compiler_params=_params("parallel"),
        name="outproj_ln1",
    )(a, dattn, w, x2d, ln_g.reshape(1, d), ln_b.reshape(1, d))


def _cross_router_kernel(x_ref, wq_ref, wo_ref, k_ref, v_ref, g_ref, b_ref, rwh_ref, rwl_ref, rb_ref,
                         x2_ref, gate_ref, pos_ref, cnt_ref):
    x = x_ref[...]
    q = (_dot(x.astype(BF16), wq_ref[...]) * (X_HEAD_DIM ** -0.5)).astype(BF16)
    heads = []
    for h in range(X_HEADS):
        lo = h * X_HEAD_DIM
        s = _dot_nt(q[:, lo:lo + X_HEAD_DIM], k_ref[:, lo:lo + X_HEAD_DIM])
        p = jnp.exp(s - jnp.max(s, axis=-1, keepdims=True))
        p = p / jnp.sum(p, axis=-1, keepdims=True)
        heads.append(_dot(p.astype(BF16), v_ref[:, lo:lo + X_HEAD_DIM]).astype(BF16))
    o = jnp.concatenate(heads, axis=-1)
    x2 = _layer_norm(ALPHA * x + _dot(o, wo_ref[...]), g_ref[...], b_ref[...])
    x2_ref[...] = x2

    x_hi = x2.astype(BF16)
    x_lo = (x2 - x_hi.astype(F32)).astype(BF16)
    logits = _dot(x_hi, rwh_ref[...]) + _dot(x_lo, rwh_ref[...]) + _dot(x_hi, rwl_ref[...]) + rb_ref[...]
    lane = lax.broadcasted_iota(jnp.int32, logits.shape, 1).astype(F32)
    work = logits
    top_v, top_i = [], []
    for _ in range(TOP_K):
        m = jnp.max(work, axis=-1, keepdims=True)
        sel = jnp.min(jnp.where(work == m, lane, float(LANES)), axis=-1, keepdims=True)
        top_v.append(m)
        top_i.append(sel)
        work = jnp.where(lane == sel, -jnp.inf, work)
    e = [jnp.exp(v - top_v[0]) for v in top_v]
    denom = e[0] + e[1] + e[2] + e[3]
    gate_out = jnp.zeros(logits.shape, F32)
    for k in range(TOP_K):
        gate_out = jnp.where(lane == float(k), e[k] / denom, gate_out)
    gate_ref[...] = gate_out.T[:SUBLANES, :]

    tm = logits.shape[0]
    onehot = jnp.zeros(logits.shape, F32)
    for k in range(TOP_K):
        onehot = jnp.where(lane == top_i[k], 1.0, onehot)
    total = jnp.sum(onehot, axis=0, keepdims=True)
    segments = jnp.floor((total + (SEG_ROWS - 1)) * (1.0 / SEG_ROWS))
    e_r = lax.broadcasted_iota(jnp.int32, (LANES, LANES), 0)
    e_c = lax.broadcasted_iota(jnp.int32, (LANES, LANES), 1)
    before = jnp.where(e_r < e_c, 1.0, 0.0).astype(BF16)
    run_start = SEG_ROWS * _dot(jnp.broadcast_to(segments, (SUBLANES, LANES)).astype(BF16), before)[0:1, :]
    r_i = lax.broadcasted_iota(jnp.int32, (tm, tm), 0)
    c_i = lax.broadcasted_iota(jnp.int32, (tm, tm), 1)
    earlier = jnp.where(c_i < r_i, 1.0, 0.0).astype(BF16)
    pos = _dot(earlier, onehot.astype(BF16)) + run_start
    pos_out = jnp.zeros(logits.shape, F32)
    for k in range(TOP_K):
        pk = jnp.sum(jnp.where(lane == top_i[k], pos, 0.0), axis=-1, keepdims=True)
        pos_out = jnp.where(lane == float(k), pk, pos_out)
    pos_ref[...] = (pos_out * ROW_TILE).T[:SUBLANES, :].astype(jnp.int32)
    sub = lax.broadcasted_iota(jnp.int32, (SUBLANES, LANES), 0)
    cnt_ref[...] = jnp.where(sub == 1, run_start, total).astype(jnp.int32)


def _cross_router(x1, kv, w_q, w_o, ln_g, ln_b, router_w, router_b, batch, seq, mem_len, tm=MOE_TILE):
    n, d = x1.shape
    nt = seq // tm
    pad = LANES - N_EXPERTS
    rw = jnp.pad(router_w, ((0, 0), (0, pad)))
    rw_hi = rw.astype(BF16)
    rw_lo = (rw - rw_hi.astype(F32)).astype(BF16)
    rb = jnp.pad(router_b, (0, pad), constant_values=-jnp.inf).reshape(1, LANES)
    const = lambda b, i: (0, 0)
    row = lambda b, i: (b * nt + i, 0)
    return pl.pallas_call(
        _cross_router_kernel,
        out_shape=(jax.ShapeDtypeStruct((n, d), F32),
                   jax.ShapeDtypeStruct((n // tm * SUBLANES, tm), F32),
                   jax.ShapeDtypeStruct((n // tm * SUBLANES, tm), jnp.int32),
                   jax.ShapeDtypeStruct((n // tm * SUBLANES, LANES), jnp.int32)),
        grid=(batch, nt),
        in_specs=[pl.BlockSpec((tm, d), row),
                  pl.BlockSpec((d, d), const),
                  pl.BlockSpec((d, d), const),
                  pl.BlockSpec((mem_len, d), lambda b, i: (b, 0)),
                  pl.BlockSpec((mem_len, d), lambda b, i: (b, 1)),
                  pl.BlockSpec((1, d), const),
                  pl.BlockSpec((1, d), const),
                  pl.BlockSpec((d, LANES), const),
                  pl.BlockSpec((d, LANES), const),
                  pl.BlockSpec((1, LANES), const)],
        out_specs=(pl.BlockSpec((tm, d), row),
                   pl.BlockSpec((SUBLANES, tm), row),
                   pl.BlockSpec((SUBLANES, tm), row),
                   pl.BlockSpec((SUBLANES, LANES), row)),
        compiler_params=_params("parallel", "parallel"),
        name="cross_attn_router",
    )(x1, w_q.astype(BF16), w_o.astype(BF16), kv, kv, ln_g.reshape(1, d), ln_b.reshape(1, d), rw_hi, rw_lo, rb)


def _to_row_tiles(dst_ref, x, rows, first_row=0):
    for c in range(ROW_TILE):
        dst_ref[pl.ds(first_row * ROW_TILE + c, rows, stride=ROW_TILE), :] = x[:, c * LANES:(c + 1) * LANES]


def _from_row_tiles(src_ref, rows, first_row=0):
    return jnp.concatenate([src_ref[pl.ds(first_row * ROW_TILE + c, rows, stride=ROW_TILE), :]
                            for c in range(ROW_TILE)], axis=-1)


def _tile_rows(row):
    return pl.multiple_of(row * ROW_TILE, ROW_TILE)


def _num_segments(count):
    return lax.shift_right_logical(count + (SEG_ROWS - 1), SEG_ROWS.bit_length() - 1)


def _tile_segments(tcnt_ref):
    return lax.fori_loop(0, N_EXPERTS, lambda e, t: t + _num_segments(tcnt_ref[0, 0, e]), 0)


def _dispatch_kernel(pstart_ref, counts_ref, padded_ref, ldest_ref, tcnt_ref, tbase_ref, lstart_ref, x_ref,
                     xs_hbm, xrt, stage, zeros, nseg, sems, zsem, rsem, *, tm, nt):
    i = pl.program_id(0)
    slot = lax.rem(i, 2)

    @pl.when(i == 0)
    def _():
        stage[...] = jnp.zeros_like(stage)
        zeros[...] = jnp.zeros_like(zeros)

    _to_row_tiles(xrt, x_ref[...], tm)
    st = stage.at[slot]
    for k in range(TOP_K):
        def move(r, c, k=k):
            dst = pl.multiple_of(ldest_ref[0, 0, k * tm + r], ROW_TILE)
            st[pl.ds(dst, ROW_TILE), :] = xrt[pl.ds(_tile_rows(r), ROW_TILE), :]
            return c

        lax.fori_loop(0, tm, move, 0, unroll=32)

    def seg_copy(s, src_row, dst_row):
        return pltpu.make_async_copy(stage.at[s, pl.ds(_tile_rows(src_row), SEG_ROWS * ROW_TILE), :],
                                     xs_hbm.at[pl.ds(_tile_rows(dst_row), SEG_ROWS * ROW_TILE), :], sems.at[s])

    def wait_segments(s, count):
        def wait(j, c):
            seg_copy(s, 0, 0).wait()
            return c

        lax.fori_loop(0, count, wait, 0)

    @pl.when(i > 0)
    def _():
        wait_segments(1 - slot, nseg[1 - slot])

    def issue(e, total):
        n = _num_segments(tcnt_ref[0, 0, e])
        src0 = lstart_ref[0, 0, e]
        dst0 = pstart_ref[e] + tbase_ref[0, 0, e]

        def one(j, c):
            seg_copy(slot, src0 + j * SEG_ROWS, dst0 + j * SEG_ROWS).start()
            return c

        lax.fori_loop(0, n, one, 0)
        return total + n

    total = lax.fori_loop(0, N_EXPERTS, issue, 0)
    nseg[slot] = total

    @pl.when(i == nt - 1)
    def _():
        wait_segments(slot, total)

        def gap(e):
            lo = pstart_ref[e] + counts_ref[e]
            width = padded_ref[e] - counts_ref[e]
            full = lax.shift_right_logical(width, SEG_ROWS.bit_length() - 1)
            return lo, full, width - full * SEG_ROWS

        def seg_zero(row):
            return pltpu.make_async_copy(zeros, xs_hbm.at[pl.ds(_tile_rows(row), SEG_ROWS * ROW_TILE), :], zsem)

        def row_zero(row):
            return pltpu.make_async_copy(zeros.at[pl.ds(0, ROW_TILE), :],
                                         xs_hbm.at[pl.ds(_tile_rows(row), ROW_TILE), :], rsem)

        def fill(e, c):
            lo, full, rest = gap(e)
            lax.fori_loop(0, full, lambda j, c2: (seg_zero(lo + j * SEG_ROWS).start(), c2)[1], 0)
            lax.fori_loop(0, rest, lambda j, c2: (row_zero(lo + full * SEG_ROWS + j).start(), c2)[1], 0)
            return c

        def fill_wait(e, c):
            _, full, rest = gap(e)
            lax.fori_loop(0, full, lambda j, c2: (seg_zero(0).wait(), c2)[1], 0)
            lax.fori_loop(0, rest, lambda j, c2: (row_zero(0).wait(), c2)[1], 0)
            return c

        lax.fori_loop(0, N_EXPERTS, fill, 0)
        lax.fori_loop(0, N_EXPERTS, fill_wait, 0)


def _stage_rows(tm):
    return TOP_K * tm + N_EXPERTS * SEG_ROWS


def _dispatch(x2, pstart, counts, padded, ldest_t, tcnt_t, tbase_t, lstart_t, n_pad, tm):
    n, d = x2.shape
    nt = n // tm
    smem_tile = lambda width: pl.BlockSpec((1, 1, width), lambda i, *_: (i, 0, 0), memory_space=pltpu.SMEM)
    grid_spec = pltpu.PrefetchScalarGridSpec(
        num_scalar_prefetch=3,
        grid=(nt,),
        in_specs=[smem_tile(TOP_K * tm), smem_tile(LANES), smem_tile(LANES), smem_tile(LANES),
                  pl.BlockSpec((tm, d), lambda i, *_: (i, 0))],
        out_specs=pl.BlockSpec(memory_space=pl.ANY),
        scratch_shapes=[pltpu.VMEM((tm * ROW_TILE, LANES), F32),
                        pltpu.VMEM((2, _stage_rows(tm) * ROW_TILE, LANES), F32),
                        pltpu.VMEM((SEG_ROWS * ROW_TILE, LANES), F32),
                        pltpu.SMEM((2,), jnp.int32),
                        pltpu.SemaphoreType.DMA((2,)),
                        pltpu.SemaphoreType.DMA,
                        pltpu.SemaphoreType.DMA],
    )
    return pl.pallas_call(
        functools.partial(_dispatch_kernel, tm=tm, nt=nt),
        out_shape=jax.ShapeDtypeStruct((n_pad * ROW_TILE, LANES), F32),
        grid_spec=grid_spec,
        compiler_params=_params("arbitrary"),
        name="moe_dispatch",
    )(pstart, counts, padded, ldest_t, tcnt_t, tbase_t, lstart_t, x2)


def _expert_kernel(be_ref, valid_ref, xs_ref, wup_ref, bup_ref, wdn_ref, bdn_ref, y_ref, wup_bf, wdn_bf):
    i = pl.program_id(0)

    @pl.when(valid_ref[i] == 0)
    def _():
        y_ref[...] = jnp.zeros_like(y_ref)

    @pl.when(valid_ref[i] > 0)
    def _():
        prev = be_ref[jnp.maximum(i - 1, 0)]

        @pl.when(jnp.logical_or(i == 0, be_ref[i] != prev))
        def _():
            wup_bf[...] = wup_ref[0].astype(BF16)
            wdn_bf[...] = wdn_ref[0].astype(BF16)

        def expert_rows(first_row):
            xb = _from_row_tiles(xs_ref, MOE_HALF, first_row).astype(BF16)
            h = _dot(xb, wup_bf[...]) + bup_ref[0]
            glu = jnp.minimum(h[:, :D_FF], SWIGLU_LIMIT)
            lin = jnp.clip(h[:, D_FF:], -SWIGLU_LIMIT, SWIGLU_LIMIT)
            act = glu * _sigmoid(SWIGLU_ALPHA * glu) * (lin + 1.0)
            _to_row_tiles(y_ref, _dot(act.astype(BF16), wdn_bf[...]) + bdn_ref[0], MOE_HALF, first_row)

        @pl.when(valid_ref[i] > MOE_HALF)
        def _():
            expert_rows(0)
            expert_rows(MOE_HALF)

        @pl.when(valid_ref[i] <= MOE_HALF)
        def _():
            expert_rows(0)
            y_ref[MOE_HALF * ROW_TILE:, :] = jnp.zeros((MOE_HALF * ROW_TILE, LANES), F32)


def _experts(xs, blk_expert, blk_valid, w_up, b_up, w_down, b_down):
    n_blocks = blk_expert.shape[0]
    d = w_up.shape[1]
    f2 = w_up.shape[2]
    blk_rows = MOE_BLOCK * ROW_TILE
    grid_spec = pltpu.PrefetchScalarGridSpec(
        num_scalar_prefetch=2,
        grid=(n_blocks,),
        in_specs=[pl.BlockSpec((blk_rows, LANES), lambda i, be, va: (jnp.where(va[i] > 0, i, 0), 0)),
                  pl.BlockSpec((1, d, f2), lambda i, be, va: (be[i], 0, 0)),
                  pl.BlockSpec((1, 1, f2), lambda i, be, va: (be[i], 0, 0)),
                  pl.BlockSpec((1, D_FF, d), lambda i, be, va: (be[i], 0, 0)),
                  pl.BlockSpec((1, 1, d), lambda i, be, va: (be[i], 0, 0))],
        out_specs=pl.BlockSpec((blk_rows, LANES), lambda i, be, va: (i, 0)),
        scratch_shapes=[pltpu.VMEM((d, f2), BF16),
                        pltpu.VMEM((D_FF, d), BF16)],
    )
    return pl.pallas_call(
        _expert_kernel,
        out_shape=jax.ShapeDtypeStruct((n_blocks * blk_rows, LANES), F32),
        grid_spec=grid_spec,
        compiler_params=_params("arbitrary"),
        name="moe_experts",
    )(blk_expert, blk_valid, xs, w_up, b_up.reshape(N_EXPERTS, 1, f2), w_down, b_down.reshape(N_EXPERTS, 1, d))


def _combine_ln_kernel(pstart_ref, ldest_ref, gate_ref, tcnt_ref, tbase_ref, lstart_ref,
                       tcnt_next_ref, tbase_next_ref, lstart_next_ref, ys_hbm, x_ref, g_ref, b_ref,
                       o_ref, stage, yrt, sems, *, tm, nt):
    i = pl.program_id(0)
    slot = lax.rem(i, 2)

    def seg_copy(s, src_row, dst_row):
        return pltpu.make_async_copy(ys_hbm.at[pl.ds(_tile_rows(src_row), SEG_ROWS * ROW_TILE), :],
                                     stage.at[s, pl.ds(_tile_rows(dst_row), SEG_ROWS * ROW_TILE), :], sems.at[s])

    def fetch(s, tcnt, tbase, lstart):
        def per_expert(e, c):
            src0 = pstart_ref[e] + tbase[0, 0, e]
            dst0 = lstart[0, 0, e]

            def one(j, c2):
                seg_copy(s, src0 + j * SEG_ROWS, dst0 + j * SEG_ROWS).start()
                return c2

            lax.fori_loop(0, _num_segments(tcnt[0, 0, e]), one, 0)
            return c

        lax.fori_loop(0, N_EXPERTS, per_expert, 0)

    @pl.when(i == 0)
    def _():
        fetch(0, tcnt_ref, tbase_ref, lstart_ref)

    @pl.when(i + 1 < nt)
    def _():
        fetch(1 - slot, tcnt_next_ref, tbase_next_ref, lstart_next_ref)

    def wait(j, c):
        seg_copy(slot, 0, 0).wait()
        return c

    lax.fori_loop(0, _tile_segments(tcnt_ref), wait, 0)

    st = stage.at[slot]

    def reduce(r, c):
        acc = None
        for k in range(TOP_K):
            row = st[pl.ds(pl.multiple_of(ldest_ref[0, 0, k * tm + r], ROW_TILE), ROW_TILE), :]
            term = gate_ref[0, 0, k * tm + r] * row
            acc = term if acc is None else acc + term
        yrt[pl.ds(_tile_rows(r), ROW_TILE), :] = acc
        return c

    lax.fori_loop(0, tm, reduce, 0, unroll=16)
    y = _from_row_tiles(yrt, tm)
    o_ref[...] = _layer_norm(ALPHA * x_ref[...] + y, g_ref[...], b_ref[...])


def _combine_ln(ys, pstart, ldest_t, gates_t, tcnt_t, tbase_t, lstart_t, x2, ln_g, ln_b, tm):
    n, d = x2.shape
    nt = n // tm
    cur = lambda width: pl.BlockSpec((1, 1, width), lambda i, *_: (i, 0, 0), memory_space=pltpu.SMEM)
    nxt = lambda width: pl.BlockSpec((1, 1, width), lambda i, *_: (jnp.minimum(i + 1, nt - 1), 0, 0),
                                     memory_space=pltpu.SMEM)
    grid_spec = pltpu.PrefetchScalarGridSpec(
        num_scalar_prefetch=1,
        grid=(nt,),
        in_specs=[cur(TOP_K * tm), cur(TOP_K * tm), cur(LANES), cur(LANES), cur(LANES),
                  nxt(LANES), nxt(LANES), nxt(LANES),
                  pl.BlockSpec(memory_space=pl.ANY),
                  pl.BlockSpec((tm, d), lambda i, *_: (i, 0)),
                  pl.BlockSpec((1, d), lambda i, *_: (0, 0)),
                  pl.BlockSpec((1, d), lambda i, *_: (0, 0))],
        out_specs=pl.BlockSpec((tm, d), lambda i, *_: (i, 0)),
        scratch_shapes=[pltpu.VMEM((2, _stage_rows(tm) * ROW_TILE, LANES), F32),
                        pltpu.VMEM((tm * ROW_TILE, LANES), F32),
                        pltpu.SemaphoreType.DMA((2,))],
    )
    return pl.pallas_call(
        functools.partial(_combine_ln_kernel, tm=tm, nt=nt),
        out_shape=jax.ShapeDtypeStruct((n, d), F32),
        grid_spec=grid_spec,
        compiler_params=_params("arbitrary"),
        name="moe_combine_ln3",
    )(pstart, ldest_t, gates_t, tcnt_t, tbase_t, lstart_t, tcnt_t, tbase_t, lstart_t, ys, x2,
      ln_g.reshape(1, d), ln_b.reshape(1, d))


def _round_up(v, m):
    return (v + m - 1) // m * m


def _moe_layer(x2, pos_pad, tile_pad, gates_pad, w_up, b_up, w_down, b_down, ln_g, ln_b, tm=MOE_TILE):
    n = x2.shape[0]
    nt = n // tm
    i32 = jnp.int32
    tile_info = tile_pad.reshape(nt, SUBLANES, LANES)
    tcnt = tile_info[:, 0, :N_EXPERTS]
    lstart = tile_info[:, 1, :N_EXPERTS]
    tbase = jnp.cumsum(tcnt, axis=0) - tcnt
    counts = jnp.sum(tcnt, axis=0)
    padded = _round_up(counts + SEG_ROWS, MOE_BLOCK)
    pend = jnp.cumsum(padded)
    pstart = pend - padded
    n_blocks = -(-(n * TOP_K + N_EXPERTS * (SEG_ROWS + MOE_BLOCK - 1)) // MOE_BLOCK)
    block_start = jnp.arange(n_blocks, dtype=i32) * MOE_BLOCK
    blk_expert = jnp.minimum(jnp.sum(pend[None, :] <= block_start[:, None], axis=1), N_EXPERTS - 1).astype(i32)
    blk_valid = jnp.clip(counts[blk_expert] - (block_start - pstart[blk_expert]), 0, MOE_BLOCK).astype(i32)
    per_tile = lambda a: a.reshape(nt, SUBLANES, tm)[:, :TOP_K, :].reshape(nt, 1, TOP_K * tm)
    ldest_t = per_tile(pos_pad)
    gates_t = per_tile(gates_pad)
    lane_pad = lambda a: jnp.pad(a.astype(i32), ((0, 0), (0, LANES - N_EXPERTS))).reshape(nt, 1, LANES)
    tcnt_t, tbase_t, lstart_t = lane_pad(tcnt), lane_pad(tbase), lane_pad(lstart)
    pstart, counts, padded = pstart.astype(i32), counts.astype(i32), padded.astype(i32)

    xs = _dispatch(x2, pstart, counts, padded, ldest_t, tcnt_t, tbase_t, lstart_t, n_blocks * MOE_BLOCK, tm)
    ys = _experts(xs, blk_expert, blk_valid, w_up, b_up, w_down, b_down)
    return _combine_ln(ys, pstart, ldest_t, gates_t, tcnt_t, tbase_t, lstart_t, x2, ln_g, ln_b, tm)


def _s5_kernel(x_ref, win_ref, bblk_ref, ar_ref, ai_ref, cblk_ref, dsk_ref, wval_ref, wgate_ref, g_ref, b_ref,
               o_ref, bur, bui, sr, si, *, tt, batch):
    rows = tt * batch

    @pl.when(pl.program_id(0) == 0)
    def _():
        sr[...] = jnp.zeros_like(sr)
        si[...] = jnp.zeros_like(si)

    x = x_ref[...]
    u = _dot(x.astype(BF16), win_ref[...])
    ub = u.astype(BF16)
    for c in range(S5_N_CHUNKS):
        bu = _dot(ub[:, c * S5_CHUNK_IN:(c + 1) * S5_CHUNK_IN], bblk_ref[c])
        bur[:, c * S5_CHUNK_STATE:(c + 1) * S5_CHUNK_STATE] = bu[:, :S5_CHUNK_STATE]
        bui[:, c * S5_CHUNK_STATE:(c + 1) * S5_CHUNK_STATE] = bu[:, S5_CHUNK_STATE:]

    for c in range(S5_N_CHUNKS):
        cols = pl.ds(c * S5_CHUNK_STATE, S5_CHUNK_STATE)
        a_r = ar_ref[:, cols]
        a_i = ai_ref[:, cols]

        def step(t, carry):
            s_r, s_i = carry
            rsl = pl.ds(pl.multiple_of(t * batch, batch), batch)
            n_r = a_r * s_r - a_i * s_i + bur[rsl, cols]
            n_i = a_r * s_i + a_i * s_r + bui[rsl, cols]
            bur[rsl, cols] = n_r
            bui[rsl, cols] = n_i
            return n_r, n_i

        f_r, f_i = lax.fori_loop(0, tt, step, (sr[:, cols], si[:, cols]), unroll=True)
        sr[:, cols] = f_r
        si[:, cols] = f_i

    ys = []
    for c in range(S5_N_CHUNKS):
        cols = pl.ds(c * S5_CHUNK_STATE, S5_CHUNK_STATE)
        xri = jnp.concatenate([bur[:, cols].astype(BF16), bui[:, cols].astype(BF16)], axis=-1)
        ys.append(_dot(xri, cblk_ref[c]))
    y = jnp.concatenate(ys, axis=-1) + dsk_ref[...] * u
    yb = _gelu(y).astype(BF16)
    hmix = _dot(yb, wval_ref[...]) * _sigmoid(_dot(yb, wgate_ref[...]))
    o_ref[...] = _layer_norm(ALPHA * x + hmix, g_ref[...], b_ref[...])


def _s5_discretize(log_dt, lambda_re, lambda_im, b_re, b_im, c_re, c_im):
    dt = jnp.exp(log_dt)[:, None]
    mag = jnp.exp(lambda_re * dt)
    ar = mag * jnp.cos(lambda_im * dt)
    ai = mag * jnp.sin(lambda_im * dt)
    den = lambda_re * lambda_re + lambda_im * lambda_im
    zr = ((ar - 1.0) * lambda_re + ai * lambda_im) / den
    zi = (ai * lambda_re - (ar - 1.0) * lambda_im) / den
    bbar_re = zr[..., None] * b_re - zi[..., None] * b_im
    bbar_im = zr[..., None] * b_im + zi[..., None] * b_re
    ng, gc, p, gw = S5_N_CHUNKS, S5_CHUNK_GROUPS, S5_STATE, S5_GROUP
    eye = jnp.eye(gc, dtype=F32)
    bre = bbar_re.reshape(ng, gc, p, gw).transpose(0, 1, 3, 2)
    bim = bbar_im.reshape(ng, gc, p, gw).transpose(0, 1, 3, 2)
    blk_re = jnp.einsum('cgip,gh->cgihp', bre, eye).reshape(ng, gc * gw, gc * p)
    blk_im = jnp.einsum('cgip,gh->cgihp', bim, eye).reshape(ng, gc * gw, gc * p)
    bblk = jnp.concatenate([blk_re, blk_im], axis=-1).astype(BF16)
    cre = c_re.reshape(ng, gc, gw, p).transpose(0, 1, 3, 2)
    cim = c_im.reshape(ng, gc, gw, p).transpose(0, 1, 3, 2)
    cblk_re = jnp.einsum('cgpi,gh->cgphi', cre, eye).reshape(ng, gc * p, gc * gw)
    cblk_im = jnp.einsum('cgpi,gh->cgphi', cim, eye).reshape(ng, gc * p, gc * gw)
    cblk = jnp.concatenate([cblk_re, -cblk_im], axis=1).astype(BF16)
    return bblk, ar.reshape(1, S5_STATES), ai.reshape(1, S5_STATES), cblk


def _s5_mixer_ln(x_tm, w_in, log_dt, lambda_re, lambda_im, b_re, b_im, c_re, c_im, d_skip, w_val, w_gate,
                 ln_g, ln_b, batch, tt=16):
    n, d = x_tm.shape
    rows = tt * batch
    bblk, ar, ai, cblk = _s5_discretize(log_dt, lambda_re, lambda_im, b_re, b_im, c_re, c_im)
    const2 = lambda i: (0, 0)
    const3 = lambda i: (0, 0, 0)
    row = lambda i: (i, 0)
    return pl.pallas_call(
        functools.partial(_s5_kernel, tt=tt, batch=batch),
        out_shape=jax.ShapeDtypeStruct((n, d), F32),
        grid=(n // rows,),
        in_specs=[pl.BlockSpec((rows, d), row),
                  pl.BlockSpec((d, d), const2),
                  pl.BlockSpec(bblk.shape, const3),
                  pl.BlockSpec(ar.shape, const2),
                  pl.BlockSpec(ai.shape, const2),
                  pl.BlockSpec(cblk.shape, const3),
                  pl.BlockSpec((1, d), const2),
                  pl.BlockSpec((d, d), const2),
                  pl.BlockSpec((d, d), const2),
                  pl.BlockSpec((1, d), const2),
                  pl.BlockSpec((1, d), const2)],
        out_specs=pl.BlockSpec((rows, d), row),
        scratch_shapes=[pltpu.VMEM((rows, S5_STATES), F32),
                        pltpu.VMEM((rows, S5_STATES), F32),
                        pltpu.VMEM((batch, S5_STATES), F32),
                        pltpu.VMEM((batch, S5_STATES), F32)],
        compiler_params=_params("arbitrary"),
        name="s5_mixer_ln1",
    )(x_tm, w_in.astype(BF16), bblk, ar, ai, cblk, d_skip.reshape(1, d), w_val.astype(BF16), w_gate.astype(BF16),
      ln_g.reshape(1, d), ln_b.reshape(1, d))


def _diff_lambda_init(layer_idx):
    return 0.8 - 0.6 * math.exp(-0.3 * layer_idx)


def kernel(x, mem, w_mem_kv, l0_w_in, l0_sgu_ln_g, l0_sgu_ln_b, l0_w_spatial, l0_b_spatial, l0_lam_q1, l0_lam_k1, l0_lam_q2, l0_lam_k2, l0_subln_g, l0_w_out, l0_ln1_g, l0_ln1_b, l0_xq, l0_xo, l0_ln2_g, l0_ln2_b, l0_router_w, l0_router_b, l0_exp_w_up, l0_exp_b_up, l0_exp_w_down, l0_exp_b_down, l0_ln3_g, l0_ln3_b, l1_w_in, l1_log_dt, l1_lambda_re, l1_lambda_im, l1_b_re, l1_b_im, l1_c_re, l1_c_im, l1_d_skip, l1_w_val, l1_w_gate, l1_ln1_g, l1_ln1_b, l1_xq, l1_xo, l1_ln2_g, l1_ln2_b, l1_router_w, l1_router_b, l1_exp_w_up, l1_exp_b_up, l1_exp_w_down, l1_exp_b_down, l1_ln3_g, l1_ln3_b):
    batch, seq, d = x.shape
    mem_len = mem.shape[1]
    n = batch * seq
    x0 = x.reshape(n, d)

    kv = _matmul(mem.reshape(batch * mem_len, d), w_mem_kv.astype(BF16), BF16,
                 tm=min(512, batch * mem_len), tn=d)

    a, qk, vt = _inproj_sgu(x0, l0_w_in, l0_sgu_ln_g, l0_sgu_ln_b, l0_w_spatial, l0_b_spatial)
    dattn = _diff_attention(qk, vt, l0_lam_q1, l0_lam_k1, l0_lam_q2, l0_lam_k2, l0_subln_g,
                            _diff_lambda_init(0), batch, seq)
    x1 = _outproj_ln(a, dattn, l0_w_out, x0, l0_ln1_g, l0_ln1_b)
    x2, gates, pos, tile_info = _cross_router(x1, kv, l0_xq, l0_xo, l0_ln2_g, l0_ln2_b, l0_router_w, l0_router_b,
                                              batch, seq, mem_len)
    x3 = _moe_layer(x2, pos, tile_info, gates, l0_exp_w_up, l0_exp_b_up, l0_exp_w_down, l0_exp_b_down,
                    l0_ln3_g, l0_ln3_b)

    x3_tm = x3.reshape(batch, seq, d).transpose(1, 0, 2).reshape(n, d)
    x4_tm = _s5_mixer_ln(x3_tm, l1_w_in, l1_log_dt, l1_lambda_re, l1_lambda_im, l1_b_re, l1_b_im,
                         l1_c_re, l1_c_im, l1_d_skip, l1_w_val, l1_w_gate, l1_ln1_g, l1_ln1_b, batch)
    x4 = x4_tm.reshape(seq, batch, d).transpose(1, 0, 2).reshape(n, d)
    x5, gates, pos, tile_info = _cross_router(x4, kv, l1_xq, l1_xo, l1_ln2_g, l1_ln2_b, l1_router_w, l1_router_b,
                                              batch, seq, mem_len)
    x6 = _moe_layer(x5, pos, tile_info, gates, l1_exp_w_up, l1_exp_b_up, l1_exp_w_down, l1_exp_b_down,
                    l1_ln3_g, l1_ln3_b)
    return x6.reshape(batch, seq, d)
```

```python
import functools
import math

import jax
import jax.numpy as jnp
from jax import lax
from jax.experimental import pallas as pl
from jax.experimental.pallas import tpu as pltpu

F32 = jnp.float32
BF16 = jnp.bfloat16

D_MODEL = 1024
CHUNK = 128
SGU_GROUPS = 4
SGU_WIDTH = 512
DIFF_HEADS = 4
DIFF_HEAD_DIM = 64
DIFF_V_DIM = 128
DIFF_QK_WIDTH = DIFF_HEADS * 2 * DIFF_HEAD_DIM
S5_GROUP = 16
S5_GROUPS = 64
S5_STATE = 64
X_HEADS = 4
X_HEAD_DIM = 256
N_EXPERTS = 32
TOP_K = 4
D_FF = 1024
SWIGLU_LIMIT = 7.0
SWIGLU_ALPHA = 1.702
MOE_BLOCK = 512
MOE_HALF = MOE_BLOCK // 2
MOE_TILE = 512
EXPERT_RING = 3
DEPTH = 2
ALPHA = (2 * DEPTH) ** 0.25
LN_EPS = 1e-5
NEG_INF = -1e30

LANES = 128
SUBLANES = 8
VMEM_LIMIT_BYTES = 56 * 1024 * 1024
ROW_TILE = D_MODEL // LANES
SEG_ROWS = 64

S5_CHUNK_GROUPS = 8
S5_CHUNK_IN = S5_CHUNK_GROUPS * S5_GROUP
S5_CHUNK_STATE = S5_CHUNK_GROUPS * S5_STATE
S5_N_CHUNKS = S5_GROUPS // S5_CHUNK_GROUPS
S5_STATES = S5_GROUPS * S5_STATE


def _params(*sem):
    return pltpu.CompilerParams(dimension_semantics=sem, vmem_limit_bytes=VMEM_LIMIT_BYTES)


def _gelu(x):
    return 0.5 * x * (1.0 + jnp.tanh(math.sqrt(2.0 / math.pi) * (x + 0.044715 * (x * x * x))))


def _sigmoid(x):
    return 1.0 / (1.0 + jnp.exp(-x))


def _layer_norm(z, g, b):
    mu = jnp.mean(z, axis=-1, keepdims=True)
    zc = z - mu
    var = jnp.mean(zc * zc, axis=-1, keepdims=True)
    return zc * lax.rsqrt(var + LN_EPS) * g + b


def _dot(a, b):
    return jnp.dot(a, b, preferred_element_type=F32)


def _dot_nt(a, b):
    return lax.dot_general(a, b, (((1,), (1,)), ((), ())), preferred_element_type=F32)


def _matmul_kernel(x_ref, w_ref, o_ref):
    o_ref[...] = _dot(x_ref[...].astype(BF16), w_ref[...]).astype(o_ref.dtype)


def _matmul(x, w, out_dtype, tm, tn):
    m, k = x.shape
    n = w.shape[1]
    return pl.pallas_call(
        _matmul_kernel,
        out_shape=jax.ShapeDtypeStruct((m, n), out_dtype),
        grid=(n // tn, m // tm),
        in_specs=[pl.BlockSpec((tm, k), lambda j, i: (i, 0)),
                  pl.BlockSpec((k, tn), lambda j, i: (0, j))],
        out_specs=pl.BlockSpec((tm, tn), lambda j, i: (i, j)),
        compiler_params=_params("parallel", "parallel"),
        name="matmul",
    )(x, w)


def _inproj_sgu_kernel(x_ref, w_ref, lng_ref, lnb_ref, wsp_ref, bsp_ref, a_ref, qk_ref, vt_ref, *, tm):
    xb = x_ref[...].astype(BF16)
    qk_ref[...] = _dot(xb, w_ref[:, 2 * SGU_WIDTH:2 * SGU_WIDTH + 2 * DIFF_QK_WIDTH]).astype(BF16)
    vt_ref[...] = _dot(xb, w_ref[:, 2 * SGU_WIDTH + 2 * DIFF_QK_WIDTH:]).T.astype(BF16)
    h = _dot(xb, w_ref[:, :2 * SGU_WIDTH])
    for g in range(SGU_GROUPS):
        lo = g * CHUNK
        u = _gelu(h[:, lo:lo + CHUNK])
        v = _gelu(h[:, SGU_WIDTH + lo:SGU_WIDTH + lo + CHUNK])
        vn = _layer_norm(v, lng_ref[g:g + 1, :], lnb_ref[g:g + 1, :]).astype(BF16)
        w_g = wsp_ref[g]
        b_g = bsp_ref[:, g:g + 1]
        for c in range(tm // CHUNK):
            r = c * CHUNK
            gate = _dot(w_g, vn[r:r + CHUNK, :]) + b_g
            a_ref[r:r + CHUNK, lo:lo + CHUNK] = (u[r:r + CHUNK, :] * gate).astype(BF16)


def _inproj_sgu(x2d, w_in, ln_g, ln_b, w_spatial, b_spatial, tm=512):
    n, d = x2d.shape
    w = w_in.astype(BF16)
    wsp = jnp.tril(w_spatial).astype(BF16)
    bsp_t = b_spatial.T
    v_width = w.shape[1] - 2 * SGU_WIDTH - 2 * DIFF_QK_WIDTH
    const = lambda i: (0, 0)
    return pl.pallas_call(
        functools.partial(_inproj_sgu_kernel, tm=tm),
        out_shape=(jax.ShapeDtypeStruct((n, SGU_WIDTH), BF16),
                   jax.ShapeDtypeStruct((n, 2 * DIFF_QK_WIDTH), BF16),
                   jax.ShapeDtypeStruct((v_width, n), BF16)),
        grid=(n // tm,),
        in_specs=[pl.BlockSpec((tm, d), lambda i: (i, 0)),
                  pl.BlockSpec(w.shape, const),
                  pl.BlockSpec((SGU_GROUPS, CHUNK), const),
                  pl.BlockSpec((SGU_GROUPS, CHUNK), const),
                  pl.BlockSpec(wsp.shape, lambda i: (0, 0, 0)),
                  pl.BlockSpec(bsp_t.shape, const)],
        out_specs=(pl.BlockSpec((tm, SGU_WIDTH), lambda i: (i, 0)),
                   pl.BlockSpec((tm, 2 * DIFF_QK_WIDTH), lambda i: (i, 0)),
                   pl.BlockSpec((v_width, tm), lambda i: (0, i))),
        compiler_params=_params("parallel"),
        name="inproj_sgu",
    )(x2d, w, ln_g.reshape(SGU_GROUPS, CHUNK), ln_b.reshape(SGU_GROUPS, CHUNK), wsp, bsp_t)


def _diff_attn_kernel(q_ref, k_ref, vt_ref, lam_ref, g_ref, o_ref, *, tq, lam_init):
    i = pl.program_id(1)
    hw = 2 * DIFF_HEAD_DIM
    lane = lax.broadcasted_iota(jnp.int32, (1, hw), 1)
    queries = []
    for h in range(DIFF_HEADS):
        q = q_ref[:, h * hw:(h + 1) * hw] * jnp.asarray(DIFF_HEAD_DIM ** -0.5, BF16)
        zero = jnp.zeros_like(q)
        queries.append((h, jnp.where(lane < DIFF_HEAD_DIM, q, zero)))
        queries.append((h, jnp.where(lane >= DIFF_HEAD_DIM, q, zero)))

    def block(j, carry, masked):
        off = pl.multiple_of(j * tq, tq)
        scores = [_dot_nt(k_ref[pl.ds(off, tq), h * hw:(h + 1) * hw], qc) for h, qc in queries]
        if masked:
            key = lax.broadcasted_iota(jnp.int32, (tq, tq), 0)
            qry = lax.broadcasted_iota(jnp.int32, (tq, tq), 1)
            scores = [jnp.where(key <= qry, st, NEG_INF) for st in scores]
        stats, probs = [], []
        for (m, l, _), st in zip(carry, scores):
            m_new = jnp.maximum(m, jnp.max(st, axis=0, keepdims=True))
            alpha = jnp.exp(m - m_new)
            p = jnp.exp(st - m_new)
            stats.append((m_new, alpha * l + jnp.sum(p, axis=0, keepdims=True), alpha))
            probs.append(p.astype(BF16))
        pv = [_dot(vt_ref[h * DIFF_V_DIM:(h + 1) * DIFF_V_DIM, pl.ds(off, tq)], p)
              for (h, _), p in zip(queries, probs)]
        return tuple((m_new, l_new, alpha * acc + o)
                     for (m_new, l_new, alpha), (_, _, acc), o in zip(stats, carry, pv))

    init = (jnp.full((1, tq), NEG_INF, F32), jnp.zeros((1, tq), F32), jnp.zeros((DIFF_V_DIM, tq), F32))
    carry = lax.fori_loop(0, i, lambda j, c: block(j, c, False), (init,) * len(queries))
    final = block(i, carry, True)

    lam_v = lam_ref[...]
    s_a = jnp.sum(lam_v[0:1, :] * lam_v[1:2, :], axis=-1, keepdims=True)
    s_b = jnp.sum(lam_v[2:3, :] * lam_v[3:4, :], axis=-1, keepdims=True)
    lam = jnp.exp(s_a) - jnp.exp(s_b) + lam_init
    for h in range(DIFF_HEADS):
        (_, l1, a1), (_, l2, a2) = final[2 * h], final[2 * h + 1]
        o_t = a1 / l1 - lam * (a2 / l2)
        o_t = o_t * lax.rsqrt(jnp.mean(o_t * o_t, axis=0, keepdims=True) + LN_EPS) * g_ref[...] * (1.0 - lam_init)
        o_ref[:, h * DIFF_V_DIM:(h + 1) * DIFF_V_DIM] = o_t.T.astype(BF16)


def _diff_attention(qk, vt, lam_q1, lam_k1, lam_q2, lam_k2, subln_g, lam_init, batch, seq, tq=256):
    n = qk.shape[0]
    nq = seq // tq
    lam_v = jnp.stack([lam_q1, lam_k1, lam_q2, lam_k2])
    v_width = DIFF_HEADS * DIFF_V_DIM
    return pl.pallas_call(
        functools.partial(_diff_attn_kernel, tq=tq, lam_init=lam_init),
        out_shape=jax.ShapeDtypeStruct((n, v_width), BF16),
        grid=(batch, nq),
        in_specs=[pl.BlockSpec((tq, DIFF_QK_WIDTH), lambda b, i: (b * nq + i, 0)),
                  pl.BlockSpec((seq, DIFF_QK_WIDTH), lambda b, i: (b, 1)),
                  pl.BlockSpec((v_width, seq), lambda b, i: (0, b)),
                  pl.BlockSpec(lam_v.shape, lambda b, i: (0, 0)),
                  pl.BlockSpec((DIFF_V_DIM, 1), lambda b, i: (0, 0))],
        out_specs=pl.BlockSpec((tq, v_width), lambda b, i: (b * nq + i, 0)),
        compiler_params=_params("parallel", "parallel"),
        name="diff_attention",
    )(qk, qk, vt, lam_v, subln_g.reshape(DIFF_V_DIM, 1))


def _outproj_ln_kernel(a_ref, d_ref, w_ref, x_ref, g_ref, b_ref, o_ref):
    y = _dot(a_ref[...], w_ref[:SGU_WIDTH, :]) + _dot(d_ref[...], w_ref[SGU_WIDTH:, :])
    o_ref[...] = _layer_norm(ALPHA * x_ref[...] + y, g_ref[...], b_ref[...])


def _outproj_ln(a, dattn, w_out, x2d, ln_g, ln_b, tm=1024):
    n, d = x2d.shape
    w = w_out.astype(BF16)
    const = lambda i: (0, 0)
    row = lambda i: (i, 0)
    return pl.pallas_call(
        _outproj_ln_kernel,
        out_shape=jax.ShapeDtypeStruct((n, d), F32),
        grid=(n // tm,),
        in_specs=[pl.BlockSpec((tm, a.shape[1]), row),
                  pl.BlockSpec((tm, dattn.shape[1]), row),
                  pl.BlockSpec(w.shape, const),
                  pl.BlockSpec((tm, d), row),
                  pl.BlockSpec((1, d), const),
                  pl.BlockSpec((1, d), const)],
        out_specs=pl.BlockSpec((tm, d), row),
        compiler_params=_params("parallel"),
        name="outproj_ln1",
    )(a, dattn, w, x2d, ln_g.reshape(1, d), ln_b.reshape(1, d))


def _cross_router_kernel(x_ref, wq_ref, wo_ref, k_ref, v_ref, g_ref, b_ref, rwh_ref, rwl_ref, rb_ref,
                         x2_ref, gate_ref, pos_ref, cnt_ref):
    x = x_ref[...]
    q = (_dot(x.astype(BF16), wq_ref[...]) * (X_HEAD_DIM ** -0.5)).astype(BF16)
    heads = []
    for h in range(X_HEADS):
        lo = h * X_HEAD_DIM
        s = _dot_nt(q[:, lo:lo + X_HEAD_DIM], k_ref[:, lo:lo + X_HEAD_DIM])
        p = jnp.exp(s - jnp.max(s, axis=-1, keepdims=True))
        p = p / jnp.sum(p, axis=-1, keepdims=True)
        heads.append(_dot(p.astype(BF16), v_ref[:, lo:lo + X_HEAD_DIM]).astype(BF16))
    o = jnp.concatenate(heads, axis=-1)
    x2 = _layer_norm(ALPHA * x + _dot(o, wo_ref[...]), g_ref[...], b_ref[...])
    x2_ref[...] = x2

    x_hi = x2.astype(BF16)
    x_lo = (x2 - x_hi.astype(F32)).astype(BF16)
    logits = _dot(x_hi, rwh_ref[...]) + _dot(x_lo, rwh_ref[...]) + _dot(x_hi, rwl_ref[...]) + rb_ref[...]
    lane = lax.broadcasted_iota(jnp.int32, logits.shape, 1).astype(F32)
    work = logits
    top_v, top_i = [], []
    for _ in range(TOP_K):
        m = jnp.max(work, axis=-1, keepdims=True)
        sel = jnp.min(jnp.where(work == m, lane, float(LANES)), axis=-1, keepdims=True)
        top_v.append(m)
        top_i.append(sel)
        work = jnp.where(lane == sel, -jnp.inf, work)
    e = [jnp.exp(v - top_v[0]) for v in top_v]
    denom = e[0] + e[1] + e[2] + e[3]
    gate_out = jnp.zeros(logits.shape, F32)
    for k in range(TOP_K):
        gate_out = jnp.where(lane == float(k), e[k] / denom, gate_out)
    gate_ref[...] = gate_out.T[:SUBLANES, :]

    tm = logits.shape[0]
    onehot = jnp.zeros(logits.shape, F32)
    for k in range(TOP_K):
        onehot = jnp.where(lane == top_i[k], 1.0, onehot)
    total = jnp.sum(onehot, axis=0, keepdims=True)
    segments = jnp.floor((total + (SEG_ROWS - 1)) * (1.0 / SEG_ROWS))
    e_r = lax.broadcasted_iota(jnp.int32, (LANES, LANES), 0)
    e_c = lax.broadcasted_iota(jnp.int32, (LANES, LANES), 1)
    before = jnp.where(e_r < e_c, 1.0, 0.0).astype(BF16)
    run_start = SEG_ROWS * _dot(jnp.broadcast_to(segments, (SUBLANES, LANES)).astype(BF16), before)[0:1, :]
    r_i = lax.broadcasted_iota(jnp.int32, (tm, tm), 0)
    c_i = lax.broadcasted_iota(jnp.int32, (tm, tm), 1)
    earlier = jnp.where(c_i < r_i, 1.0, 0.0).astype(BF16)
    pos = _dot(earlier, onehot.astype(BF16)) + run_start
    pos_out = jnp.zeros(logits.shape, F32)
    for k in range(TOP_K):
        pk = jnp.sum(jnp.where(lane == top_i[k], pos, 0.0), axis=-1, keepdims=True)
        pos_out = jnp.where(lane == float(k), pk, pos_out)
    pos_ref[...] = (pos_out * ROW_TILE).T[:SUBLANES, :].astype(jnp.int32)
    sub = lax.broadcasted_iota(jnp.int32, (SUBLANES, LANES), 0)
    cnt_ref[...] = jnp.where(sub == 1, run_start, total).astype(jnp.int32)


def _cross_router(x1, kv, w_q, w_o, ln_g, ln_b, router_w, router_b, batch, seq, mem_len, tm=MOE_TILE):
    n, d = x1.shape
    nt = seq // tm
    pad = LANES - N_EXPERTS
    rw = jnp.pad(router_w, ((0, 0), (0, pad)))
    rw_hi = rw.astype(BF16)
    rw_lo = (rw - rw_hi.astype(F32)).astype(BF16)
    rb = jnp.pad(router_b, (0, pad), constant_values=-jnp.inf).reshape(1, LANES)
    const = lambda b, i: (0, 0)
    row = lambda b, i: (b * nt + i, 0)
    return pl.pallas_call(
        _cross_router_kernel,
        out_shape=(jax.ShapeDtypeStruct((n, d), F32),
                   jax.ShapeDtypeStruct((n // tm * SUBLANES, tm), F32),
                   jax.ShapeDtypeStruct((n // tm * SUBLANES, tm), jnp.int32),
                   jax.ShapeDtypeStruct((n // tm * SUBLANES, LANES), jnp.int32)),
        grid=(batch, nt),
        in_specs=[pl.BlockSpec((tm, d), row),
                  pl.BlockSpec((d, d), const),
                  pl.BlockSpec((d, d), const),
                  pl.BlockSpec((mem_len, d), lambda b, i: (b, 0)),
                  pl.BlockSpec((mem_len, d), lambda b, i: (b, 1)),
                  pl.BlockSpec((1, d), const),
                  pl.BlockSpec((1, d), const),
                  pl.BlockSpec((d, LANES), const),
                  pl.BlockSpec((d, LANES), const),
                  pl.BlockSpec((1, LANES), const)],
        out_specs=(pl.BlockSpec((tm, d), row),
                   pl.BlockSpec((SUBLANES, tm), row),
                   pl.BlockSpec((SUBLANES, tm), row),
                   pl.BlockSpec((SUBLANES, LANES), row)),
        compiler_params=_params("parallel", "parallel"),
        name="cross_attn_router",
    )(x1, w_q.astype(BF16), w_o.astype(BF16), kv, kv, ln_g.reshape(1, d), ln_b.reshape(1, d), rw_hi, rw_lo, rb)


def _to_row_tiles(dst_ref, x, rows, first_row=0):
    for c in range(ROW_TILE):
        dst_ref[pl.ds(first_row * ROW_TILE + c, rows, stride=ROW_TILE), :] = x[:, c * LANES:(c + 1) * LANES]


def _from_row_tiles(src_ref, rows, first_row=0):
    return jnp.concatenate([src_ref[pl.ds(first_row * ROW_TILE + c, rows, stride=ROW_TILE), :]
                            for c in range(ROW_TILE)], axis=-1)


def _tile_rows(row):
    return pl.multiple_of(row * ROW_TILE, ROW_TILE)


def _num_segments(count):
    return lax.shift_right_logical(count + (SEG_ROWS - 1), SEG_ROWS.bit_length() - 1)


def _tile_segments(tcnt_ref):
    return lax.fori_loop(0, N_EXPERTS, lambda e, t: t + _num_segments(tcnt_ref[0, 0, e]), 0)


def _dispatch_kernel(pstart_ref, counts_ref, padded_ref, ldest_ref, tcnt_ref, tbase_ref, lstart_ref, x_ref,
                     xs_hbm, xrt, stage, zeros, nseg, sems, zsem, rsem, *, tm, nt):
    i = pl.program_id(0)
    slot = lax.rem(i, 2)

    @pl.when(i == 0)
    def _():
        stage[...] = jnp.zeros_like(stage)
        zeros[...] = jnp.zeros_like(zeros)

    _to_row_tiles(xrt, x_ref[...], tm)
    st = stage.at[slot]
    for k in range(TOP_K):
        def move(r, c, k=k):
            dst = pl.multiple_of(ldest_ref[0, 0, k * tm + r], ROW_TILE)
            st[pl.ds(dst, ROW_TILE), :] = xrt[pl.ds(_tile_rows(r), ROW_TILE), :]
            return c

        lax.fori_loop(0, tm, move, 0, unroll=32)

    def seg_copy(s, src_row, dst_row):
        return pltpu.make_async_copy(stage.at[s, pl.ds(_tile_rows(src_row), SEG_ROWS * ROW_TILE), :],
                                     xs_hbm.at[pl.ds(_tile_rows(dst_row), SEG_ROWS * ROW_TILE), :], sems.at[s])

    def wait_segments(s, count):
        def wait(j, c):
            seg_copy(s, 0, 0).wait()
            return c

        lax.fori_loop(0, count, wait, 0)

    @pl.when(i > 0)
    def _():
        wait_segments(1 - slot, nseg[1 - slot])

    def issue(e, total):
        n = _num_segments(tcnt_ref[0, 0, e])
        src0 = lstart_ref[0, 0, e]
        dst0 = pstart_ref[e] + tbase_ref[0, 0, e]

        def one(j, c):
            seg_copy(slot, src0 + j * SEG_ROWS, dst0 + j * SEG_ROWS).start()
            return c

        lax.fori_loop(0, n, one, 0)
        return total + n

    total = lax.fori_loop(0, N_EXPERTS, issue, 0)
    nseg[slot] = total

    @pl.when(i == nt - 1)
    def _():
        wait_segments(slot, total)

        def gap(e):
            lo = pstart_ref[e] + counts_ref[e]
            width = padded_ref[e] - counts_ref[e]
            full = lax.shift_right_logical(width, SEG_ROWS.bit_length() - 1)
            return lo, full, width - full * SEG_ROWS

        def seg_zero(row):
            return pltpu.make_async_copy(zeros, xs_hbm.at[pl.ds(_tile_rows(row), SEG_ROWS * ROW_TILE), :], zsem)

        def row_zero(row):
            return pltpu.make_async_copy(zeros.at[pl.ds(0, ROW_TILE), :],
                                         xs_hbm.at[pl.ds(_tile_rows(row), ROW_TILE), :], rsem)

        def fill(e, c):
            lo, full, rest = gap(e)
            lax.fori_loop(0, full, lambda j, c2: (seg_zero(lo + j * SEG_ROWS).start(), c2)[1], 0)
            lax.fori_loop(0, rest, lambda j, c2: (row_zero(lo + full * SEG_ROWS + j).start(), c2)[1], 0)
            return c

        def fill_wait(e, c):
            _, full, rest = gap(e)
            lax.fori_loop(0, full, lambda j, c2: (seg_zero(0).wait(), c2)[1], 0)
            lax.fori_loop(0, rest, lambda j, c2: (row_zero(0).wait(), c2)[1], 0)
            return c

        lax.fori_loop(0, N_EXPERTS, fill, 0)
        lax.fori_loop(0, N_EXPERTS, fill_wait, 0)


def _stage_rows(tm):
    return TOP_K * tm + N_EXPERTS * SEG_ROWS


def _dispatch(x2, pstart, counts, padded, ldest_t, tcnt_t, tbase_t, lstart_t, n_pad, tm):
    n, d = x2.shape
    nt = n // tm
    smem_tile = lambda width: pl.BlockSpec((1, 1, width), lambda i, *_: (i, 0, 0), memory_space=pltpu.SMEM)
    grid_spec = pltpu.PrefetchScalarGridSpec(
        num_scalar_prefetch=3,
        grid=(nt,),
        in_specs=[smem_tile(TOP_K * tm), smem_tile(LANES), smem_tile(LANES), smem_tile(LANES),
                  pl.BlockSpec((tm, d), lambda i, *_: (i, 0))],
        out_specs=pl.BlockSpec(memory_space=pl.ANY),
        scratch_shapes=[pltpu.VMEM((tm * ROW_TILE, LANES), F32),
                        pltpu.VMEM((2, _stage_rows(tm) * ROW_TILE, LANES), F32),
                        pltpu.VMEM((SEG_ROWS * ROW_TILE, LANES), F32),
                        pltpu.SMEM((2,), jnp.int32),
                        pltpu.SemaphoreType.DMA((2,)),
                        pltpu.SemaphoreType.DMA,
                        pltpu.SemaphoreType.DMA],
    )
    return pl.pallas_call(
        functools.partial(_dispatch_kernel, tm=tm, nt=nt),
        out_shape=jax.ShapeDtypeStruct((n_pad * ROW_TILE, LANES), F32),
        grid_spec=grid_spec,
        compiler_params=_params("arbitrary"),
        name="moe_dispatch",
    )(pstart, counts, padded, ldest_t, tcnt_t, tbase_t, lstart_t, x2)


def _expert_kernel(be_ref, valid_ref, xs_hbm, wup_ref, bup_ref, wdn_ref, bdn_ref, y_ref, wup_bf, wdn_bf, ring, sems,
                   *, n_blocks):
    i = pl.program_id(0)
    blk_rows = MOE_BLOCK * ROW_TILE

    def fetch(j):
        slot = lax.rem(j, EXPERT_RING)
        return pltpu.make_async_copy(xs_hbm.at[pl.ds(pl.multiple_of(j * blk_rows, blk_rows), blk_rows), :],
                                     ring.at[slot], sems.at[slot])

    def start_fetch(j):
        @pl.when(jnp.logical_and(j < n_blocks, valid_ref[jnp.minimum(j, n_blocks - 1)] > 0))
        def _():
            fetch(j).start()

    @pl.when(i == 0)
    def _():
        for j in range(EXPERT_RING - 1):
            start_fetch(jnp.int32(j))

    start_fetch(i + EXPERT_RING - 1)

    @pl.when(valid_ref[i] == 0)
    def _():
        y_ref[...] = jnp.zeros_like(y_ref)

    @pl.when(valid_ref[i] > 0)
    def _():
        prev = be_ref[jnp.maximum(i - 1, 0)]

        @pl.when(jnp.logical_or(i == 0, be_ref[i] != prev))
        def _():
            wup_bf[...] = wup_ref[0].astype(BF16)
            wdn_bf[...] = wdn_ref[0].astype(BF16)

        fetch(i).wait()
        xs_ref = ring.at[lax.rem(i, EXPERT_RING)]

        def expert_rows(first_row):
            xb = _from_row_tiles(xs_ref, MOE_HALF, first_row).astype(BF16)
            h = _dot(xb, wup_bf[...]) + bup_ref[0]
            glu = jnp.minimum(h[:, :D_FF], SWIGLU_LIMIT)
            lin = jnp.clip(h[:, D_FF:], -SWIGLU_LIMIT, SWIGLU_LIMIT)
            act = glu * _sigmoid(SWIGLU_ALPHA * glu) * (lin + 1.0)
            _to_row_tiles(y_ref, _dot(act.astype(BF16), wdn_bf[...]) + bdn_ref[0], MOE_HALF, first_row)

        @pl.when(valid_ref[i] > MOE_HALF)
        def _():
            expert_rows(0)
            expert_rows(MOE_HALF)

        @pl.when(valid_ref[i] <= MOE_HALF)
        def _():
            expert_rows(0)
            y_ref[MOE_HALF * ROW_TILE:, :] = jnp.zeros((MOE_HALF * ROW_TILE, LANES), F32)


def _experts(xs, blk_expert, blk_valid, w_up, b_up, w_down, b_down):
    n_blocks = blk_expert.shape[0]
    d = w_up.shape[1]
    f2 = w_up.shape[2]
    blk_rows = MOE_BLOCK * ROW_TILE
    grid_spec = pltpu.PrefetchScalarGridSpec(
        num_scalar_prefetch=2,
        grid=(n_blocks,),
        in_specs=[pl.BlockSpec(memory_space=pl.ANY),
                  pl.BlockSpec((1, d, f2), lambda i, be, va: (be[i], 0, 0)),
                  pl.BlockSpec((1, 1, f2), lambda i, be, va: (be[i], 0, 0)),
                  pl.BlockSpec((1, D_FF, d), lambda i, be, va: (be[i], 0, 0)),
                  pl.BlockSpec((1, 1, d), lambda i, be, va: (be[i], 0, 0))],
        out_specs=pl.BlockSpec((blk_rows, LANES), lambda i, be, va: (i, 0)),
        scratch_shapes=[pltpu.VMEM((d, f2), BF16),
                        pltpu.VMEM((D_FF, d), BF16),
                        pltpu.VMEM((EXPERT_RING, blk_rows, LANES), F32),
                        pltpu.SemaphoreType.DMA((EXPERT_RING,))],
    )
    return pl.pallas_call(
        functools.partial(_expert_kernel, n_blocks=n_blocks),
        out_shape=jax.ShapeDtypeStruct((n_blocks * blk_rows, LANES), F32),
        grid_spec=grid_spec,
        compiler_params=_params("arbitrary"),
        name="moe_experts",
    )(blk_expert, blk_valid, xs, w_up, b_up.reshape(N_EXPERTS, 1, f2), w_down, b_down.reshape(N_EXPERTS, 1, d))


def _combine_ln_kernel(pstart_ref, ldest_ref, gate_ref, tcnt_ref, tbase_ref, lstart_ref,
                       tcnt_next_ref, tbase_next_ref, lstart_next_ref, ys_hbm, x_ref, g_ref, b_ref,
                       o_ref, stage, yrt, sems, *, tm, nt):
    i = pl.program_id(0)
    slot = lax.rem(i, 2)

    def seg_copy(s, src_row, dst_row):
        return pltpu.make_async_copy(ys_hbm.at[pl.ds(_tile_rows(src_row), SEG_ROWS * ROW_TILE), :],
                                     stage.at[s, pl.ds(_tile_rows(dst_row), SEG_ROWS * ROW_TILE), :], sems.at[s])

    def fetch(s, tcnt, tbase, lstart):
        def per_expert(e, c):
            src0 = pstart_ref[e] + tbase[0, 0, e]
            dst0 = lstart[0, 0, e]

            def one(j, c2):
                seg_copy(s, src0 + j * SEG_ROWS, dst0 + j * SEG_ROWS).start()
                return c2

            lax.fori_loop(0, _num_segments(tcnt[0, 0, e]), one, 0)
            return c

        lax.fori_loop(0, N_EXPERTS, per_expert, 0)

    @pl.when(i == 0)
    def _():
        fetch(0, tcnt_ref, tbase_ref, lstart_ref)

    @pl.when(i + 1 < nt)
    def _():
        fetch(1 - slot, tcnt_next_ref, tbase_next_ref, lstart_next_ref)

    def wait(j, c):
        seg_copy(slot, 0, 0).wait()
        return c

    lax.fori_loop(0, _tile_segments(tcnt_ref), wait, 0)

    st = stage.at[slot]

    def reduce(r, c):
        acc = None
        for k in range(TOP_K):
            row = st[pl.ds(pl.multiple_of(ldest_ref[0, 0, k * tm + r], ROW_TILE), ROW_TILE), :]
            term = gate_ref[0, 0, k * tm + r] * row
            acc = term if acc is None else acc + term
        yrt[pl.ds(_tile_rows(r), ROW_TILE), :] = acc
        return c

    lax.fori_loop(0, tm, reduce, 0, unroll=16)
    y = _from_row_tiles(yrt, tm)
    o_ref[...] = _layer_norm(ALPHA * x_ref[...] + y, g_ref[...], b_ref[...])


def _combine_ln(ys, pstart, ldest_t, gates_t, tcnt_t, tbase_t, lstart_t, x2, ln_g, ln_b, tm):
    n, d = x2.shape
    nt = n // tm
    cur = lambda width: pl.BlockSpec((1, 1, width), lambda i, *_: (i, 0, 0), memory_space=pltpu.SMEM)
    nxt = lambda width: pl.BlockSpec((1, 1, width), lambda i, *_: (jnp.minimum(i + 1, nt - 1), 0, 0),
                                     memory_space=pltpu.SMEM)
    grid_spec = pltpu.PrefetchScalarGridSpec(
        num_scalar_prefetch=1,
        grid=(nt,),
        in_specs=[cur(TOP_K * tm), cur(TOP_K * tm), cur(LANES), cur(LANES), cur(LANES),
                  nxt(LANES), nxt(LANES), nxt(LANES),
                  pl.BlockSpec(memory_space=pl.ANY),
                  pl.BlockSpec((tm, d), lambda i, *_: (i, 0)),
                  pl.BlockSpec((1, d), lambda i, *_: (0, 0)),
                  pl.BlockSpec((1, d), lambda i, *_: (0, 0))],
        out_specs=pl.BlockSpec((tm, d), lambda i, *_: (i, 0)),
        scratch_shapes=[pltpu.VMEM((2, _stage_rows(tm) * ROW_TILE, LANES), F32),
                        pltpu.VMEM((tm * ROW_TILE, LANES), F32),
                        pltpu.SemaphoreType.DMA((2,))],
    )
    return pl.pallas_call(
        functools.partial(_combine_ln_kernel, tm=tm, nt=nt),
        out_shape=jax.ShapeDtypeStruct((n, d), F32),
        grid_spec=grid_spec,
        compiler_params=_params("arbitrary"),
        name="moe_combine_ln3",
    )(pstart, ldest_t, gates_t, tcnt_t, tbase_t, lstart_t, tcnt_t, tbase_t, lstart_t, ys, x2,
      ln_g.reshape(1, d), ln_b.reshape(1, d))


def _round_up(v, m):
    return (v + m - 1) // m * m


def _moe_layer(x2, pos_pad, tile_pad, gates_pad, w_up, b_up, w_down, b_down, ln_g, ln_b, tm=MOE_TILE):
    n = x2.shape[0]
    nt = n // tm
    i32 = jnp.int32
    tile_info = tile_pad.reshape(nt, SUBLANES, LANES)
    tcnt = tile_info[:, 0, :N_EXPERTS]
    lstart = tile_info[:, 1, :N_EXPERTS]
    tbase = jnp.cumsum(tcnt, axis=0) - tcnt
    counts = jnp.sum(tcnt, axis=0)
    padded = _round_up(counts + SEG_ROWS, MOE_BLOCK)
    pend = jnp.cumsum(padded)
    pstart = pend - padded
    n_blocks = -(-(n * TOP_K + N_EXPERTS * (SEG_ROWS + MOE_BLOCK - 1)) // MOE_BLOCK)
    block_start = jnp.arange(n_blocks, dtype=i32) * MOE_BLOCK
    blk_expert = jnp.minimum(jnp.sum(pend[None, :] <= block_start[:, None], axis=1), N_EXPERTS - 1).astype(i32)
    blk_valid = jnp.clip(counts[blk_expert] - (block_start - pstart[blk_expert]), 0, MOE_BLOCK).astype(i32)
    per_tile = lambda a: a.reshape(nt, SUBLANES, tm)[:, :TOP_K, :].reshape(nt, 1, TOP_K * tm)
    ldest_t = per_tile(pos_pad)
    gates_t = per_tile(gates_pad)
    lane_pad = lambda a: jnp.pad(a.astype(i32), ((0, 0), (0, LANES - N_EXPERTS))).reshape(nt, 1, LANES)
    tcnt_t, tbase_t, lstart_t = lane_pad(tcnt), lane_pad(tbase), lane_pad(lstart)
    pstart, counts, padded = pstart.astype(i32), counts.astype(i32), padded.astype(i32)

    xs = _dispatch(x2, pstart, counts, padded, ldest_t, tcnt_t, tbase_t, lstart_t, n_blocks * MOE_BLOCK, tm)
    ys = _experts(xs, blk_expert, blk_valid, w_up, b_up, w_down, b_down)
    return _combine_ln(ys, pstart, ldest_t, gates_t, tcnt_t, tbase_t, lstart_t, x2, ln_g, ln_b, tm)


def _s5_kernel(x_ref, win_ref, bblk_ref, ar_ref, ai_ref, cblk_ref, dsk_ref, wval_ref, wgate_ref, g_ref, b_ref,
               o_ref, bur, bui, sr, si, *, tt, batch):
    rows = tt * batch

    @pl.when(pl.program_id(0) == 0)
    def _():
        sr[...] = jnp.zeros_like(sr)
        si[...] = jnp.zeros_like(si)

    x = x_ref[...]
    u = _dot(x.astype(BF16), win_ref[...])
    ub = u.astype(BF16)
    for c in range(S5_N_CHUNKS):
        bu = _dot(ub[:, c * S5_CHUNK_IN:(c + 1) * S5_CHUNK_IN], bblk_ref[c])
        bur[:, c * S5_CHUNK_STATE:(c + 1) * S5_CHUNK_STATE] = bu[:, :S5_CHUNK_STATE]
        bui[:, c * S5_CHUNK_STATE:(c + 1) * S5_CHUNK_STATE] = bu[:, S5_CHUNK_STATE:]

    for c in range(S5_N_CHUNKS):
        cols = pl.ds(c * S5_CHUNK_STATE, S5_CHUNK_STATE)
        a_r = ar_ref[:, cols]
        a_i = ai_ref[:, cols]

        def step(t, carry):
            s_r, s_i = carry
            rsl = pl.ds(pl.multiple_of(t * batch, batch), batch)
            n_r = a_r * s_r - a_i * s_i + bur[rsl, cols]
            n_i = a_r * s_i + a_i * s_r + bui[rsl, cols]
            bur[rsl, cols] = n_r
            bui[rsl, cols] = n_i
            return n_r, n_i

        f_r, f_i = lax.fori_loop(0, tt, step, (sr[:, cols], si[:, cols]), unroll=True)
        sr[:, cols] = f_r
        si[:, cols] = f_i

    ys = []
    for c in range(S5_N_CHUNKS):
        cols = pl.ds(c * S5_CHUNK_STATE, S5_CHUNK_STATE)
        xri = jnp.concatenate([bur[:, cols].astype(BF16), bui[:, cols].astype(BF16)], axis=-1)
        ys.append(_dot(xri, cblk_ref[c]))
    y = jnp.concatenate(ys, axis=-1) + dsk_ref[...] * u
    yb = _gelu(y).astype(BF16)
    hmix = _dot(yb, wval_ref[...]) * _sigmoid(_dot(yb, wgate_ref[...]))
    o_ref[...] = _layer_norm(ALPHA * x + hmix, g_ref[...], b_ref[...])


def _s5_discretize(log_dt, lambda_re, lambda_im, b_re, b_im, c_re, c_im):
    dt = jnp.exp(log_dt)[:, None]
    mag = jnp.exp(lambda_re * dt)
    ar = mag * jnp.cos(lambda_im * dt)
    ai = mag * jnp.sin(lambda_im * dt)
    den = lambda_re * lambda_re + lambda_im * lambda_im
    zr = ((ar - 1.0) * lambda_re + ai * lambda_im) / den
    zi = (ai * lambda_re - (ar - 1.0) * lambda_im) / den
    bbar_re = zr[..., None] * b_re - zi[..., None] * b_im
    bbar_im = zr[..., None] * b_im + zi[..., None] * b_re
    ng, gc, p, gw = S5_N_CHUNKS, S5_CHUNK_GROUPS, S5_STATE, S5_GROUP
    eye = jnp.eye(gc, dtype=F32)
    bre = bbar_re.reshape(ng, gc, p, gw).transpose(0, 1, 3, 2)
    bim = bbar_im.reshape(ng, gc, p, gw).transpose(0, 1, 3, 2)
    blk_re = jnp.einsum('cgip,gh->cgihp', bre, eye).reshape(ng, gc * gw, gc * p)
    blk_im = jnp.einsum('cgip,gh->cgihp', bim, eye).reshape(ng, gc * gw, gc * p)
    bblk = jnp.concatenate([blk_re, blk_im], axis=-1).astype(BF16)
    cre = c_re.reshape(ng, gc, gw, p).transpose(0, 1, 3, 2)
    cim = c_im.reshape(ng, gc, gw, p).transpose(0, 1, 3, 2)
    cblk_re = jnp.einsum('cgpi,gh->cgphi', cre, eye).reshape(ng, gc * p, gc * gw)
    cblk_im = jnp.einsum('cgpi,gh->cgphi', cim, eye).reshape(ng, gc * p, gc * gw)
    cblk = jnp.concatenate([cblk_re, -cblk_im], axis=1).astype(BF16)
    return bblk, ar.reshape(1, S5_STATES), ai.reshape(1, S5_STATES), cblk


def _s5_mixer_ln(x_tm, w_in, log_dt, lambda_re, lambda_im, b_re, b_im, c_re, c_im, d_skip, w_val, w_gate,
                 ln_g, ln_b, batch, tt=16):
    n, d = x_tm.shape
    rows = tt * batch
    bblk, ar, ai, cblk = _s5_discretize(log_dt, lambda_re, lambda_im, b_re, b_im, c_re, c_im)
    const2 = lambda i: (0, 0)
    const3 = lambda i: (0, 0, 0)
    row = lambda i: (i, 0)
    return pl.pallas_call(
        functools.partial(_s5_kernel, tt=tt, batch=batch),
        out_shape=jax.ShapeDtypeStruct((n, d), F32),
        grid=(n // rows,),
        in_specs=[pl.BlockSpec((rows, d), row),
                  pl.BlockSpec((d, d), const2),
                  pl.BlockSpec(bblk.shape, const3),
                  pl.BlockSpec(ar.shape, const2),
                  pl.BlockSpec(ai.shape, const2),
                  pl.BlockSpec(cblk.shape, const3),
                  pl.BlockSpec((1, d), const2),
                  pl.BlockSpec((d, d), const2),
                  pl.BlockSpec((d, d), const2),
                  pl.BlockSpec((1, d), const2),
                  pl.BlockSpec((1, d), const2)],
        out_specs=pl.BlockSpec((rows, d), row),
        scratch_shapes=[pltpu.VMEM((rows, S5_STATES), F32),
                        pltpu.VMEM((rows, S5_STATES), F32),
                        pltpu.VMEM((batch, S5_STATES), F32),
                        pltpu.VMEM((batch, S5_STATES), F32)],
        compiler_params=_params("arbitrary"),
        name="s5_mixer_ln1",
    )(x_tm, w_in.astype(BF16), bblk, ar, ai, cblk, d_skip.reshape(1, d), w_val.astype(BF16), w_gate.astype(BF16),
      ln_g.reshape(1, d), ln_b.reshape(1, d))


def _diff_lambda_init(layer_idx):
    return 0.8 - 0.6 * math.exp(-0.3 * layer_idx)


def kernel(x, mem, w_mem_kv, l0_w_in, l0_sgu_ln_g, l0_sgu_ln_b, l0_w_spatial, l0_b_spatial, l0_lam_q1, l0_lam_k1, l0_lam_q2, l0_lam_k2, l0_subln_g, l0_w_out, l0_ln1_g, l0_ln1_b, l0_xq, l0_xo, l0_ln2_g, l0_ln2_b, l0_router_w, l0_router_b, l0_exp_w_up, l0_exp_b_up, l0_exp_w_down, l0_exp_b_down, l0_ln3_g, l0_ln3_b, l1_w_in, l1_log_dt, l1_lambda_re, l1_lambda_im, l1_b_re, l1_b_im, l1_c_re, l1_c_im, l1_d_skip, l1_w_val, l1_w_gate, l1_ln1_g, l1_ln1_b, l1_xq, l1_xo, l1_ln2_g, l1_ln2_b, l1_router_w, l1_router_b, l1_exp_w_up, l1_exp_b_up, l1_exp_w_down, l1_exp_b_down, l1_ln3_g, l1_ln3_b):
    batch, seq, d = x.shape
    mem_len = mem.shape[1]
    n = batch * seq
    x0 = x.reshape(n, d)

    kv = _matmul(mem.reshape(batch * mem_len, d), w_mem_kv.astype(BF16), BF16,
                 tm=min(512, batch * mem_len), tn=d)

    a, qk, vt = _inproj_sgu(x0, l0_w_in, l0_sgu_ln_g, l0_sgu_ln_b, l0_w_spatial, l0_b_spatial)
    dattn = _diff_attention(qk, vt, l0_lam_q1, l0_lam_k1, l0_lam_q2, l0_lam_k2, l0_subln_g,
                            _diff_lambda_init(0), batch, seq)
    x1 = _outproj_ln(a, dattn, l0_w_out, x0, l0_ln1_g, l0_ln1_b)
    x2, gates, pos, tile_info = _cross_router(x1, kv, l0_xq, l0_xo, l0_ln2_g, l0_ln2_b, l0_router_w, l0_router_b,
                                              batch, seq, mem_len)
    x3 = _moe_layer(x2, pos, tile_info, gates, l0_exp_w_up, l0_exp_b_up, l0_exp_w_down, l0_exp_b_down,
                    l0_ln3_g, l0_ln3_b)

    x3_tm = x3.reshape(batch, seq, d).transpose(1, 0, 2).reshape(n, d)
    x4_tm = _s5_mixer_ln(x3_tm, l1_w_in, l1_log_dt, l1_lambda_re, l1_lambda_im, l1_b_re, l1_b_im,
                         l1_c_re, l1_c_im, l1_d_skip, l1_w_val, l1_w_gate, l1_ln1_g, l1_ln1_b, batch)
    x4 = x4_tm.reshape(seq, batch, d).transpose(1, 0, 2).reshape(n, d)
    x5, gates, pos, tile_info = _cross_router(x4, kv, l1_xq, l1_xo, l1_ln2_g, l1_ln2_b, l1_router_w, l1_router_b,
                                              batch, seq, mem_len)
    x6 = _moe_layer(x5, pos, tile_info, gates, l1_exp_w_up, l1_exp_b_up, l1_exp_w_down, l1_exp_b_down,
                    l1_ln3_g, l1_ln3_b)
    return x6.reshape(batch, seq, d)
```
